```python
import jax, jax.numpy as jnp
from jax import lax
import numpy as np

D_MODEL = 1024
BATCH = 8
SEQ = 4096
DEPTH = 1

D_MIX = D_MODEL
D_CONV = D_MIX // 2
D_REC = D_MIX - D_CONV
CONV_WIDTH = 31
REC_EXPAND = 128
REC_HEADS = D_REC // REC_EXPAND
REC_DK = REC_EXPAND
REC_DV = D_REC // REC_HEADS
CHUNK = 64
D_IN = 2 * D_CONV + 4 * D_REC
IN_SPLITS = (D_CONV, 2 * D_CONV, 2 * D_CONV + D_REC, 2 * D_CONV + 2 * D_REC, 2 * D_CONV + 3 * D_REC)
N_GROUPS = 4
EXPERTS_PER_GROUP = 8
N_EXPERTS = N_GROUPS * EXPERTS_PER_GROUP
TOP_K = 2
D_EXPERT = D_MODEL // 4
EXPERT_BLOCK = 128
EPS = 1e-6

kernel_name = "hybrid_conv_hgrn2_hmoe_adaln"


def rms_norm(x, gain):
    xf = x.astype(jnp.float32)
    y = xf * lax.rsqrt(jnp.mean(xf * xf, axis=-1, keepdims=True) + EPS)
    return (y * gain.astype(jnp.float32)).astype(x.dtype)


def layer_norm(x, gain, bias):
    xf = x.astype(jnp.float32)
    mu = jnp.mean(xf, axis=-1, keepdims=True)
    xc = xf - mu
    y = xc * lax.rsqrt(jnp.mean(xc * xc, axis=-1, keepdims=True) + EPS)
    return (y * gain.astype(jnp.float32) + bias.astype(jnp.float32)).astype(x.dtype)


def conv_module(val, gate, dw_kernel, dw_bias, ln_gain, ln_bias):
    u = val * jax.nn.sigmoid(gate)
    u = lax.conv_general_dilated(
        u, dw_kernel[:, None, :].astype(u.dtype), window_strides=(1,),
        padding=[(CONV_WIDTH - 1, 0)],
        dimension_numbers=('NWC', 'WIO', 'NWC'),
        feature_group_count=D_CONV) + dw_bias.astype(u.dtype)
    u = layer_norm(u, ln_gain, ln_bias)
    return jax.nn.silu(u)


def hgrn2(q, f_logit, i, g, lb, norm_gain):
    B, S, _ = q.shape
    N = S // CHUNK
    out_dtype = q.dtype
    f32 = jnp.float32
    qf = jax.nn.silu(q.astype(f32))
    forget = lb + (1.0 - lb) * jax.nn.sigmoid(f_logit.astype(f32))
    k = 1.0 - forget
    log_f = jnp.log(forget)

    def heads(t, d):
        return t.reshape(B, N, CHUNK, REC_HEADS, d).transpose(0, 3, 1, 2, 4)

    qh, kh, lfh = heads(qf, REC_DK), heads(k, REC_DK), heads(log_f, REC_DK)
    vh = heads(i.astype(f32), REC_DV)
    b = jnp.cumsum(lfh, axis=3)
    b_last = b[:, :, :, -1:, :]
    q_dec = qh * jnp.exp(b)
    k_inv = kh * jnp.exp(-b)
    scores = jnp.einsum('bhncd,bhnsd->bhncs', q_dec, k_inv)
    causal = jnp.tril(jnp.ones((CHUNK, CHUNK), dtype=bool))
    scores = jnp.where(causal, scores, 0.0)
    o_intra = jnp.einsum('bhncs,bhnse->bhnce', scores, vh)

    k_end = kh * jnp.exp(b_last - b)
    chunk_state = jnp.einsum('bhncd,bhnce->nbhde', k_end, vh)
    chunk_decay = jnp.exp(b_last[:, :, :, 0, :]).transpose(2, 0, 1, 3)

    def step(state, inp):
        d_state, a = inp
        return state * a[..., None] + d_state, state

    s0 = jnp.zeros((B, REC_HEADS, REC_DK, REC_DV), f32)
    _, s_prev = lax.scan(step, s0, (chunk_state, chunk_decay))
    o_inter = jnp.einsum('bhncd,nbhde->bhnce', q_dec, s_prev)
    o = rms_norm(o_intra + o_inter, norm_gain)
    o = o.transpose(0, 2, 3, 1, 4).reshape(B, S, D_REC)
    return (o * jax.nn.silu(g.astype(f32))).astype(out_dtype)


def hier_moe(h, w_rg, b_rg, w_re, b_re, w_gate, w_up, w_down):
    B, S, D = h.shape
    T = B * S
    dt = h.dtype
    ht = h.reshape(T, D)
    f32 = jnp.float32
    group_p = jax.nn.softmax((ht @ w_rg).astype(f32) + b_rg.astype(f32), axis=-1)
    g_prob, g_idx = lax.top_k(group_p, 1)
    e_logit = ((ht @ w_re).astype(f32) + b_re.astype(f32)).reshape(T, N_GROUPS, EXPERTS_PER_GROUP)
    e_logit = jnp.take_along_axis(e_logit, g_idx[:, :, None], axis=1)[:, 0]
    e_p = jax.nn.softmax(e_logit, axis=-1)
    e_prob, e_local = lax.top_k(e_p, TOP_K)
    e_prob = e_prob / jnp.sum(e_prob, axis=-1, keepdims=True)
    weights = (g_prob * e_prob).reshape(-1)
    expert_ids = (g_idx * EXPERTS_PER_GROUP + e_local).reshape(-1)
    A = T * TOP_K
    token_ids = jnp.arange(A) // TOP_K

    order = jnp.argsort(expert_ids)
    sorted_e = expert_ids[order]
    sorted_tok = token_ids[order]
    counts = jnp.bincount(expert_ids, length=N_EXPERTS)
    padded = (counts + EXPERT_BLOCK - 1) // EXPERT_BLOCK * EXPERT_BLOCK
    pad_end = jnp.cumsum(padded)
    pad_start = pad_end - padded
    start = jnp.cumsum(counts) - counts
    dest = pad_start[sorted_e] + (jnp.arange(A) - start[sorted_e])
    n_rows = -(-A // EXPERT_BLOCK) * EXPERT_BLOCK + N_EXPERTS * EXPERT_BLOCK
    n_blocks = n_rows // EXPERT_BLOCK
    x_pad = jnp.zeros((n_rows, D), dt).at[dest].set(ht[sorted_tok])
    block_start = jnp.arange(n_blocks) * EXPERT_BLOCK
    block_expert = jnp.minimum(jnp.sum(block_start[:, None] >= pad_end[None, :], axis=1), N_EXPERTS - 1)

    def expert_block(args):
        xb, e = args
        hb = jax.nn.silu(xb @ w_gate[e]) * (xb @ w_up[e])
        return hb @ w_down[e]

    y_pad = lax.map(expert_block, (x_pad.reshape(n_blocks, EXPERT_BLOCK, D), block_expert))
    y_sorted = y_pad.reshape(n_rows, D)[dest] * weights[order][:, None].astype(dt)
    out = jnp.zeros((T, D), dt).at[sorted_tok].add(y_sorted)
    return out.reshape(B, S, D)


def setup_inputs(seed: int = 0) -> dict:
    key = jax.random.key(seed)
    ks = jax.random.split(key, 24)
    nrm = jax.random.normal
    f32 = jnp.float32
    L, D = DEPTH, D_MODEL
    return {
        "x": nrm(ks[0], (BATCH, SEQ, D), f32),
        "c": nrm(ks[1], (BATCH, D), f32),
        "w_ada": nrm(ks[2], (L, D, 6 * D), f32) * (0.5 * D ** -0.5),
        "b_ada": nrm(ks[3], (L, 6 * D), f32) * 0.01,
        "g_pre_mix": 1.0 + 0.01 * nrm(ks[4], (L, D), f32),
        "g_post_mix": 1.0 + 0.01 * nrm(ks[5], (L, D), f32),
        "w_in": nrm(ks[6], (L, D, D_IN), f32) * D ** -0.5,
        "dw_kernel": nrm(ks[7], (L, CONV_WIDTH, D_CONV), f32) * CONV_WIDTH ** -0.5,
        "dw_bias": nrm(ks[8], (L, D_CONV), f32) * 0.01,
        "conv_ln_gain": 1.0 + 0.01 * nrm(ks[9], (L, D_CONV), f32),
        "conv_ln_bias": nrm(ks[10], (L, D_CONV), f32) * 0.01,
        "lb_logits": nrm(ks[11], (L + 1, D_REC), f32) * 0.1,
        "rec_norm_gain": 1.0 + 0.01 * nrm(ks[12], (L, REC_DV), f32),
        "w_out": nrm(ks[13], (L, D_MIX, D), f32) * D_MIX ** -0.5,
        "g_pre_ffn": 1.0 + 0.01 * nrm(ks[14], (L, D), f32),
        "g_post_ffn": 1.0 + 0.01 * nrm(ks[15], (L, D), f32),
        "w_router_group": nrm(ks[16], (L, D, N_GROUPS), f32) * D ** -0.5,
        "b_router_group": nrm(ks[17], (L, N_GROUPS), f32) * 0.01,
        "w_router_expert": nrm(ks[18], (L, D, N_EXPERTS), f32) * D ** -0.5,
        "b_router_expert": nrm(ks[19], (L, N_EXPERTS), f32) * 0.01,
        "w_gate": nrm(ks[20], (L, N_EXPERTS, D, D_EXPERT), f32) * D ** -0.5,
        "w_up": nrm(ks[21], (L, N_EXPERTS, D, D_EXPERT), f32) * D ** -0.5,
        "w_down": nrm(ks[22], (L, N_EXPERTS, D_EXPERT, D), f32) * D_EXPERT ** -0.5,
    }


def reference(x, c, w_ada, b_ada, g_pre_mix, g_post_mix, w_in, dw_kernel, dw_bias,
              conv_ln_gain, conv_ln_bias, lb_logits, rec_norm_gain, w_out, g_pre_ffn,
              g_post_ffn, w_router_group, b_router_group, w_router_expert, b_router_expert,
              w_gate, w_up, w_down):
    cond = jax.nn.silu(c)
    lb_table = jnp.cumsum(jax.nn.softmax(lb_logits.astype(jnp.float32), axis=0), axis=0)
    for l in range(DEPTH):
        mod = (cond @ w_ada[l] + b_ada[l])[:, None, :]
        sh1, sc1, gt1, sh2, sc2, gt2 = jnp.split(mod, 6, axis=-1)

        h = rms_norm(x, g_pre_mix[l]) * (1.0 + sc1) + sh1
        proj = h @ w_in[l]
        cv, cg, q, f, i, g = jnp.split(proj, IN_SPLITS, axis=-1)
        y_conv = conv_module(cv, cg, dw_kernel[l], dw_bias[l], conv_ln_gain[l], conv_ln_bias[l])
        y_rec = hgrn2(q, f, i, g, lb_table[l], rec_norm_gain[l])
        y = jnp.concatenate([y_conv, y_rec], axis=-1) @ w_out[l]
        x = x + gt1 * rms_norm(y, g_post_mix[l])

        h = rms_norm(x, g_pre_ffn[l]) * (1.0 + sc2) + sh2
        y = hier_moe(h, w_router_group[l], b_router_group[l], w_router_expert[l],
                     b_router_expert[l], w_gate[l], w_up[l], w_down[l])
        x = x + gt2 * rms_norm(y, g_post_ffn[l])
    return x
```

```python
from functools import partial

import jax
import jax.numpy as jnp
from jax import lax
from jax.experimental import pallas as pl
from jax.experimental.pallas import tpu as pltpu

D_MODEL = 1024
D_CONV = 512
D_REC = 512
CONV_WIDTH = 31
REC_HEADS = 4
REC_DK = 128
REC_DV = 128
CHUNK = 64
D_IN = 2 * D_CONV + 4 * D_REC
N_GROUPS = 4
EXPERTS_PER_GROUP = 8
N_EXPERTS = 32
D_EXPERT = 256
EPS = 1e-6

LANES = 128
SEQ_TILE = 512
CONV_ROWS = 64
CONV_HALO = 32
MOE_TILE = 1024
GROUP_LANE0 = 32
VMEM_LIMIT = 56 * 1024 * 1024

F32 = jnp.float32
BF16 = jnp.bfloat16
HI = lax.Precision.HIGHEST


def _sigmoid(v):
    return 1.0 / (1.0 + jnp.exp(-v))


def _silu(v):
    return v * _sigmoid(v)


def _rms(v, gain):
    return v * lax.rsqrt(jnp.mean(v * v, axis=-1, keepdims=True) + EPS) * gain


def _ada_kernel(c_ref, w_ref, b_ref, o_ref):
    cond = _silu(c_ref[...])
    o_ref[...] = jnp.dot(cond, w_ref[...], precision=HI, preferred_element_type=F32) + b_ref[...]


def _ada(c, w, b):
    bsz, d = c.shape
    n = w.shape[1]
    return pl.pallas_call(
        _ada_kernel,
        grid=(n // d,),
        in_specs=[
            pl.BlockSpec((bsz, d), lambda j: (0, 0)),
            pl.BlockSpec((d, d), lambda j: (0, j)),
            pl.BlockSpec((1, d), lambda j: (0, j)),
        ],
        out_specs=pl.BlockSpec((bsz, d), lambda j: (0, j)),
        out_shape=jax.ShapeDtypeStruct((bsz, n), F32),
        name="ada_mod",
    )(c, w, b.reshape(1, n))


def _mix_kernel(x_ref, mod_ref, gpre_ref, gpost_ref, gffn_ref, win_ref, dwk_ref, dwb_ref,
                lng_ref, lnb_ref, lbl_ref, rng_ref, wout_ref, wr_ref, br_ref,
                x1_ref, h2_ref, wd_ref,
                ubuf, qf_s, k_s, lf_s, v_s, sg_s, yb, state):
    j = pl.program_id(1)
    t = SEQ_TILE

    @pl.when(j == 0)
    def _():
        ubuf[0:CONV_HALO, :] = jnp.zeros((CONV_HALO, D_CONV), F32)
        state[...] = jnp.zeros(state.shape, F32)

    x = x_ref[0]
    sh1 = mod_ref[0, 0:1, :]
    sc1 = mod_ref[0, 1:2, :]
    gt1 = mod_ref[0, 2:3, :]
    sh2 = mod_ref[0, 3:4, :]
    sc2 = mod_ref[0, 4:5, :]

    h = _rms(x, gpre_ref[...]) * (1.0 + sc1) + sh1
    proj = jnp.dot(h.astype(BF16), win_ref[...], preferred_element_type=F32)

    cv = proj[:, 0:D_CONV]
    cg = proj[:, D_CONV:2 * D_CONV]
    ubuf[CONV_HALO:CONV_HALO + t, :] = cv * _sigmoid(cg)
    off = CONV_HALO - (CONV_WIDTH - 1)
    for ci in range(t // CONV_ROWS):
        r0 = ci * CONV_ROWS
        acc = jnp.broadcast_to(dwb_ref[...], (CONV_ROWS, D_CONV))
        for kk in range(CONV_WIDTH):
            acc = acc + ubuf[r0 + off + kk:r0 + off + kk + CONV_ROWS, :] * dwk_ref[kk:kk + 1, :]
        mu = jnp.mean(acc, axis=-1, keepdims=True)
        xc = acc - mu
        yc = xc * lax.rsqrt(jnp.mean(xc * xc, axis=-1, keepdims=True) + EPS) * lng_ref[...] + lnb_ref[...]
        yb[r0:r0 + CONV_ROWS, 0:D_CONV] = _silu(yc).astype(BF16)
    ubuf[0:CONV_HALO, :] = ubuf[t:t + CONV_HALO, :]

    q = proj[:, 2 * D_CONV:2 * D_CONV + D_REC]
    f = proj[:, 2 * D_CONV + D_REC:2 * D_CONV + 2 * D_REC]
    iv = proj[:, 2 * D_CONV + 2 * D_REC:2 * D_CONV + 3 * D_REC]
    g = proj[:, 2 * D_CONV + 3 * D_REC:2 * D_CONV + 4 * D_REC]
    l0 = lbl_ref[0:1, :]
    lmax = jnp.max(lbl_ref[...], axis=0, keepdims=True)
    lb = jnp.exp(l0 - lmax) / jnp.sum(jnp.exp(lbl_ref[...] - lmax), axis=0, keepdims=True)
    forget = lb + (1.0 - lb) * _sigmoid(f)
    qf_s[...] = _silu(q)
    k_s[...] = 1.0 - forget
    lf_s[...] = jnp.log(forget)
    v_s[...] = iv
    sg_s[...] = _silu(g)

    row = lax.broadcasted_iota(jnp.int32, (CHUNK, CHUNK), 0)
    col = lax.broadcasted_iota(jnp.int32, (CHUNK, CHUNK), 1)
    causal = row >= col
    tri = causal.astype(F32)
    nt_dims = (((1,), (1,)), ((), ()))

    def chunk_body(ci, carry):
        r0 = pl.multiple_of(ci * CHUNK, CHUNK)
        rows = pl.ds(r0, CHUNK)
        for hd in range(REC_HEADS):
            cols = slice(hd * REC_DK, (hd + 1) * REC_DK)
            lf_c = lf_s[rows, cols]
            k_c = k_s[rows, cols]
            v_c = v_s[rows, cols].astype(BF16)
            bcum = jnp.dot(tri, lf_c, precision=HI, preferred_element_type=F32)
            blast = bcum[CHUNK - 1:CHUNK, :]
            q_dec = (qf_s[rows, cols] * jnp.exp(bcum)).astype(BF16)
            k_inv = (k_c * jnp.exp(-bcum)).astype(BF16)
            k_end = k_c * jnp.exp(blast - bcum)
            scores = lax.dot_general(q_dec, k_inv, nt_dims, preferred_element_type=F32)
            scores = jnp.where(causal, scores, 0.0).astype(BF16)
            st = state[hd]
            o = jnp.dot(scores, v_c, preferred_element_type=F32)
            o = o + lax.dot_general(q_dec, st.astype(BF16), nt_dims, preferred_element_type=F32)
            upd = jnp.dot(v_s[rows, cols].T.astype(BF16), k_end.astype(BF16),
                          preferred_element_type=F32)
            state[hd] = st * jnp.exp(blast) + upd
            o = _rms(o, rng_ref[...]) * sg_s[rows, cols]
            yb[rows, D_CONV + hd * REC_DV:D_CONV + (hd + 1) * REC_DV] = o.astype(BF16)
        return carry

    lax.fori_loop(0, t // CHUNK, chunk_body, 0)

    y = jnp.dot(yb[...], wout_ref[...], preferred_element_type=F32)
    x1 = x + gt1 * _rms(y, gpost_ref[...])
    x1_ref[0] = x1

    h2 = _rms(x1, gffn_ref[...]) * (1.0 + sc2) + sh2
    h2_ref[0] = h2.astype(BF16)
    logits = jnp.dot(h2, wr_ref[...], precision=HI, preferred_element_type=F32) + br_ref[...]
    lane = lax.broadcasted_iota(jnp.int32, (t, LANES), 1)
    neg = jnp.float32(-jnp.inf)
    is_group = (lane >= GROUP_LANE0) & (lane < GROUP_LANE0 + N_GROUPS)
    gl = jnp.where(is_group, logits, neg)
    gmax = jnp.max(gl, axis=-1, keepdims=True)
    gidx = jnp.min(jnp.where(gl == gmax, lane - GROUP_LANE0, LANES), axis=-1, keepdims=True)
    gprob = 1.0 / jnp.sum(jnp.exp(gl - gmax), axis=-1, keepdims=True)
    in_group = (lane < N_EXPERTS) & ((lane // EXPERTS_PER_GROUP) == gidx)
    el = jnp.where(in_group, logits, neg)
    m1 = jnp.max(el, axis=-1, keepdims=True)
    i1 = jnp.min(jnp.where(el == m1, lane, LANES), axis=-1, keepdims=True)
    el2 = jnp.where(lane == i1, neg, el)
    m2 = jnp.max(el2, axis=-1, keepdims=True)
    i2 = jnp.min(jnp.where(el2 == m2, lane, LANES), axis=-1, keepdims=True)
    r = jnp.exp(m2 - m1)
    w1 = gprob / (1.0 + r)
    w2 = gprob * r / (1.0 + r)
    wd_ref[0] = jnp.where(lane == i1, w1, jnp.where(lane == i2, w2, 0.0))


def _mix(x, mod3, g_pre, g_post, g_ffn, w_in, dwk, dwb, lng, lnb, lbl, rng, w_out, w_r, b_r):
    bsz, s, d = x.shape
    t = SEQ_TILE
    grid = (bsz, s // t)
    tile = lambda b, j: (b, j, 0)
    const2 = lambda b, j: (0, 0)
    return pl.pallas_call(
        _mix_kernel,
        grid=grid,
        in_specs=[
            pl.BlockSpec((1, t, d), tile),
            pl.BlockSpec((1, 6, d), lambda b, j: (b, 0, 0)),
            pl.BlockSpec((1, d), const2),
            pl.BlockSpec((1, d), const2),
            pl.BlockSpec((1, d), const2),
            pl.BlockSpec((d, D_IN), const2),
            pl.BlockSpec((CONV_WIDTH, D_CONV), const2),
            pl.BlockSpec((1, D_CONV), const2),
            pl.BlockSpec((1, D_CONV), const2),
            pl.BlockSpec((1, D_CONV), const2),
            pl.BlockSpec((2, D_REC), const2),
            pl.BlockSpec((1, REC_DV), const2),
            pl.BlockSpec((d, d), const2),
            pl.BlockSpec((d, LANES), const2),
            pl.BlockSpec((1, LANES), const2),
        ],
        out_specs=[
            pl.BlockSpec((1, t, d), tile),
            pl.BlockSpec((1, t, d), tile),
            pl.BlockSpec((1, t, LANES), tile),
        ],
        out_shape=[
            jax.ShapeDtypeStruct((bsz, s, d), F32),
            jax.ShapeDtypeStruct((bsz, s, d), BF16),
            jax.ShapeDtypeStruct((bsz, s, LANES), F32),
        ],
        scratch_shapes=[
            pltpu.VMEM((CONV_HALO + t, D_CONV), F32),
            pltpu.VMEM((t, D_REC), F32),
            pltpu.VMEM((t, D_REC), F32),
            pltpu.VMEM((t, D_REC), F32),
            pltpu.VMEM((t, D_REC), F32),
            pltpu.VMEM((t, D_REC), F32),
            pltpu.VMEM((t, d), BF16),
            pltpu.VMEM((REC_HEADS, REC_DV, REC_DK), F32),
        ],
        compiler_params=pltpu.CompilerParams(
            dimension_semantics=("arbitrary", "arbitrary"),
            vmem_limit_bytes=VMEM_LIMIT),
        name="mixer",
    )(x, mod3, g_pre, g_post, g_ffn, w_in, dwk, dwb, lng, lnb, lbl, rng, w_out, w_r, b_r)


def _moe_kernel(h2_ref, wd_ref, x1_ref, gt2_ref, gpost_ref, wg_ref, wu_ref, wdn_ref, o_ref, acc):
    e = pl.program_id(1)

    @pl.when(e == 0)
    def _():
        acc[...] = jnp.zeros(acc.shape, F32)

    xb = h2_ref[...]
    lane = lax.broadcasted_iota(jnp.int32, wd_ref.shape, 1)
    wcol = jnp.sum(jnp.where(lane == e, wd_ref[...], 0.0), axis=-1, keepdims=True)
    gate = jnp.dot(xb, wg_ref[0], preferred_element_type=F32)
    up = jnp.dot(xb, wu_ref[0], preferred_element_type=F32)
    hb = (_silu(gate) * up).astype(BF16)
    acc[...] += jnp.dot(hb, wdn_ref[0], preferred_element_type=F32) * wcol

    @pl.when(e == pl.num_programs(1) - 1)
    def _():
        o_ref[...] = x1_ref[...] + gt2_ref[0] * _rms(acc[...], gpost_ref[...])


def _moe(h2, wdense, x1, gt2, g_post, w_gate, w_up, w_down, seq):
    n, d = h2.shape
    tm = MOE_TILE
    per_batch = seq // tm
    tok = lambda i, e: (i, 0)
    return pl.pallas_call(
        _moe_kernel,
        grid=(n // tm, N_EXPERTS),
        in_specs=[
            pl.BlockSpec((tm, d), tok),
            pl.BlockSpec((tm, LANES), tok),
            pl.BlockSpec((tm, d), tok),
            pl.BlockSpec((1, 1, d), lambda i, e: (i // per_batch, 0, 0)),
            pl.BlockSpec((1, d), lambda i, e: (0, 0)),
            pl.BlockSpec((1, d, D_EXPERT), lambda i, e: (e, 0, 0)),
            pl.BlockSpec((1, d, D_EXPERT), lambda i, e: (e, 0, 0)),
            pl.BlockSpec((1, D_EXPERT, d), lambda i, e: (e, 0, 0)),
        ],
        out_specs=pl.BlockSpec((tm, d), tok),
        out_shape=jax.ShapeDtypeStruct((n, d), F32),
        scratch_shapes=[pltpu.VMEM((tm, d), F32)],
        compiler_params=pltpu.CompilerParams(
            dimension_semantics=("arbitrary", "arbitrary"),
            vmem_limit_bytes=VMEM_LIMIT),
        name="moe_dense",
    )(h2, wdense, x1, gt2, g_post, w_gate, w_up, w_down)


def kernel(x, c, w_ada, b_ada, g_pre_mix, g_post_mix, w_in, dw_kernel, dw_bias, conv_ln_gain, conv_ln_bias, lb_logits, rec_norm_gain, w_out, g_pre_ffn, g_post_ffn, w_router_group, b_router_group, w_router_expert, b_router_expert, w_gate, w_up, w_down):
    bsz, s, d = x.shape
    depth = w_ada.shape[0]
    assert depth == 1 and lb_logits.shape[0] == 2
    for l in range(depth):
        mod = _ada(c, w_ada[l], b_ada[l])
        mod3 = mod.reshape(bsz, 6, d)
        pad = LANES - N_EXPERTS - N_GROUPS
        w_r = jnp.concatenate([w_router_expert[l], w_router_group[l], jnp.zeros((d, pad), F32)], axis=1)
        b_r = jnp.concatenate([b_router_expert[l], b_router_group[l], jnp.zeros((pad,), F32)]).reshape(1, LANES)
        x1, h2, wdense = _mix(
            x, mod3, g_pre_mix[l].reshape(1, d), g_post_mix[l].reshape(1, d), g_pre_ffn[l].reshape(1, d),
            w_in[l].astype(BF16), dw_kernel[l], dw_bias[l].reshape(1, D_CONV),
            conv_ln_gain[l].reshape(1, D_CONV), conv_ln_bias[l].reshape(1, D_CONV),
            lb_logits, rec_norm_gain[l].reshape(1, REC_DV), w_out[l].astype(BF16), w_r, b_r)
        gt2 = mod3[:, 5:6, :]
        out = _moe(h2.reshape(bsz * s, d), wdense.reshape(bsz * s, LANES), x1.reshape(bsz * s, d),
                   gt2, g_post_ffn[l].reshape(1, d),
                   w_gate[l].astype(BF16), w_up[l].astype(BF16), w_down[l].astype(BF16), s)
        x = out.reshape(bsz, s, d)
    return x
```

```python
import jax
import jax.numpy as jnp
from jax import lax
from jax.experimental import pallas as pl
from jax.experimental.pallas import tpu as pltpu

D_MODEL = 1024
D_CONV = 512
D_REC = 512
CONV_WIDTH = 31
REC_HEADS = 4
REC_DK = 128
REC_DV = 128
CHUNK = 64
D_IN = 2 * D_CONV + 4 * D_REC
N_GROUPS = 4
EXPERTS_PER_GROUP = 8
N_EXPERTS = 32
TOP_K = 2
D_EXPERT = 256
EPS = 1e-6

LANES = 128
SUBLANES = 8
SEQ_TILE = 512
CONV_ROWS = 64
CONV_HALO = 32
GROUP_ROW0 = 32
EXPERT_ROWS = 256
ROW_SEMS = 128
ROW_UNROLL = 8
VMEM_LIMIT = 56 * 1024 * 1024

F32 = jnp.float32
BF16 = jnp.bfloat16
I32 = jnp.int32
HI = lax.Precision.HIGHEST


def _sigmoid(v):
    return 1.0 / (1.0 + jnp.exp(-v))


def _silu(v):
    return v * _sigmoid(v)


def _rms(v, gain):
    return v * lax.rsqrt(jnp.mean(v * v, axis=-1, keepdims=True) + EPS) * gain


def _ada_kernel(c_ref, w_ref, b_ref, o_ref):
    cond = _silu(c_ref[...])
    o_ref[...] = jnp.dot(cond, w_ref[...], precision=HI, preferred_element_type=F32) + b_ref[...]


def _ada(c, w, b):
    bsz, d = c.shape
    n = w.shape[1]
    return pl.pallas_call(
        _ada_kernel,
        grid=(n // d,),
        in_specs=[
            pl.BlockSpec((bsz, d), lambda j: (0, 0)),
            pl.BlockSpec((d, d), lambda j: (0, j)),
            pl.BlockSpec((1, d), lambda j: (0, j)),
        ],
        out_specs=pl.BlockSpec((bsz, d), lambda j: (0, j)),
        out_shape=jax.ShapeDtypeStruct((bsz, n), F32),
        name="ada_mod",
    )(c, w, b.reshape(1, n))


def _mix_kernel(x_ref, mod_ref, gpre_ref, gpost_ref, gffn_ref, win_ref, dwk_ref, dwb_ref,
                lng_ref, lnb_ref, lbl_ref, rng_ref, wout_ref, wr_ref, br_ref, upper_ref,
                x1_ref, h2_ref, ri_ref, rw_ref, cnt_ref,
                ubuf, qf_s, k_s, lf_s, v_s, sg_s, yb, state, carry):
    b = pl.program_id(0)
    j = pl.program_id(1)
    t = SEQ_TILE

    @pl.when(j == 0)
    def _():
        ubuf[0:CONV_HALO, :] = jnp.zeros((CONV_HALO, D_CONV), F32)
        state[...] = jnp.zeros(state.shape, F32)

    @pl.when((j == 0) & (b == 0))
    def _():
        carry[...] = jnp.zeros(carry.shape, F32)

    x = x_ref[0]
    sh1 = mod_ref[0, 0:1, :]
    sc1 = mod_ref[0, 1:2, :]
    gt1 = mod_ref[0, 2:3, :]
    sh2 = mod_ref[0, 3:4, :]
    sc2 = mod_ref[0, 4:5, :]

    h = _rms(x, gpre_ref[...]) * (1.0 + sc1) + sh1
    proj = jnp.dot(h.astype(BF16), win_ref[...], preferred_element_type=F32)

    cv = proj[:, 0:D_CONV]
    cg = proj[:, D_CONV:2 * D_CONV]
    ubuf[CONV_HALO:CONV_HALO + t, :] = cv * _sigmoid(cg)
    off = CONV_HALO - (CONV_WIDTH - 1)
    for ci in range(t // CONV_ROWS):
        r0 = ci * CONV_ROWS
        acc = jnp.broadcast_to(dwb_ref[...], (CONV_ROWS, D_CONV))
        for kk in range(CONV_WIDTH):
            acc = acc + ubuf[r0 + off + kk:r0 + off + kk + CONV_ROWS, :] * dwk_ref[kk:kk + 1, :]
        mu = jnp.mean(acc, axis=-1, keepdims=True)
        xc = acc - mu
        yc = xc * lax.rsqrt(jnp.mean(xc * xc, axis=-1, keepdims=True) + EPS) * lng_ref[...] + lnb_ref[...]
        yb[r0:r0 + CONV_ROWS, 0:D_CONV] = _silu(yc).astype(BF16)
    ubuf[0:CONV_HALO, :] = ubuf[t:t + CONV_HALO, :]

    q = proj[:, 2 * D_CONV:2 * D_CONV + D_REC]
    f = proj[:, 2 * D_CONV + D_REC:2 * D_CONV + 2 * D_REC]
    iv = proj[:, 2 * D_CONV + 2 * D_REC:2 * D_CONV + 3 * D_REC]
    g = proj[:, 2 * D_CONV + 3 * D_REC:2 * D_CONV + 4 * D_REC]
    l0 = lbl_ref[0:1, :]
    lmax = jnp.max(lbl_ref[...], axis=0, keepdims=True)
    lb = jnp.exp(l0 - lmax) / jnp.sum(jnp.exp(lbl_ref[...] - lmax), axis=0, keepdims=True)
    forget = lb + (1.0 - lb) * _sigmoid(f)
    qf_s[...] = _silu(q)
    k_s[...] = 1.0 - forget
    lf_s[...] = jnp.log(forget)
    v_s[...] = iv
    sg_s[...] = _silu(g)

    row = lax.broadcasted_iota(I32, (CHUNK, CHUNK), 0)
    col = lax.broadcasted_iota(I32, (CHUNK, CHUNK), 1)
    causal = row >= col
    tri = causal.astype(F32)
    nt_dims = (((1,), (1,)), ((), ()))

    def chunk_body(ci, c_):
        r0 = pl.multiple_of(ci * CHUNK, CHUNK)
        rows = pl.ds(r0, CHUNK)
        for hd in range(REC_HEADS):
            cols = slice(hd * REC_DK, (hd + 1) * REC_DK)
            lf_c = lf_s[rows, cols]
            k_c = k_s[rows, cols]
            v_c = v_s[rows, cols].astype(BF16)
            bcum = jnp.dot(tri, lf_c, precision=HI, preferred_element_type=F32)
            blast = bcum[CHUNK - 1:CHUNK, :]
            q_dec = (qf_s[rows, cols] * jnp.exp(bcum)).astype(BF16)
            k_inv = (k_c * jnp.exp(-bcum)).astype(BF16)
            k_end = k_c * jnp.exp(blast - bcum)
            scores = lax.dot_general(q_dec, k_inv, nt_dims, preferred_element_type=F32)
            scores = jnp.where(causal, scores, 0.0).astype(BF16)
            st = state[hd]
            o = jnp.dot(scores, v_c, preferred_element_type=F32)
            o = o + lax.dot_general(q_dec, st.astype(BF16), nt_dims, preferred_element_type=F32)
            upd = jnp.dot(v_s[rows, cols].T.astype(BF16), k_end.astype(BF16),
                          preferred_element_type=F32)
            state[hd] = st * jnp.exp(blast) + upd
            o = _rms(o, rng_ref[...]) * sg_s[rows, cols]
            yb[rows, D_CONV + hd * REC_DV:D_CONV + (hd + 1) * REC_DV] = o.astype(BF16)
        return c_

    lax.fori_loop(0, t // CHUNK, chunk_body, 0)

    y = jnp.dot(yb[...], wout_ref[...], preferred_element_type=F32)
    x1 = x + gt1 * _rms(y, gpost_ref[...])
    x1_ref[0] = x1

    h2 = _rms(x1, gffn_ref[...]) * (1.0 + sc2) + sh2
    h2_ref[0] = h2
    logits = jnp.dot(h2, wr_ref[...], precision=HI, preferred_element_type=F32) + br_ref[...]
    lt = logits.T
    neg = jnp.float32(-jnp.inf)
    r8 = lax.broadcasted_iota(I32, (SUBLANES, t), 0)
    gl = jnp.where(r8 < N_GROUPS, lt[GROUP_ROW0:GROUP_ROW0 + SUBLANES], neg)
    gmax = jnp.max(gl, axis=0, keepdims=True)
    gidx = jnp.min(jnp.where(gl == gmax, r8, SUBLANES), axis=0, keepdims=True)
    gprob = 1.0 / jnp.sum(jnp.exp(gl - gmax), axis=0, keepdims=True)
    re = lax.broadcasted_iota(I32, (N_EXPERTS, t), 0)
    el = jnp.where((re // EXPERTS_PER_GROUP) == gidx, lt[0:N_EXPERTS], neg)
    m1 = jnp.max(el, axis=0, keepdims=True)
    i1 = jnp.min(jnp.where(el == m1, re, N_EXPERTS), axis=0, keepdims=True)
    el2 = jnp.where(re == i1, neg, el)
    m2 = jnp.max(el2, axis=0, keepdims=True)
    i2 = jnp.min(jnp.where(el2 == m2, re, N_EXPERTS), axis=0, keepdims=True)
    r = jnp.exp(m2 - m1)
    w1 = gprob / (1.0 + r)
    w2 = gprob * r / (1.0 + r)
    hot1 = re == i1
    hot2 = re == i2
    hot = jnp.where(hot1 | hot2, 1.0, 0.0)
    prefix = jnp.dot(hot.astype(BF16), upper_ref[...], preferred_element_type=F32) + carry[...]
    rank1 = jnp.sum(jnp.where(hot1, prefix, 0.0), axis=0, keepdims=True)
    rank2 = jnp.sum(jnp.where(hot2, prefix, 0.0), axis=0, keepdims=True)
    carry[...] = carry[...] + jnp.sum(hot, axis=1, keepdims=True)
    zi = jnp.zeros((SUBLANES - 4, t), I32)
    ri_ref[0] = jnp.concatenate([i1, i2, rank1.astype(I32), rank2.astype(I32), zi], axis=0)
    rw_ref[0] = jnp.concatenate([w1, w2, jnp.zeros((SUBLANES - 2, t), F32)], axis=0)
    cnt_ref[...] = jnp.broadcast_to(carry[...], cnt_ref.shape)


def _mix(x, mod3, g_pre, g_post, g_ffn, w_in, dwk, dwb, lng, lnb, lbl, rng, w_out, w_r, b_r):
    bsz, s, d = x.shape
    t = SEQ_TILE
    nt = s // t
    tile = lambda b, j: (b, j, 0)
    rtile = lambda b, j: (b * nt + j, 0, 0)
    const2 = lambda b, j: (0, 0)
    upper = jnp.triu(jnp.ones((t, t), BF16), k=1)
    return pl.pallas_call(
        _mix_kernel,
        grid=(bsz, nt),
        in_specs=[
            pl.BlockSpec((1, t, d), tile),
            pl.BlockSpec((1, 6, d), lambda b, j: (b, 0, 0)),
            pl.BlockSpec((1, d), const2),
            pl.BlockSpec((1, d), const2),
            pl.BlockSpec((1, d), const2),
            pl.BlockSpec((d, D_IN), const2),
            pl.BlockSpec((CONV_WIDTH, D_CONV), const2),
            pl.BlockSpec((1, D_CONV), const2),
            pl.BlockSpec((1, D_CONV), const2),
            pl.BlockSpec((1, D_CONV), const2),
            pl.BlockSpec((2, D_REC), const2),
            pl.BlockSpec((1, REC_DV), const2),
            pl.BlockSpec((d, d), const2),
            pl.BlockSpec((d, LANES), const2),
            pl.BlockSpec((1, LANES), const2),
            pl.BlockSpec((t, t), const2),
        ],
        out_specs=[
            pl.BlockSpec((1, t, d), tile),
            pl.BlockSpec((1, t, d), tile),
            pl.BlockSpec((1, SUBLANES, t), rtile),
            pl.BlockSpec((1, SUBLANES, t), rtile),
            pl.BlockSpec((N_EXPERTS, LANES), const2),
        ],
        out_shape=[
            jax.ShapeDtypeStruct((bsz, s, d), F32),
            jax.ShapeDtypeStruct((bsz, s, d), F32),
            jax.ShapeDtypeStruct((bsz * nt, SUBLANES, t), I32),
            jax.ShapeDtypeStruct((bsz * nt, SUBLANES, t), F32),
            jax.ShapeDtypeStruct((N_EXPERTS, LANES), F32),
        ],
        scratch_shapes=[
            pltpu.VMEM((CONV_HALO + t, D_CONV), F32),
            pltpu.VMEM((t, D_REC), F32),
            pltpu.VMEM((t, D_REC), F32),
            pltpu.VMEM((t, D_REC), F32),
            pltpu.VMEM((t, D_REC), F32),
            pltpu.VMEM((t, D_REC), F32),
            pltpu.VMEM((t, d), BF16),
            pltpu.VMEM((REC_HEADS, REC_DV, REC_DK), F32),
            pltpu.VMEM((N_EXPERTS, 1), F32),
        ],
        compiler_params=pltpu.CompilerParams(
            dimension_semantics=("arbitrary", "arbitrary"),
            vmem_limit_bytes=VMEM_LIMIT),
        name="mixer",
    )(x, mod3, g_pre, g_post, g_ffn, w_in, dwk, dwb, lng, lnb, lbl, rng, w_out, w_r, b_r, upper)


def _meta_kernel(cnt_ref, start_ref, bexp_ref, nblk_ref):
    shift = EXPERT_ROWS.bit_length() - 1
    n_blocks = bexp_ref.shape[0]

    def fill(e):
        def body(jb, c_):
            bexp_ref[jb] = e
            return c_
        return body

    def per_expert(e, blk0):
        nb = (cnt_ref[e] + (EXPERT_ROWS - 1)) >> shift
        start_ref[e] = blk0 << shift
        lax.fori_loop(blk0, blk0 + nb, fill(e), 0)
        return blk0 + nb

    used = lax.fori_loop(0, N_EXPERTS, per_expert, jnp.int32(0))
    lax.fori_loop(used, n_blocks, fill(N_EXPERTS - 1), 0)
    nblk_ref[0] = used


def _meta(counts, n_blocks):
    smem = pl.BlockSpec(memory_space=pltpu.SMEM)
    return pl.pallas_call(
        _meta_kernel,
        in_specs=[smem],
        out_specs=[smem, smem, smem],
        out_shape=[
            jax.ShapeDtypeStruct((N_EXPERTS,), I32),
            jax.ShapeDtypeStruct((n_blocks,), I32),
            jax.ShapeDtypeStruct((1,), I32),
        ],
        name="moe_layout",
    )(counts)


def _row_copy(src_ref, src_row, dst_ref, dst_row, sem):
    return pltpu.make_async_copy(src_ref.at[pl.ds(src_row, 1)], dst_ref.at[pl.ds(dst_row, 1)], sem)


def _dispatch_kernel(start_ref, cnt_ref, nblk_ref, ri_ref, h2_ref, xs_ref, zeros, zsem, sems):
    i = pl.program_id(0)
    t = SEQ_TILE
    shift = EXPERT_ROWS.bit_length() - 1

    @pl.when(i == 0)
    def _():
        zeros[...] = jnp.zeros(zeros.shape, F32)

        def last_block(e):
            nb = (cnt_ref[e] + (EXPERT_ROWS - 1)) >> shift
            row0 = pl.multiple_of(start_ref[e] + ((nb - 1) << shift), EXPERT_ROWS)
            return nb, pltpu.make_async_copy(zeros, xs_ref.at[pl.ds(row0, EXPERT_ROWS)], zsem.at[e])

        for e in range(N_EXPERTS):
            nb, cp = last_block(e)

            @pl.when(nb > 0)
            def _():
                cp.start()

        for e in range(N_EXPERTS):
            nb, cp = last_block(e)

            @pl.when(nb > 0)
            def _():
                cp.wait()

        n_blocks = xs_ref.shape[0] // EXPERT_ROWS

        def tail_block(jb):
            row0 = pl.multiple_of(jb * EXPERT_ROWS, EXPERT_ROWS)
            return pltpu.make_async_copy(zeros, xs_ref.at[pl.ds(row0, EXPERT_ROWS)], zsem.at[jb - nblk_ref[0]])

        def tail_start(jb, c_):
            tail_block(jb).start()
            return c_

        def tail_wait(jb, c_):
            tail_block(jb).wait()
            return c_

        lax.fori_loop(nblk_ref[0], n_blocks, tail_start, 0)
        lax.fori_loop(nblk_ref[0], n_blocks, tail_wait, 0)

    def issue(r):
        slot = r & (ROW_SEMS - 1)
        for k in range(TOP_K):
            sem = sems.at[k * ROW_SEMS + slot]

            @pl.when(r >= ROW_SEMS)
            def _():
                _row_copy(h2_ref, 0, xs_ref, 0, sem).wait()

            dest = start_ref[ri_ref[0, k, r]] + ri_ref[0, TOP_K + k, r]
            _row_copy(h2_ref, r, xs_ref, dest, sem).start()

    def body(rb, c_):
        for u in range(ROW_UNROLL):
            issue(rb * ROW_UNROLL + u)
        return c_

    lax.fori_loop(0, t // ROW_UNROLL, body, 0)

    def drain(s_, c_):
        for k in range(TOP_K):
            _row_copy(h2_ref, 0, xs_ref, 0, sems.at[k * ROW_SEMS + s_]).wait()
        return c_

    lax.fori_loop(0, ROW_SEMS, drain, 0)


def _dispatch(starts, counts, nblk, ri, h2, n_rows):
    n, d = h2.shape
    t = SEQ_TILE
    return pl.pallas_call(
        _dispatch_kernel,
        grid_spec=pltpu.PrefetchScalarGridSpec(
            num_scalar_prefetch=3,
            grid=(n // t,),
            in_specs=[
                pl.BlockSpec((1, SUBLANES, t), lambda i, s_, c_, n_: (i, 0, 0), memory_space=pltpu.SMEM),
                pl.BlockSpec((t, d), lambda i, s_, c_, n_: (i, 0)),
            ],
            out_specs=pl.BlockSpec(memory_space=pl.ANY),
            scratch_shapes=[
                pltpu.VMEM((EXPERT_ROWS, d), F32),
                pltpu.SemaphoreType.DMA((N_EXPERTS,)),
                pltpu.SemaphoreType.DMA((TOP_K * ROW_SEMS,)),
            ],
        ),
        out_shape=jax.ShapeDtypeStruct((n_rows, d), F32),
        compiler_params=pltpu.CompilerParams(dimension_semantics=("arbitrary",)),
        name="moe_dispatch",
    )(starts, counts, nblk, ri, h2)


def _expert_kernel(bexp_ref, nblk_ref, x_ref, wg_ref, wu_ref, wd_ref, y_ref):
    jb = pl.program_id(0)

    @pl.when(jb < nblk_ref[0])
    def _():
        xb = x_ref[...].astype(BF16)
        gate = jnp.dot(xb, wg_ref[0], preferred_element_type=F32)
        up = jnp.dot(xb, wu_ref[0], preferred_element_type=F32)
        hb = (_silu(gate) * up).astype(BF16)
        y_ref[...] = jnp.dot(hb, wd_ref[0], preferred_element_type=F32)

    @pl.when(jb >= nblk_ref[0])
    def _():
        y_ref[...] = jnp.zeros(y_ref.shape, F32)


def _experts(bexp, nblk, xs, w_gate, w_up, w_down):
    n_rows, d = xs.shape
    n_blocks = n_rows // EXPERT_ROWS
    return pl.pallas_call(
        _expert_kernel,
        grid_spec=pltpu.PrefetchScalarGridSpec(
            num_scalar_prefetch=2,
            grid=(n_blocks,),
            in_specs=[
                pl.BlockSpec((EXPERT_ROWS, d), lambda jb, be, nb: (jnp.minimum(jb, nb[0] - 1), 0)),
                pl.BlockSpec((1, d, D_EXPERT), lambda jb, be, nb: (be[jb], 0, 0)),
                pl.BlockSpec((1, d, D_EXPERT), lambda jb, be, nb: (be[jb], 0, 0)),
                pl.BlockSpec((1, D_EXPERT, d), lambda jb, be, nb: (be[jb], 0, 0)),
            ],
            out_specs=pl.BlockSpec((EXPERT_ROWS, d), lambda jb, be, nb: (jb, 0)),
        ),
        out_shape=jax.ShapeDtypeStruct((n_rows, d), F32),
        compiler_params=pltpu.CompilerParams(dimension_semantics=("arbitrary",)),
        name="moe_experts",
    )(bexp, nblk, xs, w_gate, w_up, w_down)


def _combine_kernel(start_ref, ri_ref, rw_ref, x1_ref, gt2_ref, gpost_ref, ys_ref, o_ref, ya, yb, sems):
    t = SEQ_TILE
    bufs = (ya, yb)

    def issue(r):
        slot = r & (ROW_SEMS - 1)
        for k in range(TOP_K):
            sem = sems.at[k * ROW_SEMS + slot]

            @pl.when(r >= ROW_SEMS)
            def _():
                _row_copy(ys_ref, 0, bufs[k], 0, sem).wait()

            src = start_ref[ri_ref[0, k, r]] + ri_ref[0, TOP_K + k, r]
            _row_copy(ys_ref, src, bufs[k], r, sem).start()

    def body(rb, c_):
        for u in range(ROW_UNROLL):
            issue(rb * ROW_UNROLL + u)
        return c_

    lax.fori_loop(0, t // ROW_UNROLL, body, 0)

    def drain(s_, c_):
        for k in range(TOP_K):
            _row_copy(ys_ref, 0, bufs[k], 0, sems.at[k * ROW_SEMS + s_]).wait()
        return c_

    lax.fori_loop(0, ROW_SEMS, drain, 0)

    wt = rw_ref[0].T
    y = ya[...] * wt[:, 0:1] + yb[...] * wt[:, 1:2]
    o_ref[...] = x1_ref[...] + gt2_ref[0] * _rms(y, gpost_ref[...])


def _combine(starts, ri, rw, x1, gt2, g_post, ys, seq):
    n, d = x1.shape
    t = SEQ_TILE
    per_batch = seq // t
    return pl.pallas_call(
        _combine_kernel,
        grid_spec=pltpu.PrefetchScalarGridSpec(
            num_scalar_prefetch=1,
            grid=(n // t,),
            in_specs=[
                pl.BlockSpec((1, SUBLANES, t), lambda i, s_: (i, 0, 0), memory_space=pltpu.SMEM),
                pl.BlockSpec((1, SUBLANES, t), lambda i, s_: (i, 0, 0)),
                pl.BlockSpec((t, d), lambda i, s_: (i, 0)),
                pl.BlockSpec((1, 1, d), lambda i, s_: (i // per_batch, 0, 0)),
                pl.BlockSpec((1, d), lambda i, s_: (0, 0)),
                pl.BlockSpec(memory_space=pl.ANY),
            ],
            out_specs=pl.BlockSpec((t, d), lambda i, s_: (i, 0)),
            scratch_shapes=[
                pltpu.VMEM((t, d), F32),
                pltpu.VMEM((t, d), F32),
                pltpu.SemaphoreType.DMA((TOP_K * ROW_SEMS,)),
            ],
        ),
        out_shape=jax.ShapeDtypeStruct((n, d), F32),
        compiler_params=pltpu.CompilerParams(dimension_semantics=("arbitrary",)),
        name="moe_combine",
    )(starts, ri, rw, x1, gt2, g_post, ys)


def kernel(x, c, w_ada, b_ada, g_pre_mix, g_post_mix, w_in, dw_kernel, dw_bias, conv_ln_gain, conv_ln_bias, lb_logits, rec_norm_gain, w_out, g_pre_ffn, g_post_ffn, w_router_group, b_router_group, w_router_expert, b_router_expert, w_gate, w_up, w_down):
    bsz, s, d = x.shape
    depth = w_ada.shape[0]
    assert depth == 1 and lb_logits.shape[0] == 2
    n_tok = bsz * s
    n_rows = n_tok * TOP_K + N_EXPERTS * EXPERT_ROWS
    for l in range(depth):
        mod = _ada(c, w_ada[l], b_ada[l])
        mod3 = mod.reshape(bsz, 6, d)
        pad = LANES - N_EXPERTS - N_GROUPS
        w_r = jnp.concatenate([w_router_expert[l], w_router_group[l], jnp.zeros((d, pad), F32)], axis=1)
        b_r = jnp.concatenate([b_router_expert[l], b_router_group[l], jnp.zeros((pad,), F32)]).reshape(1, LANES)
        x1, h2, ri, rw, cnt = _mix(
            x, mod3, g_pre_mix[l].reshape(1, d), g_post_mix[l].reshape(1, d), g_pre_ffn[l].reshape(1, d),
            w_in[l].astype(BF16), dw_kernel[l], dw_bias[l].reshape(1, D_CONV),
            conv_ln_gain[l].reshape(1, D_CONV), conv_ln_bias[l].reshape(1, D_CONV),
            lb_logits, rec_norm_gain[l].reshape(1, REC_DV), w_out[l].astype(BF16), w_r, b_r)
        counts = cnt[:, 0].astype(I32)
        starts, bexp, nblk = _meta(counts, n_rows // EXPERT_ROWS)
        xs = _dispatch(starts, counts, nblk, ri, h2.reshape(n_tok, d), n_rows)
        ys = _experts(bexp, nblk, xs, w_gate[l].astype(BF16), w_up[l].astype(BF16), w_down[l].astype(BF16))
        out = _combine(starts, ri, rw, x1.reshape(n_tok, d), mod3[:, 5:6, :], g_post_ffn[l].reshape(1, d), ys, s)
        x = out.reshape(bsz, s, d)
    return x
```

```python
import jax
import jax.numpy as jnp
from jax import lax
from jax.experimental import pallas as pl
from jax.experimental.pallas import tpu as pltpu

D_MODEL = 1024
D_CONV = 512
D_REC = 512
CONV_WIDTH = 31
REC_HEADS = 4
REC_DK = 128
REC_DV = 128
CHUNK = 64
D_IN = 2 * D_CONV + 4 * D_REC
N_GROUPS = 4
EXPERTS_PER_GROUP = 8
N_EXPERTS = 32
TOP_K = 2
D_EXPERT = 256
EPS = 1e-6

LANES = 128
SUBLANES = 8
SEQ_TILE = 512
CONV_ROWS = 64
CONV_HALO = 32
GROUP_ROW0 = 32
EXPERT_ROWS = 256
ROW_UNROLL = 8
VMEM_LIMIT = 56 * 1024 * 1024

F32 = jnp.float32
BF16 = jnp.bfloat16
I32 = jnp.int32
HI = lax.Precision.HIGHEST


def _sigmoid(v):
    return 1.0 / (1.0 + jnp.exp(-v))


def _silu(v):
    return v * _sigmoid(v)


def _rms(v, gain):
    return v * lax.rsqrt(jnp.mean(v * v, axis=-1, keepdims=True) + EPS) * gain


def _ada_kernel(c_ref, w_ref, b_ref, o_ref):
    cond = _silu(c_ref[...])
    o_ref[...] = jnp.dot(cond, w_ref[...], precision=HI, preferred_element_type=F32) + b_ref[...]


def _ada(c, w, b):
    bsz, d = c.shape
    n = w.shape[1]
    return pl.pallas_call(
        _ada_kernel,
        grid=(n // d,),
        in_specs=[
            pl.BlockSpec((bsz, d), lambda j: (0, 0)),
            pl.BlockSpec((d, d), lambda j: (0, j)),
            pl.BlockSpec((1, d), lambda j: (0, j)),
        ],
        out_specs=pl.BlockSpec((bsz, d), lambda j: (0, j)),
        out_shape=jax.ShapeDtypeStruct((bsz, n), F32),
        name="ada_mod",
    )(c, w, b.reshape(1, n))


def _mix_kernel(x_ref, mod_ref, gpre_ref, gpost_ref, gffn_ref, win_ref, dwk_ref, dwb_ref,
                lng_ref, lnb_ref, lbl_ref, rng_ref, wout_ref, wr_ref, br_ref, upper_ref,
                x1_ref, h2_ref, ri_ref, rw_ref, cnt_ref,
                ubuf, qf_s, k_s, lf_s, v_s, sg_s, yb, state, carry):
    b = pl.program_id(0)
    j = pl.program_id(1)
    t = SEQ_TILE

    @pl.when(j == 0)
    def _():
        ubuf[0:CONV_HALO, :] = jnp.zeros((CONV_HALO, D_CONV), F32)
        state[...] = jnp.zeros(state.shape, F32)

    @pl.when((j == 0) & (b == 0))
    def _():
        carry[...] = jnp.zeros(carry.shape, F32)

    x = x_ref[0]
    sh1 = mod_ref[0, 0:1, :]
    sc1 = mod_ref[0, 1:2, :]
    gt1 = mod_ref[0, 2:3, :]
    sh2 = mod_ref[0, 3:4, :]
    sc2 = mod_ref[0, 4:5, :]

    h = _rms(x, gpre_ref[...]) * (1.0 + sc1) + sh1
    proj = jnp.dot(h.astype(BF16), win_ref[...], preferred_element_type=F32)

    cv = proj[:, 0:D_CONV]
    cg = proj[:, D_CONV:2 * D_CONV]
    ubuf[CONV_HALO:CONV_HALO + t, :] = cv * _sigmoid(cg)
    off = CONV_HALO - (CONV_WIDTH - 1)
    for ci in range(t // CONV_ROWS):
        r0 = ci * CONV_ROWS
        acc = jnp.broadcast_to(dwb_ref[...], (CONV_ROWS, D_CONV))
        for kk in range(CONV_WIDTH):
            acc = acc + ubuf[r0 + off + kk:r0 + off + kk + CONV_ROWS, :] * dwk_ref[kk:kk + 1, :]
        mu = jnp.mean(acc, axis=-1, keepdims=True)
        xc = acc - mu
        yc = xc * lax.rsqrt(jnp.mean(xc * xc, axis=-1, keepdims=True) + EPS) * lng_ref[...] + lnb_ref[...]
        yb[r0:r0 + CONV_ROWS, 0:D_CONV] = _silu(yc).astype(BF16)
    ubuf[0:CONV_HALO, :] = ubuf[t:t + CONV_HALO, :]

    q = proj[:, 2 * D_CONV:2 * D_CONV + D_REC]
    f = proj[:, 2 * D_CONV + D_REC:2 * D_CONV + 2 * D_REC]
    iv = proj[:, 2 * D_CONV + 2 * D_REC:2 * D_CONV + 3 * D_REC]
    g = proj[:, 2 * D_CONV + 3 * D_REC:2 * D_CONV + 4 * D_REC]
    l0 = lbl_ref[0:1, :]
    lmax = jnp.max(lbl_ref[...], axis=0, keepdims=True)
    lb = jnp.exp(l0 - lmax) / jnp.sum(jnp.exp(lbl_ref[...] - lmax), axis=0, keepdims=True)
    forget = lb + (1.0 - lb) * _sigmoid(f)
    qf_s[...] = _silu(q)
    k_s[...] = 1.0 - forget
    lf_s[...] = jnp.log(forget)
    v_s[...] = iv
    sg_s[...] = _silu(g)

    row = lax.broadcasted_iota(I32, (CHUNK, CHUNK), 0)
    col = lax.broadcasted_iota(I32, (CHUNK, CHUNK), 1)
    causal = row >= col
    tri = causal.astype(F32)
    nt_dims = (((1,), (1,)), ((), ()))

    def chunk_body(ci, c_):
        r0 = pl.multiple_of(ci * CHUNK, CHUNK)
        rows = pl.ds(r0, CHUNK)
        for hd in range(REC_HEADS):
            cols = slice(hd * REC_DK, (hd + 1) * REC_DK)
            lf_c = lf_s[rows, cols]
            k_c = k_s[rows, cols]
            v_c = v_s[rows, cols].astype(BF16)
            bcum = jnp.dot(tri, lf_c, precision=HI, preferred_element_type=F32)
            blast = bcum[CHUNK - 1:CHUNK, :]
            q_dec = (qf_s[rows, cols] * jnp.exp(bcum)).astype(BF16)
            k_inv = (k_c * jnp.exp(-bcum)).astype(BF16)
            k_end = k_c * jnp.exp(blast - bcum)
            scores = lax.dot_general(q_dec, k_inv, nt_dims, preferred_element_type=F32)
            scores = jnp.where(causal, scores, 0.0).astype(BF16)
            st = state[hd]
            o = jnp.dot(scores, v_c, preferred_element_type=F32)
            o = o + lax.dot_general(q_dec, st.astype(BF16), nt_dims, preferred_element_type=F32)
            upd = jnp.dot(v_s[rows, cols].T.astype(BF16), k_end.astype(BF16),
                          preferred_element_type=F32)
            state[hd] = st * jnp.exp(blast) + upd
            o = _rms(o, rng_ref[...]) * sg_s[rows, cols]
            yb[rows, D_CONV + hd * REC_DV:D_CONV + (hd + 1) * REC_DV] = o.astype(BF16)
        return c_

    lax.fori_loop(0, t // CHUNK, chunk_body, 0)

    y = jnp.dot(yb[...], wout_ref[...], preferred_element_type=F32)
    x1 = x + gt1 * _rms(y, gpost_ref[...])
    x1_ref[0] = x1

    h2 = _rms(x1, gffn_ref[...]) * (1.0 + sc2) + sh2
    h2_ref[0] = h2
    logits = jnp.dot(h2, wr_ref[...], precision=HI, preferred_element_type=F32) + br_ref[...]
    lt = logits.T
    neg = jnp.float32(-jnp.inf)
    r8 = lax.broadcasted_iota(I32, (SUBLANES, t), 0)
    gl = jnp.where(r8 < N_GROUPS, lt[GROUP_ROW0:GROUP_ROW0 + SUBLANES], neg)
    gmax = jnp.max(gl, axis=0, keepdims=True)
    gidx = jnp.min(jnp.where(gl == gmax, r8, SUBLANES), axis=0, keepdims=True)
    gprob = 1.0 / jnp.sum(jnp.exp(gl - gmax), axis=0, keepdims=True)
    re = lax.broadcasted_iota(I32, (N_EXPERTS, t), 0)
    el = jnp.where((re // EXPERTS_PER_GROUP) == gidx, lt[0:N_EXPERTS], neg)
    m1 = jnp.max(el, axis=0, keepdims=True)
    i1 = jnp.min(jnp.where(el == m1, re, N_EXPERTS), axis=0, keepdims=True)
    el2 = jnp.where(re == i1, neg, el)
    m2 = jnp.max(el2, axis=0, keepdims=True)
    i2 = jnp.min(jnp.where(el2 == m2, re, N_EXPERTS), axis=0, keepdims=True)
    r = jnp.exp(m2 - m1)
    w1 = gprob / (1.0 + r)
    w2 = gprob * r / (1.0 + r)
    hot1 = re == i1
    hot2 = re == i2
    hot = jnp.where(hot1 | hot2, 1.0, 0.0)
    prefix = jnp.dot(hot.astype(BF16), upper_ref[...], preferred_element_type=F32) + carry[...]
    rank1 = jnp.sum(jnp.where(hot1, prefix, 0.0), axis=0, keepdims=True)
    rank2 = jnp.sum(jnp.where(hot2, prefix, 0.0), axis=0, keepdims=True)
    carry[...] = carry[...] + jnp.sum(hot, axis=1, keepdims=True)
    zi = jnp.zeros((SUBLANES - 4, t), I32)
    ri_ref[0] = jnp.concatenate([i1, i2, rank1.astype(I32), rank2.astype(I32), zi], axis=0)
    rw_ref[0] = jnp.concatenate([w1, w2, jnp.zeros((SUBLANES - 2, t), F32)], axis=0)
    cnt_ref[...] = jnp.broadcast_to(carry[...], cnt_ref.shape)


def _mix(x, mod3, g_pre, g_post, g_ffn, w_in, dwk, dwb, lng, lnb, lbl, rng, w_out, w_r, b_r):
    bsz, s, d = x.shape
    t = SEQ_TILE
    nt = s // t
    tile = lambda b, j: (b, j, 0)
    rtile = lambda b, j: (b * nt + j, 0, 0)
    const2 = lambda b, j: (0, 0)
    upper = jnp.triu(jnp.ones((t, t), BF16), k=1)
    return pl.pallas_call(
        _mix_kernel,
        grid=(bsz, nt),
        in_specs=[
            pl.BlockSpec((1, t, d), tile),
            pl.BlockSpec((1, 6, d), lambda b, j: (b, 0, 0)),
            pl.BlockSpec((1, d), const2),
            pl.BlockSpec((1, d), const2),
            pl.BlockSpec((1, d), const2),
            pl.BlockSpec((d, D_IN), const2),
            pl.BlockSpec((CONV_WIDTH, D_CONV), const2),
            pl.BlockSpec((1, D_CONV), const2),
            pl.BlockSpec((1, D_CONV), const2),
            pl.BlockSpec((1, D_CONV), const2),
            pl.BlockSpec((2, D_REC), const2),
            pl.BlockSpec((1, REC_DV), const2),
            pl.BlockSpec((d, d), const2),
            pl.BlockSpec((d, LANES), const2),
            pl.BlockSpec((1, LANES), const2),
            pl.BlockSpec((t, t), const2),
        ],
        out_specs=[
            pl.BlockSpec((1, t, d), tile),
            pl.BlockSpec((1, t, d), tile),
            pl.BlockSpec((1, SUBLANES, t), rtile),
            pl.BlockSpec((1, SUBLANES, t), rtile),
            pl.BlockSpec((N_EXPERTS, LANES), const2),
        ],
        out_shape=[
            jax.ShapeDtypeStruct((bsz, s, d), F32),
            jax.ShapeDtypeStruct((bsz, s, d), F32),
            jax.ShapeDtypeStruct((bsz * nt, SUBLANES, t), I32),
            jax.ShapeDtypeStruct((bsz * nt, SUBLANES, t), F32),
            jax.ShapeDtypeStruct((N_EXPERTS, LANES), F32),
        ],
        scratch_shapes=[
            pltpu.VMEM((CONV_HALO + t, D_CONV), F32),
            pltpu.VMEM((t, D_REC), F32),
            pltpu.VMEM((t, D_REC), F32),
            pltpu.VMEM((t, D_REC), F32),
            pltpu.VMEM((t, D_REC), F32),
            pltpu.VMEM((t, D_REC), F32),
            pltpu.VMEM((t, d), BF16),
            pltpu.VMEM((REC_HEADS, REC_DV, REC_DK), F32),
            pltpu.VMEM((N_EXPERTS, 1), F32),
        ],
        compiler_params=pltpu.CompilerParams(
            dimension_semantics=("arbitrary", "arbitrary"),
            vmem_limit_bytes=VMEM_LIMIT),
        name="mixer",
    )(x, mod3, g_pre, g_post, g_ffn, w_in, dwk, dwb, lng, lnb, lbl, rng, w_out, w_r, b_r, upper)


def _meta_kernel(cnt_ref, start_ref, bexp_ref, nblk_ref):
    shift = EXPERT_ROWS.bit_length() - 1
    n_blocks = bexp_ref.shape[0]

    def fill(e):
        def body(jb, c_):
            bexp_ref[jb] = e
            return c_
        return body

    def per_expert(e, blk0):
        nb = (cnt_ref[e] + (EXPERT_ROWS - 1)) >> shift
        start_ref[e] = blk0 << shift
        lax.fori_loop(blk0, blk0 + nb, fill(e), 0)
        return blk0 + nb

    used = lax.fori_loop(0, N_EXPERTS, per_expert, jnp.int32(0))
    lax.fori_loop(used, n_blocks, fill(N_EXPERTS - 1), 0)
    nblk_ref[0] = used


def _meta(counts, n_blocks):
    smem = pl.BlockSpec(memory_space=pltpu.SMEM)
    return pl.pallas_call(
        _meta_kernel,
        in_specs=[smem],
        out_specs=[smem, smem, smem],
        out_shape=[
            jax.ShapeDtypeStruct((N_EXPERTS,), I32),
            jax.ShapeDtypeStruct((n_blocks,), I32),
            jax.ShapeDtypeStruct((1,), I32),
        ],
        name="moe_layout",
    )(counts)


def _positions_kernel(start_ref, ri_ref, pos_ref):
    e = ri_ref[:, 0:TOP_K, :]
    seg = jnp.zeros(e.shape, I32)
    for k in range(N_EXPERTS):
        seg = jnp.where(e == k, start_ref[k], seg)
    pos = seg + ri_ref[:, TOP_K:2 * TOP_K, :]
    pad = jnp.zeros((e.shape[0], SUBLANES - TOP_K, e.shape[2]), I32)
    pos_ref[...] = jnp.concatenate([pos, pad], axis=1)


def _positions(starts, ri):
    n_tiles, _, t = ri.shape
    tb = SUBLANES
    return pl.pallas_call(
        _positions_kernel,
        grid_spec=pltpu.PrefetchScalarGridSpec(
            num_scalar_prefetch=1,
            grid=(n_tiles // tb,),
            in_specs=[pl.BlockSpec((tb, SUBLANES, t), lambda i, s_: (i, 0, 0))],
            out_specs=pl.BlockSpec((tb, SUBLANES, t), lambda i, s_: (i, 0, 0)),
        ),
        out_shape=jax.ShapeDtypeStruct(ri.shape, I32),
        name="moe_positions",
    )(starts, ri)


def _row_copy(src_ref, src_row, dst_ref, dst_row, sem):
    return pltpu.make_async_copy(src_ref.at[pl.ds(src_row, 1)], dst_ref.at[pl.ds(dst_row, 1)], sem)


def _dispatch_kernel(start_ref, cnt_ref, nblk_ref, pos_ref, h2_ref, xs_ref, zeros, zsem, sems):
    i = pl.program_id(0)
    t = SEQ_TILE
    shift = EXPERT_ROWS.bit_length() - 1

    @pl.when(i == 0)
    def _():
        zeros[...] = jnp.zeros(zeros.shape, F32)

        def last_block(e):
            nb = (cnt_ref[e] + (EXPERT_ROWS - 1)) >> shift
            row0 = pl.multiple_of(start_ref[e] + ((nb - 1) << shift), EXPERT_ROWS)
            return nb, pltpu.make_async_copy(zeros, xs_ref.at[pl.ds(row0, EXPERT_ROWS)], zsem.at[e])

        for e in range(N_EXPERTS):
            nb, cp = last_block(e)

            @pl.when(nb > 0)
            def _():
                cp.start()

        for e in range(N_EXPERTS):
            nb, cp = last_block(e)

            @pl.when(nb > 0)
            def _():
                cp.wait()

        n_blocks = xs_ref.shape[0] // EXPERT_ROWS

        def tail_block(jb):
            row0 = pl.multiple_of(jb * EXPERT_ROWS, EXPERT_ROWS)
            return pltpu.make_async_copy(zeros, xs_ref.at[pl.ds(row0, EXPERT_ROWS)], zsem.at[jb - nblk_ref[0]])

        def tail_start(jb, c_):
            tail_block(jb).start()
            return c_

        def tail_wait(jb, c_):
            tail_block(jb).wait()
            return c_

        lax.fori_loop(nblk_ref[0], n_blocks, tail_start, 0)
        lax.fori_loop(nblk_ref[0], n_blocks, tail_wait, 0)

    def body(rb, c_):
        for u in range(ROW_UNROLL):
            r = rb * ROW_UNROLL + u
            for k in range(TOP_K):
                _row_copy(h2_ref, r, xs_ref, pos_ref[0, k, r], sems.at[0]).start()
        return c_

    lax.fori_loop(0, t // ROW_UNROLL, body, 0)
    for k in range(TOP_K):
        pltpu.make_async_copy(h2_ref, xs_ref.at[pl.ds(0, t)], sems.at[0]).wait()


def _dispatch(starts, counts, nblk, pos, h2, n_rows):
    n, d = h2.shape
    t = SEQ_TILE
    return pl.pallas_call(
        _dispatch_kernel,
        grid_spec=pltpu.PrefetchScalarGridSpec(
            num_scalar_prefetch=3,
            grid=(n // t,),
            in_specs=[
                pl.BlockSpec((1, SUBLANES, t), lambda i, s_, c_, n_: (i, 0, 0), memory_space=pltpu.SMEM),
                pl.BlockSpec((t, d), lambda i, s_, c_, n_: (i, 0)),
            ],
            out_specs=pl.BlockSpec(memory_space=pl.ANY),
            scratch_shapes=[
                pltpu.VMEM((EXPERT_ROWS, d), F32),
                pltpu.SemaphoreType.DMA((N_EXPERTS,)),
                pltpu.SemaphoreType.DMA((1,)),
            ],
        ),
        out_shape=jax.ShapeDtypeStruct((n_rows, d), F32),
        compiler_params=pltpu.CompilerParams(dimension_semantics=("arbitrary",)),
        name="moe_dispatch",
    )(starts, counts, nblk, pos, h2)


def _expert_kernel(bexp_ref, nblk_ref, x_ref, wg_ref, wu_ref, wd_ref, y_ref):
    jb = pl.program_id(0)

    @pl.when(jb < nblk_ref[0])
    def _():
        xb = x_ref[...].astype(BF16)
        gate = jnp.dot(xb, wg_ref[0], preferred_element_type=F32)
        up = jnp.dot(xb, wu_ref[0], preferred_element_type=F32)
        hb = (_silu(gate) * up).astype(BF16)
        y_ref[...] = jnp.dot(hb, wd_ref[0], preferred_element_type=F32)

    @pl.when(jb >= nblk_ref[0])
    def _():
        y_ref[...] = jnp.zeros(y_ref.shape, F32)


def _experts(bexp, nblk, xs, w_gate, w_up, w_down):
    n_rows, d = xs.shape
    n_blocks = n_rows // EXPERT_ROWS
    return pl.pallas_call(
        _expert_kernel,
        grid_spec=pltpu.PrefetchScalarGridSpec(
            num_scalar_prefetch=2,
            grid=(n_blocks,),
            in_specs=[
                pl.BlockSpec((EXPERT_ROWS, d), lambda jb, be, nb: (jnp.minimum(jb, nb[0] - 1), 0)),
                pl.BlockSpec((1, d, D_EXPERT), lambda jb, be, nb: (be[jb], 0, 0)),
                pl.BlockSpec((1, d, D_EXPERT), lambda jb, be, nb: (be[jb], 0, 0)),
                pl.BlockSpec((1, D_EXPERT, d), lambda jb, be, nb: (be[jb], 0, 0)),
            ],
            out_specs=pl.BlockSpec((EXPERT_ROWS, d), lambda jb, be, nb: (jb, 0)),
        ),
        out_shape=jax.ShapeDtypeStruct((n_rows, d), F32),
        compiler_params=pltpu.CompilerParams(dimension_semantics=("arbitrary",)),
        name="moe_experts",
    )(bexp, nblk, xs, w_gate, w_up, w_down)


def _combine_kernel(pos_ref, rw_ref, x1_ref, gt2_ref, gpost_ref, ys_ref, o_ref, ya, yb, sems):
    t = SEQ_TILE
    bufs = (ya, yb)

    def body(rb, c_):
        for u in range(ROW_UNROLL):
            r = rb * ROW_UNROLL + u
            for k in range(TOP_K):
                _row_copy(ys_ref, pos_ref[0, k, r], bufs[k], r, sems.at[0]).start()
        return c_

    lax.fori_loop(0, t // ROW_UNROLL, body, 0)
    for k in range(TOP_K):
        pltpu.make_async_copy(ys_ref.at[pl.ds(0, t)], bufs[k], sems.at[0]).wait()

    wt = rw_ref[0].T
    y = ya[...] * wt[:, 0:1] + yb[...] * wt[:, 1:2]
    o_ref[...] = x1_ref[...] + gt2_ref[0] * _rms(y, gpost_ref[...])


def _combine(pos, rw, x1, gt2, g_post, ys, seq):
    n, d = x1.shape
    t = SEQ_TILE
    per_batch = seq // t
    return pl.pallas_call(
        _combine_kernel,
        grid=(n // t,),
        in_specs=[
            pl.BlockSpec((1, SUBLANES, t), lambda i: (i, 0, 0), memory_space=pltpu.SMEM),
            pl.BlockSpec((1, SUBLANES, t), lambda i: (i, 0, 0)),
            pl.BlockSpec((t, d), lambda i: (i, 0)),
            pl.BlockSpec((1, 1, d), lambda i: (i // per_batch, 0, 0)),
            pl.BlockSpec((1, d), lambda i: (0, 0)),
            pl.BlockSpec(memory_space=pl.ANY),
        ],
        out_specs=pl.BlockSpec((t, d), lambda i: (i, 0)),
        scratch_shapes=[
            pltpu.VMEM((t, d), F32),
            pltpu.VMEM((t, d), F32),
            pltpu.SemaphoreType.DMA((1,)),
        ],
        out_shape=jax.ShapeDtypeStruct((n, d), F32),
        compiler_params=pltpu.CompilerParams(dimension_semantics=("arbitrary",)),
        name="moe_combine",
    )(pos, rw, x1, gt2, g_post, ys)


def kernel(x, c, w_ada, b_ada, g_pre_mix, g_post_mix, w_in, dw_kernel, dw_bias, conv_ln_gain, conv_ln_bias, lb_logits, rec_norm_gain, w_out, g_pre_ffn, g_post_ffn, w_router_group, b_router_group, w_router_expert, b_router_expert, w_gate, w_up, w_down):
    bsz, s, d = x.shape
    depth = w_ada.shape[0]
    assert depth == 1 and lb_logits.shape[0] == 2
    n_tok = bsz * s
    n_rows = n_tok * TOP_K + N_EXPERTS * EXPERT_ROWS
    for l in range(depth):
        mod = _ada(c, w_ada[l], b_ada[l])
        mod3 = mod.reshape(bsz, 6, d)
        pad = LANES - N_EXPERTS - N_GROUPS
        w_r = jnp.concatenate([w_router_expert[l], w_router_group[l], jnp.zeros((d, pad), F32)], axis=1)
        b_r = jnp.concatenate([b_router_expert[l], b_router_group[l], jnp.zeros((pad,), F32)]).reshape(1, LANES)
        x1, h2, ri, rw, cnt = _mix(
            x, mod3, g_pre_mix[l].reshape(1, d), g_post_mix[l].reshape(1, d), g_pre_ffn[l].reshape(1, d),
            w_in[l].astype(BF16), dw_kernel[l], dw_bias[l].reshape(1, D_CONV),
            conv_ln_gain[l].reshape(1, D_CONV), conv_ln_bias[l].reshape(1, D_CONV),
            lb_logits, rec_norm_gain[l].reshape(1, REC_DV), w_out[l].astype(BF16), w_r, b_r)
        counts = cnt[:, 0].astype(I32)
        starts, bexp, nblk = _meta(counts, n_rows // EXPERT_ROWS)
        pos = _positions(starts, ri)
        xs = _dispatch(starts, counts, nblk, pos, h2.reshape(n_tok, d), n_rows)
        ys = _experts(bexp, nblk, xs, w_gate[l].astype(BF16), w_up[l].astype(BF16), w_down[l].astype(BF16))
        out = _combine(pos, rw, x1.reshape(n_tok, d), mod3[:, 5:6, :], g_post_ffn[l].reshape(1, d), ys, s)
        x = out.reshape(bsz, s, d)
    return x
```

```python
import jax
import jax.numpy as jnp
from jax import lax
from jax.experimental import pallas as pl
from jax.experimental.pallas import tpu as pltpu

D_MODEL = 1024
D_CONV = 512
D_REC = 512
CONV_WIDTH = 31
REC_HEADS = 4
REC_DK = 128
REC_DV = 128
CHUNK = 64
D_IN = 2 * D_CONV + 4 * D_REC
N_GROUPS = 4
EXPERTS_PER_GROUP = 8
N_EXPERTS = 32
TOP_K = 2
D_EXPERT = 256
EPS = 1e-6

LANES = 128
SUBLANES = 8
SEQ_TILE = 512
CONV_ROWS = 64
ROW_STEP = 32
STAT_ROWS = 64
CONV_HALO = 32
GROUP_ROW0 = 32
EXPERT_ROWS = 256
ROW_UNROLL = 8
VMEM_LIMIT = 56 * 1024 * 1024

F32 = jnp.float32
BF16 = jnp.bfloat16
I32 = jnp.int32
HI = lax.Precision.HIGHEST


def _sigmoid(v):
    return 1.0 / (1.0 + jnp.exp(-v))


def _silu(v):
    return v * _sigmoid(v)


def _rms(v, gain):
    return v * lax.rsqrt(jnp.mean(v * v, axis=-1, keepdims=True) + EPS) * gain


def _ada_kernel(c_ref, w_ref, b_ref, o_ref):
    cond = _silu(c_ref[...])
    o_ref[...] = jnp.dot(cond, w_ref[...], precision=HI, preferred_element_type=F32) + b_ref[...]


def _ada(c, w, b):
    bsz, d = c.shape
    n = w.shape[1]
    return pl.pallas_call(
        _ada_kernel,
        grid=(n // d,),
        in_specs=[
            pl.BlockSpec((bsz, d), lambda j: (0, 0)),
            pl.BlockSpec((d, d), lambda j: (0, j)),
            pl.BlockSpec((1, d), lambda j: (0, j)),
        ],
        out_specs=pl.BlockSpec((bsz, d), lambda j: (0, j)),
        out_shape=jax.ShapeDtypeStruct((bsz, n), F32),
        name="ada_mod",
    )(c, w, b.reshape(1, n))


def _mix_kernel(x_ref, mod_ref, gpre_ref, gpost_ref, gffn_ref, win_ref, dwk_ref, dwb_ref,
                lng_ref, lnb_ref, lbl_ref, rng_ref, wout_ref, wr_ref, br_ref, upper_ref,
                x1_ref, h2_ref, ri_ref, rw_ref, cnt_ref,
                st1, st2, vec_s, hb_s, proj_s, ubuf, cv_s, lfp_s, k_s, b_s, qd_s, ki_s, ke_s, v_s, a_s, o_s, yb, y_s,
                state, carry):
    b = pl.program_id(0)
    j = pl.program_id(1)
    t = SEQ_TILE
    n_chunks = t // CHUNK
    heads = range(REC_HEADS)

    @pl.when(j == 0)
    def _():
        ubuf[0:CONV_HALO, :] = jnp.zeros((CONV_HALO, D_CONV), F32)
        state[...] = jnp.zeros(state.shape, F32)

    @pl.when((j == 0) & (b == 0))
    def _():
        carry[...] = jnp.zeros(carry.shape, F32)

    sh1 = mod_ref[0, 0:1, :]
    sc1 = mod_ref[0, 1:2, :]
    gt1 = mod_ref[0, 2:3, :]
    sh2 = mod_ref[0, 3:4, :]
    sc2 = mod_ref[0, 4:5, :]

    def row_loop(rows_per_step, body):
        def step(ci, c_):
            body(pl.ds(pl.multiple_of(ci * rows_per_step, rows_per_step), rows_per_step))
            return c_
        lax.fori_loop(0, t // rows_per_step, step, 0)

    stat_blocks = [slice(r0, r0 + STAT_ROWS) for r0 in range(0, t, STAT_ROWS)]

    def head_cols(base, hd):
        return slice(base + hd * REC_DK, base + (hd + 1) * REC_DK)

    def lane_tiles(width):
        return [slice(l0, l0 + LANES) for l0 in range(0, width, LANES)]

    def put_stat(ref, rows, stat):
        ref[rows, :] = jnp.broadcast_to(stat, (STAT_ROWS, LANES))

    vec_s[0:1, :] = gpre_ref[...] * (1.0 + sc1)
    vec_s[1:2, :] = sh1
    vec_s[2:3, :] = gt1 * gpost_ref[...]
    vec_s[3:4, :] = gffn_ref[...] * (1.0 + sc2)
    vec_s[4:5, :] = sh2

    for rows in stat_blocks:
        xc = x_ref[0, rows, :]
        put_stat(st1, rows, lax.rsqrt(jnp.mean(xc * xc, axis=-1, keepdims=True) + EPS))

    def prenorm(rows):
        inv = st1[rows, :]
        for lanes in lane_tiles(D_MODEL):
            hb_s[rows, lanes] = (x_ref[0, rows, lanes] * inv * vec_s[0:1, lanes] + vec_s[1:2, lanes]).astype(BF16)

    row_loop(ROW_STEP, prenorm)
    proj_s[...] = jnp.dot(hb_s[...], win_ref[...], preferred_element_type=F32)

    def glu(rows):
        cv = proj_s[rows, 0:D_CONV]
        cg = proj_s[rows, D_CONV:2 * D_CONV]
        ubuf[pl.ds(pl.multiple_of(rows.start + CONV_HALO, ROW_STEP), ROW_STEP), :] = cv * _sigmoid(cg)

    row_loop(ROW_STEP, glu)

    lead = CONV_HALO - (CONV_WIDTH - 1)
    win_rows = CONV_ROWS + CONV_HALO

    def conv(rows):
        for lt in range(D_CONV // LANES):
            lanes = slice(lt * LANES, (lt + 1) * LANES)
            win = ubuf[pl.ds(rows.start, win_rows), lanes]
            acc = jnp.broadcast_to(dwb_ref[:, lanes], (CONV_ROWS, LANES))
            for res in range(SUBLANES):
                shifted = win if res == 0 else pltpu.roll(win, win_rows - res, axis=0)
                for al in range(0, win_rows - CONV_ROWS + 1, SUBLANES):
                    kk = al + res - lead
                    if 0 <= kk < CONV_WIDTH and al + CONV_ROWS + res <= win_rows:
                        acc = acc + shifted[al:al + CONV_ROWS] * dwk_ref[kk:kk + 1, lanes]
            cv_s[rows, lanes] = acc

    row_loop(CONV_ROWS, conv)
    ubuf[0:CONV_HALO, :] = ubuf[t:t + CONV_HALO, :]

    for rows in stat_blocks:
        acc = cv_s[rows, :]
        mu = jnp.mean(acc, axis=-1, keepdims=True)
        xc = acc - mu
        put_stat(st1, rows, mu)
        put_stat(st2, rows, lax.rsqrt(jnp.mean(xc * xc, axis=-1, keepdims=True) + EPS))

    def conv_norm(rows):
        mu = st1[rows, :]
        inv = st2[rows, :]
        for lanes in lane_tiles(D_CONV):
            yc = (cv_s[rows, lanes] - mu) * inv * lng_ref[:, lanes] + lnb_ref[:, lanes]
            yb[rows, lanes] = _silu(yc).astype(BF16)

    row_loop(ROW_STEP, conv_norm)

    q0 = 2 * D_CONV
    f0 = q0 + D_REC
    i0 = f0 + D_REC
    g0 = i0 + D_REC
    l0 = lbl_ref[0:1, :]
    lmax = jnp.max(lbl_ref[...], axis=0, keepdims=True)
    lb = jnp.exp(l0 - lmax) / jnp.sum(jnp.exp(lbl_ref[...] - lmax), axis=0, keepdims=True)

    def gates(rows):
        for hd in heads:
            cols = head_cols(0, hd)
            forget = lb[:, cols] + (1.0 - lb[:, cols]) * _sigmoid(proj_s[rows, head_cols(f0, hd)])
            k_s[rows, cols] = 1.0 - forget
            lf = jnp.log(forget)
            hi = lf.astype(BF16)
            rem = lf - hi.astype(F32)
            mid = rem.astype(BF16)
            lfp_s[0, rows, cols] = hi
            lfp_s[1, rows, cols] = mid
            lfp_s[2, rows, cols] = (rem - mid.astype(F32)).astype(BF16)

    row_loop(CHUNK, gates)

    row = lax.broadcasted_iota(I32, (CHUNK, CHUNK), 0)
    col = lax.broadcasted_iota(I32, (CHUNK, CHUNK), 1)
    causal = row >= col
    tri = jnp.where(causal, 1.0, 0.0).astype(BF16)
    for ci in range(n_chunks):
        rows = slice(ci * CHUNK, (ci + 1) * CHUNK)
        b_s[rows, :] = (jnp.dot(tri, lfp_s[0, rows, :], preferred_element_type=F32)
                        + jnp.dot(tri, lfp_s[1, rows, :], preferred_element_type=F32)
                        + jnp.dot(tri, lfp_s[2, rows, :], preferred_element_type=F32))

    def decays(ci, c_):
        rows = pl.ds(pl.multiple_of(ci * CHUNK, CHUNK), CHUNK)
        for hd in heads:
            cols = head_cols(0, hd)
            bcum = b_s[rows, cols]
            k_c = k_s[rows, cols]
            a_last = jnp.exp(bcum[CHUNK - 1:CHUNK, :])
            k_inv = k_c * jnp.exp(-bcum)
            qd_s[rows, cols] = (_silu(proj_s[rows, head_cols(q0, hd)]) * jnp.exp(bcum)).astype(BF16)
            ki_s[rows, cols] = k_inv.astype(BF16)
            ke_s[rows, cols] = (k_inv * a_last).astype(BF16)
            v_s[rows, cols] = proj_s[rows, head_cols(i0, hd)].astype(BF16)
            a_s[ci, :, cols] = a_last
        return c_

    lax.fori_loop(0, n_chunks, decays, 0)

    nt_dims = (((1,), (1,)), ((), ()))
    pairs = [(ci, hd) for ci in range(n_chunks) for hd in heads]

    def blk(ci, hd):
        return slice(ci * CHUNK, (ci + 1) * CHUNK), head_cols(0, hd)

    scores = {}
    for p in pairs:
        rows, cols = blk(*p)
        sc = lax.dot_general(qd_s[rows, cols], ki_s[rows, cols], nt_dims, preferred_element_type=F32)
        scores[p] = jnp.where(causal, sc, 0.0).astype(BF16)
    for p in pairs:
        rows, cols = blk(*p)
        o_s[rows, cols] = jnp.dot(scores[p], v_s[rows, cols], preferred_element_type=F32)
    upd = {}
    for p in pairs:
        rows, cols = blk(*p)
        v_t = v_s[rows, cols].astype(F32).T.astype(BF16)
        upd[p] = jnp.dot(v_t, ke_s[rows, cols], preferred_element_type=F32)
    prev = {}
    for hd in heads:
        st = state[hd]
        for ci in range(n_chunks):
            prev[(ci, hd)] = st.astype(BF16)
            st = st * a_s[ci, :, head_cols(0, hd)] + upd[(ci, hd)]
        state[hd] = st
    for p in pairs:
        rows, cols = blk(*p)
        o_s[rows, cols] += lax.dot_general(qd_s[rows, cols], prev[p], nt_dims, preferred_element_type=F32)

    for rows in stat_blocks:
        for hd in heads:
            o = _rms(o_s[rows, head_cols(0, hd)], rng_ref[...]) * _silu(proj_s[rows, head_cols(g0, hd)])
            yb[rows, head_cols(D_CONV, hd)] = o.astype(BF16)

    y_s[...] = jnp.dot(yb[...], wout_ref[...], preferred_element_type=F32)
    for rows in stat_blocks:
        yc = y_s[rows, :]
        put_stat(st1, rows, lax.rsqrt(jnp.mean(yc * yc, axis=-1, keepdims=True) + EPS))

    def residual(rows):
        inv = st1[rows, :]
        for lanes in lane_tiles(D_MODEL):
            x1_ref[0, rows, lanes] = x_ref[0, rows, lanes] + y_s[rows, lanes] * inv * vec_s[2:3, lanes]

    row_loop(ROW_STEP, residual)
    for rows in stat_blocks:
        xc = x1_ref[0, rows, :]
        put_stat(st2, rows, lax.rsqrt(jnp.mean(xc * xc, axis=-1, keepdims=True) + EPS))

    def ffn_norm(rows):
        inv = st2[rows, :]
        for lanes in lane_tiles(D_MODEL):
            h2_ref[0, rows, lanes] = x1_ref[0, rows, lanes] * inv * vec_s[3:4, lanes] + vec_s[4:5, lanes]

    row_loop(ROW_STEP, ffn_norm)

    h2v = h2_ref[0]
    h_hi = h2v.astype(BF16)
    h_lo = (h2v - h_hi.astype(F32)).astype(BF16)
    logits = (jnp.dot(h_hi, wr_ref[0], preferred_element_type=F32)
              + jnp.dot(h_hi, wr_ref[1], preferred_element_type=F32)
              + jnp.dot(h_lo, wr_ref[0], preferred_element_type=F32)) + br_ref[...]
    lt = logits.T
    neg = jnp.float32(-jnp.inf)
    r8 = lax.broadcasted_iota(I32, (SUBLANES, t), 0)
    gl = jnp.where(r8 < N_GROUPS, lt[GROUP_ROW0:GROUP_ROW0 + SUBLANES], neg)
    gmax = jnp.max(gl, axis=0, keepdims=True)
    gidx = jnp.min(jnp.where(gl == gmax, r8, SUBLANES), axis=0, keepdims=True)
    gprob = 1.0 / jnp.sum(jnp.exp(gl - gmax), axis=0, keepdims=True)
    re = lax.broadcasted_iota(I32, (N_EXPERTS, t), 0)
    el = jnp.where((re // EXPERTS_PER_GROUP) == gidx, lt[0:N_EXPERTS], neg)
    m1 = jnp.max(el, axis=0, keepdims=True)
    i1 = jnp.min(jnp.where(el == m1, re, N_EXPERTS), axis=0, keepdims=True)
    el2 = jnp.where(re == i1, neg, el)
    m2 = jnp.max(el2, axis=0, keepdims=True)
    i2 = jnp.min(jnp.where(el2 == m2, re, N_EXPERTS), axis=0, keepdims=True)
    r = jnp.exp(m2 - m1)
    w1 = gprob / (1.0 + r)
    w2 = gprob * r / (1.0 + r)
    hot1 = re == i1
    hot2 = re == i2
    hot = jnp.where(hot1 | hot2, 1.0, 0.0)
    prefix = jnp.dot(hot.astype(BF16), upper_ref[...], preferred_element_type=F32) + carry[...]
    rank1 = jnp.sum(jnp.where(hot1, prefix, 0.0), axis=0, keepdims=True)
    rank2 = jnp.sum(jnp.where(hot2, prefix, 0.0), axis=0, keepdims=True)
    carry[...] = carry[...] + jnp.sum(hot, axis=1, keepdims=True)
    zi = jnp.zeros((SUBLANES - 4, t), I32)
    ri_ref[0] = jnp.concatenate([i1, i2, rank1.astype(I32), rank2.astype(I32), zi], axis=0)
    rw_ref[0] = jnp.concatenate([w1, w2, jnp.zeros((SUBLANES - 2, t), F32)], axis=0)
    cnt_ref[...] = jnp.broadcast_to(carry[...], cnt_ref.shape)


def _mix(x, mod3, g_pre, g_post, g_ffn, w_in, dwk, dwb, lng, lnb, lbl, rng, w_out, w_r, b_r):
    bsz, s, d = x.shape
    t = SEQ_TILE
    nt = s // t
    tile = lambda b, j: (b, j, 0)
    rtile = lambda b, j: (b * nt + j, 0, 0)
    const2 = lambda b, j: (0, 0)

    def const_spec(shape):
        return pl.BlockSpec(shape, const2, pipeline_mode=pl.Buffered(1))

    upper = jnp.triu(jnp.ones((t, t), BF16), k=1)
    return pl.pallas_call(
        _mix_kernel,
        grid=(bsz, nt),
        in_specs=[
            pl.BlockSpec((1, t, d), tile),
            pl.BlockSpec((1, 6, d), lambda b, j: (b, 0, 0)),
            const_spec((1, d)),
            const_spec((1, d)),
            const_spec((1, d)),
            const_spec((d, D_IN)),
            const_spec((CONV_WIDTH, D_CONV)),
            const_spec((1, D_CONV)),
            const_spec((1, D_CONV)),
            const_spec((1, D_CONV)),
            const_spec((2, D_REC)),
            const_spec((1, REC_DV)),
            const_spec((d, d)),
            pl.BlockSpec((2, d, LANES), lambda b, j: (0, 0, 0), pipeline_mode=pl.Buffered(1)),
            const_spec((1, LANES)),
            const_spec((t, t)),
        ],
        out_specs=[
            pl.BlockSpec((1, t, d), tile),
            pl.BlockSpec((1, t, d), tile),
            pl.BlockSpec((1, SUBLANES, t), rtile),
            pl.BlockSpec((1, SUBLANES, t), rtile),
            pl.BlockSpec((N_EXPERTS, LANES), const2),
        ],
        out_shape=[
            jax.ShapeDtypeStruct((bsz, s, d), F32),
            jax.ShapeDtypeStruct((bsz, s, d), F32),
            jax.ShapeDtypeStruct((bsz * nt, SUBLANES, t), I32),
            jax.ShapeDtypeStruct((bsz * nt, SUBLANES, t), F32),
            jax.ShapeDtypeStruct((N_EXPERTS, LANES), F32),
        ],
        scratch_shapes=[
            pltpu.VMEM((t, LANES), F32),
            pltpu.VMEM((t, LANES), F32),
            pltpu.VMEM((SUBLANES, d), F32),
            pltpu.VMEM((t, d), BF16),
            pltpu.VMEM((t, D_IN), F32),
            pltpu.VMEM((CONV_HALO + t, D_CONV), F32),
            pltpu.VMEM((t, D_CONV), F32),
            pltpu.VMEM((3, t, D_REC), BF16),
            pltpu.VMEM((t, D_REC), F32),
            pltpu.VMEM((t, D_REC), F32),
            pltpu.VMEM((t, D_REC), BF16),
            pltpu.VMEM((t, D_REC), BF16),
            pltpu.VMEM((t, D_REC), BF16),
            pltpu.VMEM((t, D_REC), BF16),
            pltpu.VMEM((t // CHUNK, 1, D_REC), F32),
            pltpu.VMEM((t, D_REC), F32),
            pltpu.VMEM((t, d), BF16),
            pltpu.VMEM((t, d), F32),
            pltpu.VMEM((REC_HEADS, REC_DV, REC_DK), F32),
            pltpu.VMEM((N_EXPERTS, 1), F32),
        ],
        compiler_params=pltpu.CompilerParams(
            dimension_semantics=("arbitrary", "arbitrary"),
            vmem_limit_bytes=VMEM_LIMIT),
        name="mixer",
    )(x, mod3, g_pre, g_post, g_ffn, w_in, dwk, dwb, lng, lnb, lbl, rng, w_out, w_r, b_r, upper)


def _meta_kernel(cnt_ref, start_ref, bexp_ref, nblk_ref):
    shift = EXPERT_ROWS.bit_length() - 1
    n_blocks = bexp_ref.shape[0]

    def fill(e):
        def body(jb, c_):
            bexp_ref[jb] = e
            return c_
        return body

    def per_expert(e, blk0):
        nb = (cnt_ref[e] + (EXPERT_ROWS - 1)) >> shift
        start_ref[e] = blk0 << shift
        lax.fori_loop(blk0, blk0 + nb, fill(e), 0)
        return blk0 + nb

    used = lax.fori_loop(0, N_EXPERTS, per_expert, jnp.int32(0))
    lax.fori_loop(used, n_blocks, fill(N_EXPERTS - 1), 0)
    nblk_ref[0] = used


def _meta(counts, n_blocks):
    smem = pl.BlockSpec(memory_space=pltpu.SMEM)
    return pl.pallas_call(
        _meta_kernel,
        in_specs=[smem],
        out_specs=[smem, smem, smem],
        out_shape=[
            jax.ShapeDtypeStruct((N_EXPERTS,), I32),
            jax.ShapeDtypeStruct((n_blocks,), I32),
            jax.ShapeDtypeStruct((1,), I32),
        ],
        name="moe_layout",
    )(counts)


def _positions_kernel(start_ref, ri_ref, pos_ref):
    e = ri_ref[:, 0:TOP_K, :]
    seg = jnp.zeros(e.shape, I32)
    for k in range(N_EXPERTS):
        seg = jnp.where(e == k, start_ref[k], seg)
    pos = seg + ri_ref[:, TOP_K:2 * TOP_K, :]
    pad = jnp.zeros((e.shape[0], SUBLANES - TOP_K, e.shape[2]), I32)
    pos_ref[...] = jnp.concatenate([pos, pad], axis=1)


def _positions(starts, ri):
    n_tiles, _, t = ri.shape
    tb = SUBLANES
    return pl.pallas_call(
        _positions_kernel,
        grid_spec=pltpu.PrefetchScalarGridSpec(
            num_scalar_prefetch=1,
            grid=(n_tiles // tb,),
            in_specs=[pl.BlockSpec((tb, SUBLANES, t), lambda i, s_: (i, 0, 0))],
            out_specs=pl.BlockSpec((tb, SUBLANES, t), lambda i, s_: (i, 0, 0)),
        ),
        out_shape=jax.ShapeDtypeStruct(ri.shape, I32),
        name="moe_positions",
    )(starts, ri)


def _row_copy(src_ref, src_row, dst_ref, dst_row, sem):
    return pltpu.make_async_copy(src_ref.at[pl.ds(src_row, 1)], dst_ref.at[pl.ds(dst_row, 1)], sem)


def _dispatch_kernel(start_ref, cnt_ref, nblk_ref, pos_ref, h2_ref, xs_ref, zeros, zsem, sems):
    i = pl.program_id(0)
    t = SEQ_TILE
    shift = EXPERT_ROWS.bit_length() - 1

    @pl.when(i == 0)
    def _():
        zeros[...] = jnp.zeros(zeros.shape, F32)

        def last_block(e):
            nb = (cnt_ref[e] + (EXPERT_ROWS - 1)) >> shift
            row0 = pl.multiple_of(start_ref[e] + ((nb - 1) << shift), EXPERT_ROWS)
            return nb, pltpu.make_async_copy(zeros, xs_ref.at[pl.ds(row0, EXPERT_ROWS)], zsem.at[e])

        for e in range(N_EXPERTS):
            nb, cp = last_block(e)

            @pl.when(nb > 0)
            def _():
                cp.start()

        for e in range(N_EXPERTS):
            nb, cp = last_block(e)

            @pl.when(nb > 0)
            def _():
                cp.wait()

        n_blocks = xs_ref.shape[0] // EXPERT_ROWS

        def tail_block(jb):
            row0 = pl.multiple_of(jb * EXPERT_ROWS, EXPERT_ROWS)
            return pltpu.make_async_copy(zeros, xs_ref.at[pl.ds(row0, EXPERT_ROWS)], zsem.at[jb - nblk_ref[0]])

        def tail_start(jb, c_):
            tail_block(jb).start()
            return c_

        def tail_wait(jb, c_):
            tail_block(jb).wait()
            return c_

        lax.fori_loop(nblk_ref[0], n_blocks, tail_start, 0)
        lax.fori_loop(nblk_ref[0], n_blocks, tail_wait, 0)

    def body(rb, c_):
        for u in range(ROW_UNROLL):
            r = rb * ROW_UNROLL + u
            for k in range(TOP_K):
                _row_copy(h2_ref, r, xs_ref, pos_ref[0, k, r], sems.at[0]).start()
        return c_

    lax.fori_loop(0, t // ROW_UNROLL, body, 0)
    for k in range(TOP_K):
        pltpu.make_async_copy(h2_ref, xs_ref.at[pl.ds(0, t)], sems.at[0]).wait()


def _dispatch(starts, counts, nblk, pos, h2, n_rows):
    n, d = h2.shape
    t = SEQ_TILE
    return pl.pallas_call(
        _dispatch_kernel,
        grid_spec=pltpu.PrefetchScalarGridSpec(
            num_scalar_prefetch=3,
            grid=(n // t,),
            in_specs=[
                pl.BlockSpec((1, SUBLANES, t), lambda i, s_, c_, n_: (i, 0, 0), memory_space=pltpu.SMEM),
                pl.BlockSpec((t, d), lambda i, s_, c_, n_: (i, 0)),
            ],
            out_specs=pl.BlockSpec(memory_space=pl.ANY),
            scratch_shapes=[
                pltpu.VMEM((EXPERT_ROWS, d), F32),
                pltpu.SemaphoreType.DMA((N_EXPERTS,)),
                pltpu.SemaphoreType.DMA((1,)),
            ],
        ),
        out_shape=jax.ShapeDtypeStruct((n_rows, d), F32),
        compiler_params=pltpu.CompilerParams(dimension_semantics=("arbitrary",)),
        name="moe_dispatch",
    )(starts, counts, nblk, pos, h2)


def _expert_kernel(bexp_ref, nblk_ref, x_ref, wg_ref, wu_ref, wd_ref, y_ref):
    jb = pl.program_id(0)

    @pl.when(jb < nblk_ref[0])
    def _():
        xb = x_ref[...].astype(BF16)
        gate = jnp.dot(xb, wg_ref[0], preferred_element_type=F32)
        up = jnp.dot(xb, wu_ref[0], preferred_element_type=F32)
        hb = (_silu(gate) * up).astype(BF16)
        y_ref[...] = jnp.dot(hb, wd_ref[0], preferred_element_type=F32)

    @pl.when(jb >= nblk_ref[0])
    def _():
        y_ref[...] = jnp.zeros(y_ref.shape, F32)


def _experts(bexp, nblk, xs, w_gate, w_up, w_down):
    n_rows, d = xs.shape
    n_blocks = n_rows // EXPERT_ROWS
    return pl.pallas_call(
        _expert_kernel,
        grid_spec=pltpu.PrefetchScalarGridSpec(
            num_scalar_prefetch=2,
            grid=(n_blocks,),
            in_specs=[
                pl.BlockSpec((EXPERT_ROWS, d), lambda jb, be, nb: (jnp.minimum(jb, nb[0] - 1), 0)),
                pl.BlockSpec((1, d, D_EXPERT), lambda jb, be, nb: (be[jb], 0, 0)),
                pl.BlockSpec((1, d, D_EXPERT), lambda jb, be, nb: (be[jb], 0, 0)),
                pl.BlockSpec((1, D_EXPERT, d), lambda jb, be, nb: (be[jb], 0, 0)),
            ],
            out_specs=pl.BlockSpec((EXPERT_ROWS, d), lambda jb, be, nb: (jb, 0)),
        ),
        out_shape=jax.ShapeDtypeStruct((n_rows, d), F32),
        compiler_params=pltpu.CompilerParams(dimension_semantics=("arbitrary",)),
        name="moe_experts",
    )(bexp, nblk, xs, w_gate, w_up, w_down)


def _combine_kernel(pos_ref, rw_ref, x1_ref, gt2_ref, gpost_ref, ys_ref, o_ref, ya, yb, sems):
    t = SEQ_TILE
    bufs = (ya, yb)

    def body(rb, c_):
        for u in range(ROW_UNROLL):
            r = rb * ROW_UNROLL + u
            for k in range(TOP_K):
                _row_copy(ys_ref, pos_ref[0, k, r], bufs[k], r, sems.at[0]).start()
        return c_

    lax.fori_loop(0, t // ROW_UNROLL, body, 0)
    for k in range(TOP_K):
        pltpu.make_async_copy(ys_ref.at[pl.ds(0, t)], bufs[k], sems.at[0]).wait()

    wt = rw_ref[0].T
    y = ya[...] * wt[:, 0:1] + yb[...] * wt[:, 1:2]
    o_ref[...] = x1_ref[...] + gt2_ref[0] * _rms(y, gpost_ref[...])


def _combine(pos, rw, x1, gt2, g_post, ys, seq):
    n, d = x1.shape
    t = SEQ_TILE
    per_batch = seq // t
    return pl.pallas_call(
        _combine_kernel,
        grid=(n // t,),
        in_specs=[
            pl.BlockSpec((1, SUBLANES, t), lambda i: (i, 0, 0), memory_space=pltpu.SMEM),
            pl.BlockSpec((1, SUBLANES, t), lambda i: (i, 0, 0)),
            pl.BlockSpec((t, d), lambda i: (i, 0)),
            pl.BlockSpec((1, 1, d), lambda i: (i // per_batch, 0, 0)),
            pl.BlockSpec((1, d), lambda i: (0, 0)),
            pl.BlockSpec(memory_space=pl.ANY),
        ],
        out_specs=pl.BlockSpec((t, d), lambda i: (i, 0)),
        scratch_shapes=[
            pltpu.VMEM((t, d), F32),
            pltpu.VMEM((t, d), F32),
            pltpu.SemaphoreType.DMA((1,)),
        ],
        out_shape=jax.ShapeDtypeStruct((n, d), F32),
        compiler_params=pltpu.CompilerParams(dimension_semantics=("arbitrary",)),
        name="moe_combine",
    )(pos, rw, x1, gt2, g_post, ys)


def kernel(x, c, w_ada, b_ada, g_pre_mix, g_post_mix, w_in, dw_kernel, dw_bias, conv_ln_gain, conv_ln_bias, lb_logits, rec_norm_gain, w_out, g_pre_ffn, g_post_ffn, w_router_group, b_router_group, w_router_expert, b_router_expert, w_gate, w_up, w_down):
    bsz, s, d = x.shape
    depth = w_ada.shape[0]
    assert depth == 1 and lb_logits.shape[0] == 2
    n_tok = bsz * s
    n_rows = n_tok * TOP_K + N_EXPERTS * EXPERT_ROWS
    for l in range(depth):
        mod = _ada(c, w_ada[l], b_ada[l])
        mod3 = mod.reshape(bsz, 6, d)
        pad = LANES - N_EXPERTS - N_GROUPS
        w_r = jnp.concatenate([w_router_expert[l], w_router_group[l], jnp.zeros((d, pad), F32)], axis=1)
        w_r_hi = w_r.astype(BF16)
        w_r = jnp.stack([w_r_hi, (w_r - w_r_hi.astype(F32)).astype(BF16)])
        b_r = jnp.concatenate([b_router_expert[l], b_router_group[l], jnp.zeros((pad,), F32)]).reshape(1, LANES)
        x1, h2, ri, rw, cnt = _mix(
            x, mod3, g_pre_mix[l].reshape(1, d), g_post_mix[l].reshape(1, d), g_pre_ffn[l].reshape(1, d),
            w_in[l].astype(BF16), dw_kernel[l], dw_bias[l].reshape(1, D_CONV),
            conv_ln_gain[l].reshape(1, D_CONV), conv_ln_bias[l].reshape(1, D_CONV),
            lb_logits, rec_norm_gain[l].reshape(1, REC_DV), w_out[l].astype(BF16), w_r, b_r)
        counts = cnt[:, 0].astype(I32)
        starts, bexp, nblk = _meta(counts, n_rows // EXPERT_ROWS)
        pos = _positions(starts, ri)
        xs = _dispatch(starts, counts, nblk, pos, h2.reshape(n_tok, d), n_rows)
        ys = _experts(bexp, nblk, xs, w_gate[l].astype(BF16), w_up[l].astype(BF16), w_down[l].astype(BF16))
        out = _combine(pos, rw, x1.reshape(n_tok, d), mod3[:, 5:6, :], g_post_ffn[l].reshape(1, d), ys, s)
        x = out.reshape(bsz, s, d)
    return x
```

```python
import jax
import jax.numpy as jnp
from jax import lax
from jax.experimental import pallas as pl
from jax.experimental.pallas import tpu as pltpu
from jax.experimental.pallas import tpu_sc as plsc

D_MODEL = 1024
D_CONV = 512
D_REC = 512
CONV_WIDTH = 31
REC_HEADS = 4
REC_DK = 128
REC_DV = 128
CHUNK = 64
D_IN = 2 * D_CONV + 4 * D_REC
N_GROUPS = 4
EXPERTS_PER_GROUP = 8
N_EXPERTS = 32
TOP_K = 2
D_EXPERT = 256
EPS = 1e-6

LANES = 128
SUBLANES = 8
SEQ_TILE = 512
CONV_ROWS = 64
ROW_STEP = 32
STAT_ROWS = 64
CONV_HALO = 32
GROUP_ROW0 = 32
EXPERT_ROWS = 256
SC_WINDOW = 32
VMEM_LIMIT = 56 * 1024 * 1024

F32 = jnp.float32
BF16 = jnp.bfloat16
I32 = jnp.int32
HI = lax.Precision.HIGHEST


def _sigmoid(v):
    return 1.0 / (1.0 + jnp.exp(-v))


def _silu(v):
    return v * _sigmoid(v)


def _rms(v, gain):
    return v * lax.rsqrt(jnp.mean(v * v, axis=-1, keepdims=True) + EPS) * gain


def _ada_kernel(c_ref, w_ref, b_ref, o_ref):
    cond = _silu(c_ref[...])
    o_ref[...] = jnp.dot(cond, w_ref[...], precision=HI, preferred_element_type=F32) + b_ref[...]


def _ada(c, w, b):
    bsz, d = c.shape
    n = w.shape[1]
    return pl.pallas_call(
        _ada_kernel,
        grid=(n // d,),
        in_specs=[
            pl.BlockSpec((bsz, d), lambda j: (0, 0)),
            pl.BlockSpec((d, d), lambda j: (0, j)),
            pl.BlockSpec((1, d), lambda j: (0, j)),
        ],
        out_specs=pl.BlockSpec((bsz, d), lambda j: (0, j)),
        out_shape=jax.ShapeDtypeStruct((bsz, n), F32),
        name="ada_mod",
    )(c, w, b.reshape(1, n))


def _mix_kernel(x_ref, mod_ref, gpre_ref, gpost_ref, gffn_ref, win_ref, dwk_ref, dwb_ref,
                lng_ref, lnb_ref, lbl_ref, rng_ref, wout_ref, wr_ref, br_ref, upper_ref,
                x1_ref, h2_ref, ri_ref, rw_ref, cnt_ref,
                st1, st2, vec_s, hb_s, proj_s, ubuf, cv_s, lfp_s, k_s, b_s, qd_s, ki_s, ke_s, v_s, a_s, o_s, yb, y_s,
                state, carry):
    b = pl.program_id(0)
    j = pl.program_id(1)
    t = SEQ_TILE
    n_chunks = t // CHUNK
    heads = range(REC_HEADS)

    @pl.when(j == 0)
    def _():
        ubuf[0:CONV_HALO, :] = jnp.zeros((CONV_HALO, D_CONV), F32)
        state[...] = jnp.zeros(state.shape, F32)

    @pl.when((j == 0) & (b == 0))
    def _():
        carry[...] = jnp.zeros(carry.shape, F32)

    sh1 = mod_ref[0, 0:1, :]
    sc1 = mod_ref[0, 1:2, :]
    gt1 = mod_ref[0, 2:3, :]
    sh2 = mod_ref[0, 3:4, :]
    sc2 = mod_ref[0, 4:5, :]

    def row_loop(rows_per_step, body):
        def step(ci, c_):
            body(pl.ds(pl.multiple_of(ci * rows_per_step, rows_per_step), rows_per_step))
            return c_
        lax.fori_loop(0, t // rows_per_step, step, 0)

    stat_blocks = [slice(r0, r0 + STAT_ROWS) for r0 in range(0, t, STAT_ROWS)]

    def head_cols(base, hd):
        return slice(base + hd * REC_DK, base + (hd + 1) * REC_DK)

    def lane_tiles(width):
        return [slice(l0, l0 + LANES) for l0 in range(0, width, LANES)]

    def put_stat(ref, rows, stat):
        ref[rows, :] = jnp.broadcast_to(stat, (STAT_ROWS, LANES))

    vec_s[0:1, :] = gpre_ref[...] * (1.0 + sc1)
    vec_s[1:2, :] = sh1
    vec_s[2:3, :] = gt1 * gpost_ref[...]
    vec_s[3:4, :] = gffn_ref[...] * (1.0 + sc2)
    vec_s[4:5, :] = sh2

    for rows in stat_blocks:
        xc = x_ref[0, rows, :]
        put_stat(st1, rows, lax.rsqrt(jnp.mean(xc * xc, axis=-1, keepdims=True) + EPS))

    def prenorm(rows):
        inv = st1[rows, :]
        for lanes in lane_tiles(D_MODEL):
            hb_s[rows, lanes] = (x_ref[0, rows, lanes] * inv * vec_s[0:1, lanes] + vec_s[1:2, lanes]).astype(BF16)

    row_loop(ROW_STEP, prenorm)
    proj_s[...] = jnp.dot(hb_s[...], win_ref[...], preferred_element_type=F32)

    def glu(rows):
        cv = proj_s[rows, 0:D_CONV]
        cg = proj_s[rows, D_CONV:2 * D_CONV]
        ubuf[pl.ds(pl.multiple_of(rows.start + CONV_HALO, ROW_STEP), ROW_STEP), :] = cv * _sigmoid(cg)

    row_loop(ROW_STEP, glu)

    lead = CONV_HALO - (CONV_WIDTH - 1)
    win_rows = CONV_ROWS + CONV_HALO

    def conv(rows):
        for lt in range(D_CONV // LANES):
            lanes = slice(lt * LANES, (lt + 1) * LANES)
            win = ubuf[pl.ds(rows.start, win_rows), lanes]
            acc = jnp.broadcast_to(dwb_ref[:, lanes], (CONV_ROWS, LANES))
            for res in range(SUBLANES):
                shifted = win if res == 0 else pltpu.roll(win, win_rows - res, axis=0)
                for al in range(0, win_rows - CONV_ROWS + 1, SUBLANES):
                    kk = al + res - lead
                    if 0 <= kk < CONV_WIDTH and al + CONV_ROWS + res <= win_rows:
                        acc = acc + shifted[al:al + CONV_ROWS] * dwk_ref[kk:kk + 1, lanes]
            cv_s[rows, lanes] = acc

    row_loop(CONV_ROWS, conv)
    ubuf[0:CONV_HALO, :] = ubuf[t:t + CONV_HALO, :]

    for rows in stat_blocks:
        acc = cv_s[rows, :]
        mu = jnp.mean(acc, axis=-1, keepdims=True)
        xc = acc - mu
        put_stat(st1, rows, mu)
        put_stat(st2, rows, lax.rsqrt(jnp.mean(xc * xc, axis=-1, keepdims=True) + EPS))

    def conv_norm(rows):
        mu = st1[rows, :]
        inv = st2[rows, :]
        for lanes in lane_tiles(D_CONV):
            yc = (cv_s[rows, lanes] - mu) * inv * lng_ref[:, lanes] + lnb_ref[:, lanes]
            yb[rows, lanes] = _silu(yc).astype(BF16)

    row_loop(ROW_STEP, conv_norm)

    q0 = 2 * D_CONV
    f0 = q0 + D_REC
    i0 = f0 + D_REC
    g0 = i0 + D_REC
    l0 = lbl_ref[0:1, :]
    lmax = jnp.max(lbl_ref[...], axis=0, keepdims=True)
    lb = jnp.exp(l0 - lmax) / jnp.sum(jnp.exp(lbl_ref[...] - lmax), axis=0, keepdims=True)

    def gates(rows):
        for hd in heads:
            cols = head_cols(0, hd)
            forget = lb[:, cols] + (1.0 - lb[:, cols]) * _sigmoid(proj_s[rows, head_cols(f0, hd)])
            k_s[rows, cols] = 1.0 - forget
            lf = jnp.log(forget)
            hi = lf.astype(BF16)
            rem = lf - hi.astype(F32)
            mid = rem.astype(BF16)
            lfp_s[0, rows, cols] = hi
            lfp_s[1, rows, cols] = mid
            lfp_s[2, rows, cols] = (rem - mid.astype(F32)).astype(BF16)

    row_loop(CHUNK, gates)

    row = lax.broadcasted_iota(I32, (CHUNK, CHUNK), 0)
    col = lax.broadcasted_iota(I32, (CHUNK, CHUNK), 1)
    causal = row >= col
    tri = jnp.where(causal, 1.0, 0.0).astype(BF16)
    for ci in range(n_chunks):
        rows = slice(ci * CHUNK, (ci + 1) * CHUNK)
        b_s[rows, :] = (jnp.dot(tri, lfp_s[0, rows, :], preferred_element_type=F32)
                        + jnp.dot(tri, lfp_s[1, rows, :], preferred_element_type=F32)
                        + jnp.dot(tri, lfp_s[2, rows, :], preferred_element_type=F32))

    def decays(ci, c_):
        rows = pl.ds(pl.multiple_of(ci * CHUNK, CHUNK), CHUNK)
        for hd in heads:
            cols = head_cols(0, hd)
            bcum = b_s[rows, cols]
            k_c = k_s[rows, cols]
            a_last = jnp.exp(bcum[CHUNK - 1:CHUNK, :])
            k_inv = k_c * jnp.exp(-bcum)
            qd_s[rows, cols] = (_silu(proj_s[rows, head_cols(q0, hd)]) * jnp.exp(bcum)).astype(BF16)
            ki_s[rows, cols] = k_inv.astype(BF16)
            ke_s[rows, cols] = (k_inv * a_last).astype(BF16)
            v_s[rows, cols] = proj_s[rows, head_cols(i0, hd)].astype(BF16)
            a_s[ci, :, cols] = a_last
        return c_

    lax.fori_loop(0, n_chunks, decays, 0)

    nt_dims = (((1,), (1,)), ((), ()))
    pairs = [(ci, hd) for ci in range(n_chunks) for hd in heads]

    def blk(ci, hd):
        return slice(ci * CHUNK, (ci + 1) * CHUNK), head_cols(0, hd)

    scores = {}
    for p in pairs:
        rows, cols = blk(*p)
        sc = lax.dot_general(qd_s[rows, cols], ki_s[rows, cols], nt_dims, preferred_element_type=F32)
        scores[p] = jnp.where(causal, sc, 0.0).astype(BF16)
    for p in pairs:
        rows, cols = blk(*p)
        o_s[rows, cols] = jnp.dot(scores[p], v_s[rows, cols], preferred_element_type=F32)
    upd = {}
    for p in pairs:
        rows, cols = blk(*p)
        v_t = v_s[rows, cols].astype(F32).T.astype(BF16)
        upd[p] = jnp.dot(v_t, ke_s[rows, cols], preferred_element_type=F32)
    prev = {}
    for hd in heads:
        st = state[hd]
        for ci in range(n_chunks):
            prev[(ci, hd)] = st.astype(BF16)
            st = st * a_s[ci, :, head_cols(0, hd)] + upd[(ci, hd)]
        state[hd] = st
    for p in pairs:
        rows, cols = blk(*p)
        o_s[rows, cols] += lax.dot_general(qd_s[rows, cols], prev[p], nt_dims, preferred_element_type=F32)

    for rows in stat_blocks:
        for hd in heads:
            o = _rms(o_s[rows, head_cols(0, hd)], rng_ref[...]) * _silu(proj_s[rows, head_cols(g0, hd)])
            yb[rows, head_cols(D_CONV, hd)] = o.astype(BF16)

    y_s[...] = jnp.dot(yb[...], wout_ref[...], preferred_element_type=F32)
    for rows in stat_blocks:
        yc = y_s[rows, :]
        put_stat(st1, rows, lax.rsqrt(jnp.mean(yc * yc, axis=-1, keepdims=True) + EPS))

    def residual(rows):
        inv = st1[rows, :]
        for lanes in lane_tiles(D_MODEL):
            x1_ref[0, rows, lanes] = x_ref[0, rows, lanes] + y_s[rows, lanes] * inv * vec_s[2:3, lanes]

    row_loop(ROW_STEP, residual)
    for rows in stat_blocks:
        xc = x1_ref[0, rows, :]
        put_stat(st2, rows, lax.rsqrt(jnp.mean(xc * xc, axis=-1, keepdims=True) + EPS))

    def ffn_norm(rows):
        inv = st2[rows, :]
        for lanes in lane_tiles(D_MODEL):
            h2_ref[0, rows, lanes] = x1_ref[0, rows, lanes] * inv * vec_s[3:4, lanes] + vec_s[4:5, lanes]

    row_loop(ROW_STEP, ffn_norm)

    h2v = h2_ref[0]
    h_hi = h2v.astype(BF16)
    h_lo = (h2v - h_hi.astype(F32)).astype(BF16)
    logits = (jnp.dot(h_hi, wr_ref[0], preferred_element_type=F32)
              + jnp.dot(h_hi, wr_ref[1], preferred_element_type=F32)
              + jnp.dot(h_lo, wr_ref[0], preferred_element_type=F32)) + br_ref[...]
    lt = logits.T
    neg = jnp.float32(-jnp.inf)
    r8 = lax.broadcasted_iota(I32, (SUBLANES, t), 0)
    gl = jnp.where(r8 < N_GROUPS, lt[GROUP_ROW0:GROUP_ROW0 + SUBLANES], neg)
    gmax = jnp.max(gl, axis=0, keepdims=True)
    gidx = jnp.min(jnp.where(gl == gmax, r8, SUBLANES), axis=0, keepdims=True)
    gprob = 1.0 / jnp.sum(jnp.exp(gl - gmax), axis=0, keepdims=True)
    re = lax.broadcasted_iota(I32, (N_EXPERTS, t), 0)
    el = jnp.where((re // EXPERTS_PER_GROUP) == gidx, lt[0:N_EXPERTS], neg)
    m1 = jnp.max(el, axis=0, keepdims=True)
    i1 = jnp.min(jnp.where(el == m1, re, N_EXPERTS), axis=0, keepdims=True)
    el2 = jnp.where(re == i1, neg, el)
    m2 = jnp.max(el2, axis=0, keepdims=True)
    i2 = jnp.min(jnp.where(el2 == m2, re, N_EXPERTS), axis=0, keepdims=True)
    r = jnp.exp(m2 - m1)
    w1 = gprob / (1.0 + r)
    w2 = gprob * r / (1.0 + r)
    hot1 = re == i1
    hot2 = re == i2
    hot = jnp.where(hot1 | hot2, 1.0, 0.0)
    prefix = jnp.dot(hot.astype(BF16), upper_ref[...], preferred_element_type=F32) + carry[...]
    rank1 = jnp.sum(jnp.where(hot1, prefix, 0.0), axis=0, keepdims=True)
    rank2 = jnp.sum(jnp.where(hot2, prefix, 0.0), axis=0, keepdims=True)
    carry[...] = carry[...] + jnp.sum(hot, axis=1, keepdims=True)
    zi = jnp.zeros((SUBLANES - 4, t), I32)
    ri_ref[0] = jnp.concatenate([i1, i2, rank1.astype(I32), rank2.astype(I32), zi], axis=0)
    rw_ref[0] = jnp.concatenate([w1, w2, jnp.zeros((SUBLANES - 2, t), F32)], axis=0)
    cnt_ref[...] = jnp.broadcast_to(carry[...], cnt_ref.shape)


def _mix(x, mod3, g_pre, g_post, g_ffn, w_in, dwk, dwb, lng, lnb, lbl, rng, w_out, w_r, b_r):
    bsz, s, d = x.shape
    t = SEQ_TILE
    nt = s // t
    tile = lambda b, j: (b, j, 0)
    rtile = lambda b, j: (b * nt + j, 0, 0)
    const2 = lambda b, j: (0, 0)

    def const_spec(shape):
        return pl.BlockSpec(shape, const2, pipeline_mode=pl.Buffered(1))

    upper = jnp.triu(jnp.ones((t, t), BF16), k=1)
    return pl.pallas_call(
        _mix_kernel,
        grid=(bsz, nt),
        in_specs=[
            pl.BlockSpec((1, t, d), tile),
            pl.BlockSpec((1, 6, d), lambda b, j: (b, 0, 0)),
            const_spec((1, d)),
            const_spec((1, d)),
            const_spec((1, d)),
            const_spec((d, D_IN)),
            const_spec((CONV_WIDTH, D_CONV)),
            const_spec((1, D_CONV)),
            const_spec((1, D_CONV)),
            const_spec((1, D_CONV)),
            const_spec((2, D_REC)),
            const_spec((1, REC_DV)),
            const_spec((d, d)),
            pl.BlockSpec((2, d, LANES), lambda b, j: (0, 0, 0), pipeline_mode=pl.Buffered(1)),
            const_spec((1, LANES)),
            const_spec((t, t)),
        ],
        out_specs=[
            pl.BlockSpec((1, t, d), tile),
            pl.BlockSpec((1, t, d), tile),
            pl.BlockSpec((1, SUBLANES, t), rtile),
            pl.BlockSpec((1, SUBLANES, t), rtile),
            pl.BlockSpec((N_EXPERTS, LANES), const2),
        ],
        out_shape=[
            jax.ShapeDtypeStruct((bsz, s, d), F32),
            jax.ShapeDtypeStruct((bsz, s, d), F32),
            jax.ShapeDtypeStruct((bsz * nt, SUBLANES, t), I32),
            jax.ShapeDtypeStruct((bsz * nt, SUBLANES, t), F32),
            jax.ShapeDtypeStruct((N_EXPERTS, LANES), F32),
        ],
        scratch_shapes=[
            pltpu.VMEM((t, LANES), F32),
            pltpu.VMEM((t, LANES), F32),
            pltpu.VMEM((SUBLANES, d), F32),
            pltpu.VMEM((t, d), BF16),
            pltpu.VMEM((t, D_IN), F32),
            pltpu.VMEM((CONV_HALO + t, D_CONV), F32),
            pltpu.VMEM((t, D_CONV), F32),
            pltpu.VMEM((3, t, D_REC), BF16),
            pltpu.VMEM((t, D_REC), F32),
            pltpu.VMEM((t, D_REC), F32),
            pltpu.VMEM((t, D_REC), BF16),
            pltpu.VMEM((t, D_REC), BF16),
            pltpu.VMEM((t, D_REC), BF16),
            pltpu.VMEM((t, D_REC), BF16),
            pltpu.VMEM((t // CHUNK, 1, D_REC), F32),
            pltpu.VMEM((t, D_REC), F32),
            pltpu.VMEM((t, d), BF16),
            pltpu.VMEM((t, d), F32),
            pltpu.VMEM((REC_HEADS, REC_DV, REC_DK), F32),
            pltpu.VMEM((N_EXPERTS, 1), F32),
        ],
        compiler_params=pltpu.CompilerParams(
            dimension_semantics=("arbitrary", "arbitrary"),
            vmem_limit_bytes=VMEM_LIMIT),
        name="mixer",
    )(x, mod3, g_pre, g_post, g_ffn, w_in, dwk, dwb, lng, lnb, lbl, rng, w_out, w_r, b_r, upper)


def _meta_kernel(cnt_ref, start_ref, bexp_ref, bvalid_ref, nblk_ref):
    shift = EXPERT_ROWS.bit_length() - 1
    n_blocks = bexp_ref.shape[0]

    def fill(e, blk0, cnt):
        def body(jb, c_):
            bexp_ref[jb] = e
            bvalid_ref[jb] = jnp.clip(cnt - ((jb - blk0) << shift), 0, EXPERT_ROWS)
            return c_
        return body

    def per_expert(e, blk0):
        nb = (cnt_ref[e] + (EXPERT_ROWS - 1)) >> shift
        start_ref[e] = blk0 << shift
        lax.fori_loop(blk0, blk0 + nb, fill(e, blk0, cnt_ref[e]), 0)
        return blk0 + nb

    used = lax.fori_loop(0, N_EXPERTS, per_expert, jnp.int32(0))
    lax.fori_loop(used, n_blocks, fill(N_EXPERTS - 1, used, 0), 0)
    nblk_ref[0] = used


def _meta(counts, n_blocks):
    smem = pl.BlockSpec(memory_space=pltpu.SMEM)
    return pl.pallas_call(
        _meta_kernel,
        in_specs=[smem],
        out_specs=[smem, smem, smem, smem],
        out_shape=[
            jax.ShapeDtypeStruct((N_EXPERTS,), I32),
            jax.ShapeDtypeStruct((n_blocks,), I32),
            jax.ShapeDtypeStruct((n_blocks,), I32),
            jax.ShapeDtypeStruct((1,), I32),
        ],
        name="moe_layout",
    )(counts)


def _positions_kernel(start_ref, ri_ref, pos_ref):
    e = ri_ref[:, 0:TOP_K, :]
    seg = jnp.zeros(e.shape, I32)
    for k in range(N_EXPERTS):
        seg = jnp.where(e == k, start_ref[k], seg)
    pos = seg + ri_ref[:, TOP_K:2 * TOP_K, :]
    pad = jnp.zeros((e.shape[0], SUBLANES - TOP_K, e.shape[2]), I32)
    pos_ref[...] = jnp.concatenate([pos, pad], axis=1)


def _positions(starts, ri):
    n_tiles, _, t = ri.shape
    tb = SUBLANES
    return pl.pallas_call(
        _positions_kernel,
        grid_spec=pltpu.PrefetchScalarGridSpec(
            num_scalar_prefetch=1,
            grid=(n_tiles // tb,),
            in_specs=[pl.BlockSpec((tb, SUBLANES, t), lambda i, s_: (i, 0, 0))],
            out_specs=pl.BlockSpec((tb, SUBLANES, t), lambda i, s_: (i, 0, 0)),
        ),
        out_shape=jax.ShapeDtypeStruct(ri.shape, I32),
        name="moe_positions",
    )(starts, ri)


_SC_MESH = dict(core_axis_name="core", subcore_axis_name="subcore")
SC_WORKERS = 32
SC_INDEX_CHUNK = 128


def _sc_worker_base(rows_per_worker):
    wid = lax.axis_index("core") * (SC_WORKERS // 2) + lax.axis_index("subcore")
    return wid * rows_per_worker


def _sc_dispatch(h2, pos_a, pos_b, n_rows):
    n, d = h2.shape
    w = SC_WINDOW
    per_worker = n // SC_WORKERS
    chunks = per_worker // SC_INDEX_CHUNK

    @pl.kernel(out_type=jax.ShapeDtypeStruct((n_rows, d), h2.dtype),
               mesh=plsc.VectorSubcoreMesh(**_SC_MESH),
               scratch_types=[pltpu.VMEM((SC_INDEX_CHUNK,), I32), pltpu.VMEM((SC_INDEX_CHUNK,), I32),
                              pltpu.VMEM((w, d), h2.dtype)],
               name="moe_dispatch_sc")
    def run(x_hbm, ia_hbm, ib_hbm, o_hbm, ia_v, ib_v, buf):
        base = _sc_worker_base(per_worker)

        @pl.loop(0, chunks)
        def _(c):
            row0 = base + c * SC_INDEX_CHUNK
            pltpu.sync_copy(ia_hbm.at[pl.ds(row0, SC_INDEX_CHUNK)], ia_v)
            pltpu.sync_copy(ib_hbm.at[pl.ds(row0, SC_INDEX_CHUNK)], ib_v)
            for k in range(SC_INDEX_CHUNK // w):
                pltpu.sync_copy(x_hbm.at[pl.ds(row0 + k * w, w)], buf)
                pltpu.sync_copy(buf, o_hbm.at[ia_v.at[pl.ds(k * w, w)]])
                pltpu.sync_copy(buf, o_hbm.at[ib_v.at[pl.ds(k * w, w)]])

    return run(h2, pos_a, pos_b)


def _sc_gather(table, idx):
    m = idx.shape[0]
    d = table.shape[1]
    w = SC_WINDOW
    per_worker = m // SC_WORKERS
    chunks = per_worker // SC_INDEX_CHUNK

    @pl.kernel(out_type=jax.ShapeDtypeStruct((m, d), table.dtype),
               mesh=plsc.VectorSubcoreMesh(**_SC_MESH),
               scratch_types=[pltpu.VMEM((SC_INDEX_CHUNK,), I32), pltpu.VMEM((w, d), table.dtype)],
               name="moe_gather_sc")
    def run(x_hbm, i_hbm, o_hbm, i_v, buf):
        base = _sc_worker_base(per_worker)

        @pl.loop(0, chunks)
        def _(c):
            row0 = base + c * SC_INDEX_CHUNK
            pltpu.sync_copy(i_hbm.at[pl.ds(row0, SC_INDEX_CHUNK)], i_v)
            for k in range(SC_INDEX_CHUNK // w):
                pltpu.sync_copy(x_hbm.at[i_v.at[pl.ds(k * w, w)]], buf)
                pltpu.sync_copy(buf, o_hbm.at[pl.ds(row0 + k * w, w)])

    return run(table, idx)


def _expert_kernel(bexp_ref, bvalid_ref, nblk_ref, x_ref, wg_ref, wu_ref, wd_ref, y_ref):
    jb = pl.program_id(0)

    @pl.when(jb < nblk_ref[0])
    def _():
        live = lax.broadcasted_iota(I32, (EXPERT_ROWS, 1), 0) < bvalid_ref[jb]
        xb = jnp.where(live, x_ref[...], 0.0).astype(BF16)
        gate = jnp.dot(xb, wg_ref[0], preferred_element_type=F32)
        up = jnp.dot(xb, wu_ref[0], preferred_element_type=F32)
        hb = (_silu(gate) * up).astype(BF16)
        y_ref[...] = jnp.dot(hb, wd_ref[0], preferred_element_type=F32)

    @pl.when(jb >= nblk_ref[0])
    def _():
        y_ref[...] = jnp.zeros(y_ref.shape, F32)


def _experts(bexp, bvalid, nblk, xs, w_gate, w_up, w_down):
    n_rows, d = xs.shape
    n_blocks = n_rows // EXPERT_ROWS
    return pl.pallas_call(
        _expert_kernel,
        grid_spec=pltpu.PrefetchScalarGridSpec(
            num_scalar_prefetch=3,
            grid=(n_blocks,),
            in_specs=[
                pl.BlockSpec((EXPERT_ROWS, d), lambda jb, be, bv, nb: (jnp.minimum(jb, nb[0] - 1), 0)),
                pl.BlockSpec((1, d, D_EXPERT), lambda jb, be, bv, nb: (be[jb], 0, 0)),
                pl.BlockSpec((1, d, D_EXPERT), lambda jb, be, bv, nb: (be[jb], 0, 0)),
                pl.BlockSpec((1, D_EXPERT, d), lambda jb, be, bv, nb: (be[jb], 0, 0)),
            ],
            out_specs=pl.BlockSpec((EXPERT_ROWS, d), lambda jb, be, bv, nb: (jb, 0)),
        ),
        out_shape=jax.ShapeDtypeStruct((n_rows, d), F32),
        compiler_params=pltpu.CompilerParams(dimension_semantics=("arbitrary",)),
        name="moe_experts",
    )(bexp, bvalid, nblk, xs, w_gate, w_up, w_down)


def _combine_kernel(rw_ref, x1_ref, gt2_ref, gpost_ref, ya_ref, yb_ref, o_ref):
    wt = rw_ref[0].T
    y = ya_ref[...] * wt[:, 0:1] + yb_ref[...] * wt[:, 1:2]
    o_ref[...] = x1_ref[...] + gt2_ref[0] * _rms(y, gpost_ref[...])


def _combine(rw, x1, gt2, g_post, yg, seq):
    n, d = x1.shape
    t = SEQ_TILE
    per_batch = seq // t
    n_tiles = n // t
    return pl.pallas_call(
        _combine_kernel,
        grid=(n_tiles,),
        in_specs=[
            pl.BlockSpec((1, SUBLANES, t), lambda i: (i, 0, 0)),
            pl.BlockSpec((t, d), lambda i: (i, 0)),
            pl.BlockSpec((1, 1, d), lambda i: (i // per_batch, 0, 0)),
            pl.BlockSpec((1, d), lambda i: (0, 0)),
            pl.BlockSpec((t, d), lambda i: (i, 0)),
            pl.BlockSpec((t, d), lambda i: (i + n_tiles, 0)),
        ],
        out_specs=pl.BlockSpec((t, d), lambda i: (i, 0)),
        out_shape=jax.ShapeDtypeStruct((n, d), F32),
        compiler_params=pltpu.CompilerParams(dimension_semantics=("arbitrary",)),
        name="moe_combine",
    )(rw, x1, gt2, g_post, yg, yg)


def kernel(x, c, w_ada, b_ada, g_pre_mix, g_post_mix, w_in, dw_kernel, dw_bias, conv_ln_gain, conv_ln_bias, lb_logits, rec_norm_gain, w_out, g_pre_ffn, g_post_ffn, w_router_group, b_router_group, w_router_expert, b_router_expert, w_gate, w_up, w_down):
    bsz, s, d = x.shape
    depth = w_ada.shape[0]
    assert depth == 1 and lb_logits.shape[0] == 2
    n_tok = bsz * s
    n_rows = n_tok * TOP_K + N_EXPERTS * EXPERT_ROWS
    for l in range(depth):
        mod = _ada(c, w_ada[l], b_ada[l])
        mod3 = mod.reshape(bsz, 6, d)
        pad = LANES - N_EXPERTS - N_GROUPS
        w_r = jnp.concatenate([w_router_expert[l], w_router_group[l], jnp.zeros((d, pad), F32)], axis=1)
        w_r_hi = w_r.astype(BF16)
        w_r = jnp.stack([w_r_hi, (w_r - w_r_hi.astype(F32)).astype(BF16)])
        b_r = jnp.concatenate([b_router_expert[l], b_router_group[l], jnp.zeros((pad,), F32)]).reshape(1, LANES)
        x1, h2, ri, rw, cnt = _mix(
            x, mod3, g_pre_mix[l].reshape(1, d), g_post_mix[l].reshape(1, d), g_pre_ffn[l].reshape(1, d),
            w_in[l].astype(BF16), dw_kernel[l], dw_bias[l].reshape(1, D_CONV),
            conv_ln_gain[l].reshape(1, D_CONV), conv_ln_bias[l].reshape(1, D_CONV),
            lb_logits, rec_norm_gain[l].reshape(1, REC_DV), w_out[l].astype(BF16), w_r, b_r)
        counts = cnt[:, 0].astype(I32)
        starts, bexp, bvalid, nblk = _meta(counts, n_rows // EXPERT_ROWS)
        pos = _positions(starts, ri)
        pos_a = pos[:, 0, :].reshape(n_tok)
        pos_b = pos[:, 1, :].reshape(n_tok)
        xs = _sc_dispatch(h2.reshape(n_tok, d), pos_a, pos_b, n_rows)
        ys = _experts(bexp, bvalid, nblk, xs, w_gate[l].astype(BF16), w_up[l].astype(BF16), w_down[l].astype(BF16))
        yg = _sc_gather(ys, jnp.concatenate([pos_a, pos_b]))
        out = _combine(rw, x1.reshape(n_tok, d), mod3[:, 5:6, :], g_post_ffn[l].reshape(1, d), yg, s)
        x = out.reshape(bsz, s, d)
    return x
```

```python
import jax
import jax.numpy as jnp
from jax import lax
from jax.experimental import pallas as pl
from jax.experimental.pallas import tpu as pltpu
from jax.experimental.pallas import tpu_sc as plsc

D_MODEL = 1024
D_CONV = 512
D_REC = 512
CONV_WIDTH = 31
REC_HEADS = 4
REC_DK = 128
REC_DV = 128
CHUNK = 64
D_IN = 2 * D_CONV + 4 * D_REC
N_GROUPS = 4
EXPERTS_PER_GROUP = 8
N_EXPERTS = 32
TOP_K = 2
D_EXPERT = 256
EPS = 1e-6

LANES = 128
SUBLANES = 8
SEQ_TILE = 512
CONV_ROWS = 64
ROW_STEP = 32
STAT_ROWS = 64
CONV_HALO = 32
GROUP_ROW0 = 32
EXPERT_ROWS = 256
SC_WINDOW = 64
D_PACK = D_MODEL // 2
VMEM_LIMIT = 56 * 1024 * 1024

F32 = jnp.float32
BF16 = jnp.bfloat16
I32 = jnp.int32
U32 = jnp.uint32
HI = lax.Precision.HIGHEST


def _sigmoid(v):
    return 1.0 / (1.0 + jnp.exp(-v))


def _silu(v):
    return v * _sigmoid(v)


def _rms(v, gain):
    return v * lax.rsqrt(jnp.mean(v * v, axis=-1, keepdims=True) + EPS) * gain


def _pack_pair(lo, hi):
    lo_bits = lax.bitcast_convert_type(lo.astype(BF16).astype(F32), U32)
    hi_bits = lax.bitcast_convert_type(hi.astype(BF16).astype(F32), U32)
    return (lo_bits >> 16) | (hi_bits & jnp.uint32(0xFFFF0000))


def _unpack_pair(packed):
    lo = lax.bitcast_convert_type(packed << 16, F32)
    hi = lax.bitcast_convert_type(packed & jnp.uint32(0xFFFF0000), F32)
    return lo, hi


def _ada_kernel(c_ref, w_ref, b_ref, o_ref):
    cond = _silu(c_ref[...])
    o_ref[...] = jnp.dot(cond, w_ref[...], precision=HI, preferred_element_type=F32) + b_ref[...]


def _ada(c, w, b):
    bsz, d = c.shape
    n = w.shape[1]
    return pl.pallas_call(
        _ada_kernel,
        grid=(n // d,),
        in_specs=[
            pl.BlockSpec((bsz, d), lambda j: (0, 0)),
            pl.BlockSpec((d, d), lambda j: (0, j)),
            pl.BlockSpec((1, d), lambda j: (0, j)),
        ],
        out_specs=pl.BlockSpec((bsz, d), lambda j: (0, j)),
        out_shape=jax.ShapeDtypeStruct((bsz, n), F32),
        name="ada_mod",
    )(c, w, b.reshape(1, n))


def _mix_kernel(x_ref, mod_ref, gpre_ref, gpost_ref, gffn_ref, win_ref, dwk_ref, dwb_ref,
                lng_ref, lnb_ref, lbl_ref, rng_ref, wout_ref, wr_ref, br_ref, upper_ref,
                x1_ref, h2p_ref, ri_ref, rw_ref, cnt_ref,
                h2_s, st1, st2, vec_s, hb_s, proj_s, ubuf, cv_s, lfp_s, k_s, b_s, qd_s, ki_s, ke_s, v_s, a_s, o_s, yb, y_s,
                state, carry):
    b = pl.program_id(0)
    j = pl.program_id(1)
    t = SEQ_TILE
    n_chunks = t // CHUNK
    heads = range(REC_HEADS)

    @pl.when(j == 0)
    def _():
        ubuf[0:CONV_HALO, :] = jnp.zeros((CONV_HALO, D_CONV), F32)
        state[...] = jnp.zeros(state.shape, F32)

    @pl.when((j == 0) & (b == 0))
    def _():
        carry[...] = jnp.zeros(carry.shape, F32)

    sh1 = mod_ref[0, 0:1, :]
    sc1 = mod_ref[0, 1:2, :]
    gt1 = mod_ref[0, 2:3, :]
    sh2 = mod_ref[0, 3:4, :]
    sc2 = mod_ref[0, 4:5, :]

    def row_loop(rows_per_step, body):
        def step(ci, c_):
            body(pl.ds(pl.multiple_of(ci * rows_per_step, rows_per_step), rows_per_step))
            return c_
        lax.fori_loop(0, t // rows_per_step, step, 0)

    stat_blocks = [slice(r0, r0 + STAT_ROWS) for r0 in range(0, t, STAT_ROWS)]

    def head_cols(base, hd):
        return slice(base + hd * REC_DK, base + (hd + 1) * REC_DK)

    def lane_tiles(width):
        return [slice(l0, l0 + LANES) for l0 in range(0, width, LANES)]

    def put_stat(ref, rows, stat):
        ref[rows, :] = jnp.broadcast_to(stat, (STAT_ROWS, LANES))

    vec_s[0:1, :] = gpre_ref[...] * (1.0 + sc1)
    vec_s[1:2, :] = sh1
    vec_s[2:3, :] = gt1 * gpost_ref[...]
    vec_s[3:4, :] = gffn_ref[...] * (1.0 + sc2)
    vec_s[4:5, :] = sh2

    for rows in stat_blocks:
        xc = x_ref[0, rows, :]
        put_stat(st1, rows, lax.rsqrt(jnp.mean(xc * xc, axis=-1, keepdims=True) + EPS))

    def prenorm(rows):
        inv = st1[rows, :]
        for lanes in lane_tiles(D_MODEL):
            hb_s[rows, lanes] = (x_ref[0, rows, lanes] * inv * vec_s[0:1, lanes] + vec_s[1:2, lanes]).astype(BF16)

    row_loop(ROW_STEP, prenorm)
    proj_s[...] = jnp.dot(hb_s[...], win_ref[...], preferred_element_type=F32)

    def glu(rows):
        cv = proj_s[rows, 0:D_CONV]
        cg = proj_s[rows, D_CONV:2 * D_CONV]
        ubuf[pl.ds(pl.multiple_of(rows.start + CONV_HALO, ROW_STEP), ROW_STEP), :] = cv * _sigmoid(cg)

    row_loop(ROW_STEP, glu)

    lead = CONV_HALO - (CONV_WIDTH - 1)
    win_rows = CONV_ROWS + CONV_HALO

    def conv(rows):
        for lt in range(D_CONV // LANES):
            lanes = slice(lt * LANES, (lt + 1) * LANES)
            win = ubuf[pl.ds(rows.start, win_rows), lanes]
            acc = jnp.broadcast_to(dwb_ref[:, lanes], (CONV_ROWS, LANES))
            for res in range(SUBLANES):
                shifted = win if res == 0 else pltpu.roll(win, win_rows - res, axis=0)
                for al in range(0, win_rows - CONV_ROWS + 1, SUBLANES):
                    kk = al + res - lead
                    if 0 <= kk < CONV_WIDTH and al + CONV_ROWS + res <= win_rows:
                        acc = acc + shifted[al:al + CONV_ROWS] * dwk_ref[kk:kk + 1, lanes]
            cv_s[rows, lanes] = acc

    row_loop(CONV_ROWS, conv)
    ubuf[0:CONV_HALO, :] = ubuf[t:t + CONV_HALO, :]

    for rows in stat_blocks:
        acc = cv_s[rows, :]
        mu = jnp.mean(acc, axis=-1, keepdims=True)
        xc = acc - mu
        put_stat(st1, rows, mu)
        put_stat(st2, rows, lax.rsqrt(jnp.mean(xc * xc, axis=-1, keepdims=True) + EPS))

    def conv_norm(rows):
        mu = st1[rows, :]
        inv = st2[rows, :]
        for lanes in lane_tiles(D_CONV):
            yc = (cv_s[rows, lanes] - mu) * inv * lng_ref[:, lanes] + lnb_ref[:, lanes]
            yb[rows, lanes] = _silu(yc).astype(BF16)

    row_loop(ROW_STEP, conv_norm)

    q0 = 2 * D_CONV
    f0 = q0 + D_REC
    i0 = f0 + D_REC
    g0 = i0 + D_REC
    l0 = lbl_ref[0:1, :]
    lmax = jnp.max(lbl_ref[...], axis=0, keepdims=True)
    lb = jnp.exp(l0 - lmax) / jnp.sum(jnp.exp(lbl_ref[...] - lmax), axis=0, keepdims=True)

    def gates(rows):
        for hd in heads:
            cols = head_cols(0, hd)
            forget = lb[:, cols] + (1.0 - lb[:, cols]) * _sigmoid(proj_s[rows, head_cols(f0, hd)])
            k_s[rows, cols] = 1.0 - forget
            lf = jnp.log(forget)
            hi = lf.astype(BF16)
            rem = lf - hi.astype(F32)
            mid = rem.astype(BF16)
            lfp_s[0, rows, cols] = hi
            lfp_s[1, rows, cols] = mid
            lfp_s[2, rows, cols] = (rem - mid.astype(F32)).astype(BF16)

    row_loop(CHUNK, gates)

    row = lax.broadcasted_iota(I32, (CHUNK, CHUNK), 0)
    col = lax.broadcasted_iota(I32, (CHUNK, CHUNK), 1)
    causal = row >= col
    tri = jnp.where(causal, 1.0, 0.0).astype(BF16)
    for ci in range(n_chunks):
        rows = slice(ci * CHUNK, (ci + 1) * CHUNK)
        b_s[rows, :] = (jnp.dot(tri, lfp_s[0, rows, :], preferred_element_type=F32)
                        + jnp.dot(tri, lfp_s[1, rows, :], preferred_element_type=F32)
                        + jnp.dot(tri, lfp_s[2, rows, :], preferred_element_type=F32))

    def decays(ci, c_):
        rows = pl.ds(pl.multiple_of(ci * CHUNK, CHUNK), CHUNK)
        for hd in heads:
            cols = head_cols(0, hd)
            bcum = b_s[rows, cols]
            k_c = k_s[rows, cols]
            a_last = jnp.exp(bcum[CHUNK - 1:CHUNK, :])
            k_inv = k_c * jnp.exp(-bcum)
            qd_s[rows, cols] = (_silu(proj_s[rows, head_cols(q0, hd)]) * jnp.exp(bcum)).astype(BF16)
            ki_s[rows, cols] = k_inv.astype(BF16)
            ke_s[rows, cols] = (k_inv * a_last).astype(BF16)
            v_s[rows, cols] = proj_s[rows, head_cols(i0, hd)].astype(BF16)
            a_s[ci, :, cols] = a_last
        return c_

    lax.fori_loop(0, n_chunks, decays, 0)

    nt_dims = (((1,), (1,)), ((), ()))
    pairs = [(ci, hd) for ci in range(n_chunks) for hd in heads]

    def blk(ci, hd):
        return slice(ci * CHUNK, (ci + 1) * CHUNK), head_cols(0, hd)

    scores = {}
    for p in pairs:
        rows, cols = blk(*p)
        sc = lax.dot_general(qd_s[rows, cols], ki_s[rows, cols], nt_dims, preferred_element_type=F32)
        scores[p] = jnp.where(causal, sc, 0.0).astype(BF16)
    for p in pairs:
        rows, cols = blk(*p)
        o_s[rows, cols] = jnp.dot(scores[p], v_s[rows, cols], preferred_element_type=F32)
    upd = {}
    for p in pairs:
        rows, cols = blk(*p)
        v_t = v_s[rows, cols].astype(F32).T.astype(BF16)
        upd[p] = jnp.dot(v_t, ke_s[rows, cols], preferred_element_type=F32)
    prev = {}
    for hd in heads:
        st = state[hd]
        for ci in range(n_chunks):
            prev[(ci, hd)] = st.astype(BF16)
            st = st * a_s[ci, :, head_cols(0, hd)] + upd[(ci, hd)]
        state[hd] = st
    for p in pairs:
        rows, cols = blk(*p)
        o_s[rows, cols] += lax.dot_general(qd_s[rows, cols], prev[p], nt_dims, preferred_element_type=F32)

    for rows in stat_blocks:
        for hd in heads:
            o = _rms(o_s[rows, head_cols(0, hd)], rng_ref[...]) * _silu(proj_s[rows, head_cols(g0, hd)])
            yb[rows, head_cols(D_CONV, hd)] = o.astype(BF16)

    y_s[...] = jnp.dot(yb[...], wout_ref[...], preferred_element_type=F32)
    for rows in stat_blocks:
        yc = y_s[rows, :]
        put_stat(st1, rows, lax.rsqrt(jnp.mean(yc * yc, axis=-1, keepdims=True) + EPS))

    def residual(rows):
        inv = st1[rows, :]
        for lanes in lane_tiles(D_MODEL):
            x1_ref[0, rows, lanes] = x_ref[0, rows, lanes] + y_s[rows, lanes] * inv * vec_s[2:3, lanes]

    row_loop(ROW_STEP, residual)
    for rows in stat_blocks:
        xc = x1_ref[0, rows, :]
        put_stat(st2, rows, lax.rsqrt(jnp.mean(xc * xc, axis=-1, keepdims=True) + EPS))

    def ffn_norm(rows):
        inv = st2[rows, :]
        for lanes in lane_tiles(D_PACK):
            upper = slice(lanes.start + D_PACK, lanes.stop + D_PACK)
            lo = x1_ref[0, rows, lanes] * inv * vec_s[3:4, lanes] + vec_s[4:5, lanes]
            hi = x1_ref[0, rows, upper] * inv * vec_s[3:4, upper] + vec_s[4:5, upper]
            h2_s[rows, lanes] = lo
            h2_s[rows, upper] = hi
            h2p_ref[0, rows, lanes] = _pack_pair(lo, hi)

    row_loop(ROW_STEP, ffn_norm)

    h2v = h2_s[...]
    h_hi = h2v.astype(BF16)
    h_lo = (h2v - h_hi.astype(F32)).astype(BF16)
    logits = (jnp.dot(h_hi, wr_ref[0], preferred_element_type=F32)
              + jnp.dot(h_hi, wr_ref[1], preferred_element_type=F32)
              + jnp.dot(h_lo, wr_ref[0], preferred_element_type=F32)) + br_ref[...]
    lt = logits.T
    neg = jnp.float32(-jnp.inf)
    r8 = lax.broadcasted_iota(I32, (SUBLANES, t), 0)
    gl = jnp.where(r8 < N_GROUPS, lt[GROUP_ROW0:GROUP_ROW0 + SUBLANES], neg)
    gmax = jnp.max(gl, axis=0, keepdims=True)
    gidx = jnp.min(jnp.where(gl == gmax, r8, SUBLANES), axis=0, keepdims=True)
    gprob = 1.0 / jnp.sum(jnp.exp(gl - gmax), axis=0, keepdims=True)
    re = lax.broadcasted_iota(I32, (N_EXPERTS, t), 0)
    el = jnp.where((re // EXPERTS_PER_GROUP) == gidx, lt[0:N_EXPERTS], neg)
    m1 = jnp.max(el, axis=0, keepdims=True)
    i1 = jnp.min(jnp.where(el == m1, re, N_EXPERTS), axis=0, keepdims=True)
    el2 = jnp.where(re == i1, neg, el)
    m2 = jnp.max(el2, axis=0, keepdims=True)
    i2 = jnp.min(jnp.where(el2 == m2, re, N_EXPERTS), axis=0, keepdims=True)
    r = jnp.exp(m2 - m1)
    w1 = gprob / (1.0 + r)
    w2 = gprob * r / (1.0 + r)
    hot1 = re == i1
    hot2 = re == i2
    hot = jnp.where(hot1 | hot2, 1.0, 0.0)
    prefix = jnp.dot(hot.astype(BF16), upper_ref[...], preferred_element_type=F32) + carry[...]
    rank1 = jnp.sum(jnp.where(hot1, prefix, 0.0), axis=0, keepdims=True)
    rank2 = jnp.sum(jnp.where(hot2, prefix, 0.0), axis=0, keepdims=True)
    carry[...] = carry[...] + jnp.sum(hot, axis=1, keepdims=True)
    zi = jnp.zeros((SUBLANES - 4, t), I32)
    ri_ref[0] = jnp.concatenate([i1, i2, rank1.astype(I32), rank2.astype(I32), zi], axis=0)
    rw_ref[0] = jnp.concatenate([w1, w2, jnp.zeros((SUBLANES - 2, t), F32)], axis=0)
    cnt_ref[...] = jnp.broadcast_to(carry[...], cnt_ref.shape)


def _mix(x, mod3, g_pre, g_post, g_ffn, w_in, dwk, dwb, lng, lnb, lbl, rng, w_out, w_r, b_r):
    bsz, s, d = x.shape
    t = SEQ_TILE
    nt = s // t
    tile = lambda b, j: (b, j, 0)
    rtile = lambda b, j: (b * nt + j, 0, 0)
    const2 = lambda b, j: (0, 0)

    def const_spec(shape):
        return pl.BlockSpec(shape, const2, pipeline_mode=pl.Buffered(1))

    upper = jnp.triu(jnp.ones((t, t), BF16), k=1)
    return pl.pallas_call(
        _mix_kernel,
        grid=(bsz, nt),
        in_specs=[
            pl.BlockSpec((1, t, d), tile),
            pl.BlockSpec((1, 6, d), lambda b, j: (b, 0, 0)),
            const_spec((1, d)),
            const_spec((1, d)),
            const_spec((1, d)),
            const_spec((d, D_IN)),
            const_spec((CONV_WIDTH, D_CONV)),
            const_spec((1, D_CONV)),
            const_spec((1, D_CONV)),
            const_spec((1, D_CONV)),
            const_spec((2, D_REC)),
            const_spec((1, REC_DV)),
            const_spec((d, d)),
            pl.BlockSpec((2, d, LANES), lambda b, j: (0, 0, 0), pipeline_mode=pl.Buffered(1)),
            const_spec((1, LANES)),
            const_spec((t, t)),
        ],
        out_specs=[
            pl.BlockSpec((1, t, d), tile),
            pl.BlockSpec((1, t, D_PACK), tile),
            pl.BlockSpec((1, SUBLANES, t), rtile),
            pl.BlockSpec((1, SUBLANES, t), rtile),
            pl.BlockSpec((N_EXPERTS, LANES), const2),
        ],
        out_shape=[
            jax.ShapeDtypeStruct((bsz, s, d), F32),
            jax.ShapeDtypeStruct((bsz, s, D_PACK), U32),
            jax.ShapeDtypeStruct((bsz * nt, SUBLANES, t), I32),
            jax.ShapeDtypeStruct((bsz * nt, SUBLANES, t), F32),
            jax.ShapeDtypeStruct((N_EXPERTS, LANES), F32),
        ],
        scratch_shapes=[
            pltpu.VMEM((t, d), F32),
            pltpu.VMEM((t, LANES), F32),
            pltpu.VMEM((t, LANES), F32),
            pltpu.VMEM((SUBLANES, d), F32),
            pltpu.VMEM((t, d), BF16),
            pltpu.VMEM((t, D_IN), F32),
            pltpu.VMEM((CONV_HALO + t, D_CONV), F32),
            pltpu.VMEM((t, D_CONV), F32),
            pltpu.VMEM((3, t, D_REC), BF16),
            pltpu.VMEM((t, D_REC), F32),
            pltpu.VMEM((t, D_REC), F32),
            pltpu.VMEM((t, D_REC), BF16),
            pltpu.VMEM((t, D_REC), BF16),
            pltpu.VMEM((t, D_REC), BF16),
            pltpu.VMEM((t, D_REC), BF16),
            pltpu.VMEM((t // CHUNK, 1, D_REC), F32),
            pltpu.VMEM((t, D_REC), F32),
            pltpu.VMEM((t, d), BF16),
            pltpu.VMEM((t, d), F32),
            pltpu.VMEM((REC_HEADS, REC_DV, REC_DK), F32),
            pltpu.VMEM((N_EXPERTS, 1), F32),
        ],
        compiler_params=pltpu.CompilerParams(
            dimension_semantics=("arbitrary", "arbitrary"),
            vmem_limit_bytes=VMEM_LIMIT),
        name="mixer",
    )(x, mod3, g_pre, g_post, g_ffn, w_in, dwk, dwb, lng, lnb, lbl, rng, w_out, w_r, b_r, upper)


def _meta_kernel(cnt_ref, start_ref, bexp_ref, bvalid_ref, nblk_ref):
    shift = EXPERT_ROWS.bit_length() - 1
    n_blocks = bexp_ref.shape[0]

    def fill(e, blk0, cnt):
        def body(jb, c_):
            bexp_ref[jb] = e
            bvalid_ref[jb] = jnp.clip(cnt - ((jb - blk0) << shift), 0, EXPERT_ROWS)
            return c_
        return body

    def per_expert(e, blk0):
        nb = (cnt_ref[e] + (EXPERT_ROWS - 1)) >> shift
        start_ref[e] = blk0 << shift
        lax.fori_loop(blk0, blk0 + nb, fill(e, blk0, cnt_ref[e]), 0)
        return blk0 + nb

    used = lax.fori_loop(0, N_EXPERTS, per_expert, jnp.int32(0))
    lax.fori_loop(used, n_blocks, fill(N_EXPERTS - 1, used, 0), 0)
    nblk_ref[0] = used


def _meta(counts, n_blocks):
    smem = pl.BlockSpec(memory_space=pltpu.SMEM)
    return pl.pallas_call(
        _meta_kernel,
        in_specs=[smem],
        out_specs=[smem, smem, smem, smem],
        out_shape=[
            jax.ShapeDtypeStruct((N_EXPERTS,), I32),
            jax.ShapeDtypeStruct((n_blocks,), I32),
            jax.ShapeDtypeStruct((n_blocks,), I32),
            jax.ShapeDtypeStruct((1,), I32),
        ],
        name="moe_layout",
    )(counts)


def _positions_kernel(start_ref, ri_ref, pos_ref):
    e = ri_ref[:, 0:TOP_K, :]
    seg = jnp.zeros(e.shape, I32)
    for k in range(N_EXPERTS):
        seg = jnp.where(e == k, start_ref[k], seg)
    pos = seg + ri_ref[:, TOP_K:2 * TOP_K, :]
    pad = jnp.zeros((e.shape[0], SUBLANES - TOP_K, e.shape[2]), I32)
    pos_ref[...] = jnp.concatenate([pos, pad], axis=1)


def _positions(starts, ri):
    n_tiles, _, t = ri.shape
    tb = SUBLANES
    return pl.pallas_call(
        _positions_kernel,
        grid_spec=pltpu.PrefetchScalarGridSpec(
            num_scalar_prefetch=1,
            grid=(n_tiles // tb,),
            in_specs=[pl.BlockSpec((tb, SUBLANES, t), lambda i, s_: (i, 0, 0))],
            out_specs=pl.BlockSpec((tb, SUBLANES, t), lambda i, s_: (i, 0, 0)),
        ),
        out_shape=jax.ShapeDtypeStruct(ri.shape, I32),
        name="moe_positions",
    )(starts, ri)


_SC_MESH = dict(core_axis_name="core", subcore_axis_name="subcore")
SC_WORKERS = 32
SC_INDEX_CHUNK = 128


def _sc_worker_base(rows_per_worker):
    wid = lax.axis_index("core") * (SC_WORKERS // 2) + lax.axis_index("subcore")
    return wid * rows_per_worker


def _sc_dispatch(rows_in, pos_a, pos_b, n_rows):
    n, d = rows_in.shape
    w = SC_WINDOW
    per_worker = n // SC_WORKERS
    chunks = per_worker // SC_INDEX_CHUNK
    windows = SC_INDEX_CHUNK // w
    dma = pltpu.SemaphoreType.DMA

    @pl.kernel(out_type=jax.ShapeDtypeStruct((n_rows, d), rows_in.dtype),
               mesh=plsc.VectorSubcoreMesh(**_SC_MESH),
               scratch_types=[pltpu.VMEM((SC_INDEX_CHUNK,), I32), pltpu.VMEM((SC_INDEX_CHUNK,), I32)]
               + [pltpu.VMEM((w, d), rows_in.dtype)] * windows + [dma] * (3 * windows),
               name="moe_dispatch_sc")
    def run(x_hbm, ia_hbm, ib_hbm, o_hbm, ia_v, ib_v, *rest):
        bufs, sems = rest[:windows], rest[windows:]
        base = _sc_worker_base(per_worker)

        @pl.loop(0, chunks)
        def _(c):
            row0 = base + c * SC_INDEX_CHUNK
            loads = [pltpu.make_async_copy(x_hbm.at[pl.ds(row0 + k * w, w)], bufs[k], sems[3 * k])
                     for k in range(windows)]
            for cp in loads:
                cp.start()
            pltpu.sync_copy(ia_hbm.at[pl.ds(row0, SC_INDEX_CHUNK)], ia_v)
            pltpu.sync_copy(ib_hbm.at[pl.ds(row0, SC_INDEX_CHUNK)], ib_v)
            stores = []
            for k in range(windows):
                loads[k].wait()
                for idx_v, sem in ((ia_v, sems[3 * k + 1]), (ib_v, sems[3 * k + 2])):
                    cp = pltpu.make_async_copy(bufs[k], o_hbm.at[idx_v.at[pl.ds(k * w, w)]], sem)
                    cp.start()
                    stores.append(cp)
            for cp in stores:
                cp.wait()

    return run(rows_in, pos_a, pos_b)


def _sc_gather(table, idx):
    m = idx.shape[0]
    d = table.shape[1]
    w = SC_WINDOW
    per_worker = m // SC_WORKERS
    chunks = per_worker // SC_INDEX_CHUNK
    windows = SC_INDEX_CHUNK // w
    dma = pltpu.SemaphoreType.DMA

    @pl.kernel(out_type=jax.ShapeDtypeStruct((m, d), table.dtype),
               mesh=plsc.VectorSubcoreMesh(**_SC_MESH),
               scratch_types=[pltpu.VMEM((SC_INDEX_CHUNK,), I32)]
               + [pltpu.VMEM((w, d), table.dtype)] * windows + [dma] * (2 * windows),
               name="moe_gather_sc")
    def run(x_hbm, i_hbm, o_hbm, i_v, *rest):
        bufs, sems = rest[:windows], rest[windows:]
        base = _sc_worker_base(per_worker)

        @pl.loop(0, chunks)
        def _(c):
            row0 = base + c * SC_INDEX_CHUNK
            pltpu.sync_copy(i_hbm.at[pl.ds(row0, SC_INDEX_CHUNK)], i_v)
            gathers = [pltpu.make_async_copy(x_hbm.at[i_v.at[pl.ds(k * w, w)]], bufs[k], sems[2 * k])
                       for k in range(windows)]
            for cp in gathers:
                cp.start()
            stores = []
            for k in range(windows):
                gathers[k].wait()
                cp = pltpu.make_async_copy(bufs[k], o_hbm.at[pl.ds(row0 + k * w, w)], sems[2 * k + 1])
                cp.start()
                stores.append(cp)
            for cp in stores:
                cp.wait()

    return run(table, idx)


def _expert_kernel(bexp_ref, bvalid_ref, nblk_ref, x_ref, wg_ref, wu_ref, wd_ref, y_ref):
    jb = pl.program_id(0)

    @pl.when(jb < nblk_ref[0])
    def _():
        live = lax.broadcasted_iota(I32, (EXPERT_ROWS, 1), 0) < bvalid_ref[jb]
        lo, hi = _unpack_pair(jnp.where(live, x_ref[...], jnp.uint32(0)))
        lo = lo.astype(BF16)
        hi = hi.astype(BF16)

        def project(w_ref):
            return (jnp.dot(lo, w_ref[0, 0:D_PACK, :], preferred_element_type=F32)
                    + jnp.dot(hi, w_ref[0, D_PACK:D_MODEL, :], preferred_element_type=F32))

        hb = (_silu(project(wg_ref)) * project(wu_ref)).astype(BF16)
        y = jnp.dot(hb, wd_ref[0], preferred_element_type=F32)
        y_ref[...] = _pack_pair(y[:, 0:D_PACK], y[:, D_PACK:D_MODEL])

    @pl.when(jb >= nblk_ref[0])
    def _():
        y_ref[...] = jnp.zeros(y_ref.shape, U32)


def _experts(bexp, bvalid, nblk, xs, w_gate, w_up, w_down):
    n_rows, dp = xs.shape
    d = 2 * dp
    n_blocks = n_rows // EXPERT_ROWS
    return pl.pallas_call(
        _expert_kernel,
        grid_spec=pltpu.PrefetchScalarGridSpec(
            num_scalar_prefetch=3,
            grid=(n_blocks,),
            in_specs=[
                pl.BlockSpec((EXPERT_ROWS, dp), lambda jb, be, bv, nb: (jnp.minimum(jb, nb[0] - 1), 0)),
                pl.BlockSpec((1, d, D_EXPERT), lambda jb, be, bv, nb: (be[jb], 0, 0)),
                pl.BlockSpec((1, d, D_EXPERT), lambda jb, be, bv, nb: (be[jb], 0, 0)),
                pl.BlockSpec((1, D_EXPERT, d), lambda jb, be, bv, nb: (be[jb], 0, 0)),
            ],
            out_specs=pl.BlockSpec((EXPERT_ROWS, dp), lambda jb, be, bv, nb: (jb, 0)),
        ),
        out_shape=jax.ShapeDtypeStruct((n_rows, dp), U32),
        compiler_params=pltpu.CompilerParams(dimension_semantics=("arbitrary",)),
        name="moe_experts",
    )(bexp, bvalid, nblk, xs, w_gate, w_up, w_down)


def _combine_kernel(rw_ref, x1_ref, gt2_ref, gpost_ref, ya_ref, yb_ref, o_ref):
    wt = rw_ref[0].T
    a_lo, a_hi = _unpack_pair(ya_ref[...])
    b_lo, b_hi = _unpack_pair(yb_ref[...])
    y_lo = a_lo * wt[:, 0:1] + b_lo * wt[:, 1:2]
    y_hi = a_hi * wt[:, 0:1] + b_hi * wt[:, 1:2]
    ssq = jnp.sum(y_lo * y_lo, axis=-1, keepdims=True) + jnp.sum(y_hi * y_hi, axis=-1, keepdims=True)
    inv = lax.rsqrt(ssq * (1.0 / D_MODEL) + EPS)
    gate = gt2_ref[0] * gpost_ref[...]
    o_ref[:, 0:D_PACK] = x1_ref[:, 0:D_PACK] + y_lo * inv * gate[:, 0:D_PACK]
    o_ref[:, D_PACK:D_MODEL] = x1_ref[:, D_PACK:D_MODEL] + y_hi * inv * gate[:, D_PACK:D_MODEL]


def _combine(rw, x1, gt2, g_post, yg, seq):
    n, d = x1.shape
    t = SEQ_TILE
    per_batch = seq // t
    n_tiles = n // t
    return pl.pallas_call(
        _combine_kernel,
        grid=(n_tiles,),
        in_specs=[
            pl.BlockSpec((1, SUBLANES, t), lambda i: (i, 0, 0)),
            pl.BlockSpec((t, d), lambda i: (i, 0)),
            pl.BlockSpec((1, 1, d), lambda i: (i // per_batch, 0, 0)),
            pl.BlockSpec((1, d), lambda i: (0, 0)),
            pl.BlockSpec((t, D_PACK), lambda i: (i, 0)),
            pl.BlockSpec((t, D_PACK), lambda i: (i + n_tiles, 0)),
        ],
        out_specs=pl.BlockSpec((t, d), lambda i: (i, 0)),
        out_shape=jax.ShapeDtypeStruct((n, d), F32),
        compiler_params=pltpu.CompilerParams(dimension_semantics=("arbitrary",)),
        name="moe_combine",
    )(rw, x1, gt2, g_post, yg, yg)


def kernel(x, c, w_ada, b_ada, g_pre_mix, g_post_mix, w_in, dw_kernel, dw_bias, conv_ln_gain, conv_ln_bias, lb_logits, rec_norm_gain, w_out, g_pre_ffn, g_post_ffn, w_router_group, b_router_group, w_router_expert, b_router_expert, w_gate, w_up, w_down):
    bsz, s, d = x.shape
    depth = w_ada.shape[0]
    assert depth == 1 and lb_logits.shape[0] == 2
    n_tok = bsz * s
    n_rows = n_tok * TOP_K + N_EXPERTS * EXPERT_ROWS
    for l in range(depth):
        mod = _ada(c, w_ada[l], b_ada[l])
        mod3 = mod.reshape(bsz, 6, d)
        pad = LANES - N_EXPERTS - N_GROUPS
        w_r = jnp.concatenate([w_router_expert[l], w_router_group[l], jnp.zeros((d, pad), F32)], axis=1)
        w_r_hi = w_r.astype(BF16)
        w_r = jnp.stack([w_r_hi, (w_r - w_r_hi.astype(F32)).astype(BF16)])
        b_r = jnp.concatenate([b_router_expert[l], b_router_group[l], jnp.zeros((pad,), F32)]).reshape(1, LANES)
        x1, h2p, ri, rw, cnt = _mix(
            x, mod3, g_pre_mix[l].reshape(1, d), g_post_mix[l].reshape(1, d), g_pre_ffn[l].reshape(1, d),
            w_in[l].astype(BF16), dw_kernel[l], dw_bias[l].reshape(1, D_CONV),
            conv_ln_gain[l].reshape(1, D_CONV), conv_ln_bias[l].reshape(1, D_CONV),
            lb_logits, rec_norm_gain[l].reshape(1, REC_DV), w_out[l].astype(BF16), w_r, b_r)
        counts = cnt[:, 0].astype(I32)
        starts, bexp, bvalid, nblk = _meta(counts, n_rows // EXPERT_ROWS)
        pos = _positions(starts, ri)
        pos_a = pos[:, 0, :].reshape(n_tok)
        pos_b = pos[:, 1, :].reshape(n_tok)
        xs = _sc_dispatch(h2p.reshape(n_tok, D_PACK), pos_a, pos_b, n_rows)
        ys = _experts(bexp, bvalid, nblk, xs, w_gate[l].astype(BF16), w_up[l].astype(BF16), w_down[l].astype(BF16))
        yg = _sc_gather(ys, jnp.concatenate([pos_a, pos_b]))
        out = _combine(rw, x1.reshape(n_tok, d), mod3[:, 5:6, :], g_post_ffn[l].reshape(1, d), yg, s)
        x = out.reshape(bsz, s, d)
    return x
```

```python
import jax
import jax.numpy as jnp
from jax import lax
from jax.experimental import pallas as pl
from jax.experimental.pallas import tpu as pltpu
from jax.experimental.pallas import tpu_sc as plsc

D_MODEL = 1024
D_CONV = 512
D_REC = 512
CONV_WIDTH = 31
REC_HEADS = 4
REC_DK = 128
REC_DV = 128
CHUNK = 64
D_IN = 2 * D_CONV + 4 * D_REC
N_GROUPS = 4
EXPERTS_PER_GROUP = 8
N_EXPERTS = 32
TOP_K = 2
D_EXPERT = 256
EPS = 1e-6

LANES = 128
SUBLANES = 8
SEQ_TILE = 512
CONV_ROWS = 64
ROW_STEP = 32
STAT_ROWS = 64
CONV_HALO = 32
GROUP_ROW0 = 32
EXPERT_ROWS = 512
EXPERT_SUB = 256
SC_WINDOW = 64
D_PACK = D_MODEL // 2
VMEM_LIMIT = 56 * 1024 * 1024

F32 = jnp.float32
BF16 = jnp.bfloat16
I32 = jnp.int32
U32 = jnp.uint32
HI = lax.Precision.HIGHEST


def _sigmoid(v):
    return 1.0 / (1.0 + jnp.exp(-v))


def _silu(v):
    return v * _sigmoid(v)


def _rms(v, gain):
    return v * lax.rsqrt(jnp.mean(v * v, axis=-1, keepdims=True) + EPS) * gain


def _pack_pair(lo, hi):
    lo_bits = lax.bitcast_convert_type(lo.astype(BF16).astype(F32), U32)
    hi_bits = lax.bitcast_convert_type(hi.astype(BF16).astype(F32), U32)
    return (lo_bits >> 16) | (hi_bits & jnp.uint32(0xFFFF0000))


def _unpack_pair(packed):
    lo = lax.bitcast_convert_type(packed << 16, F32)
    hi = lax.bitcast_convert_type(packed & jnp.uint32(0xFFFF0000), F32)
    return lo, hi


def _ada_kernel(c_ref, w_ref, b_ref, o_ref):
    cond = _silu(c_ref[...])
    o_ref[...] = jnp.dot(cond, w_ref[...], precision=HI, preferred_element_type=F32) + b_ref[...]


def _ada(c, w, b):
    bsz, d = c.shape
    n = w.shape[1]
    return pl.pallas_call(
        _ada_kernel,
        grid=(n // d,),
        in_specs=[
            pl.BlockSpec((bsz, d), lambda j: (0, 0)),
            pl.BlockSpec((d, d), lambda j: (0, j)),
            pl.BlockSpec((1, d), lambda j: (0, j)),
        ],
        out_specs=pl.BlockSpec((bsz, d), lambda j: (0, j)),
        out_shape=jax.ShapeDtypeStruct((bsz, n), F32),
        name="ada_mod",
    )(c, w, b.reshape(1, n))


def _mix_kernel(x_ref, mod_ref, gpre_ref, gpost_ref, gffn_ref, win_ref, dwk_ref, dwb_ref,
                lng_ref, lnb_ref, lbl_ref, rng_ref, wout_ref, wr_ref, br_ref, upper_ref,
                x1_ref, h2p_ref, ri_ref, rw_ref, cnt_ref,
                h2_s, st1, st2, vec_s, hb_s, proj_s, ubuf, cv_s, lfp_s, k_s, b_s, qd_s, ki_s, ke_s, v_s, a_s, o_s, yb, y_s,
                state, carry):
    b = pl.program_id(0)
    j = pl.program_id(1)
    t = SEQ_TILE
    n_chunks = t // CHUNK
    heads = range(REC_HEADS)

    @pl.when(j == 0)
    def _():
        ubuf[0:CONV_HALO, :] = jnp.zeros((CONV_HALO, D_CONV), F32)
        state[...] = jnp.zeros(state.shape, F32)

    @pl.when((j == 0) & (b == 0))
    def _():
        carry[...] = jnp.zeros(carry.shape, F32)

    sh1 = mod_ref[0, 0:1, :]
    sc1 = mod_ref[0, 1:2, :]
    gt1 = mod_ref[0, 2:3, :]
    sh2 = mod_ref[0, 3:4, :]
    sc2 = mod_ref[0, 4:5, :]

    def row_loop(rows_per_step, body):
        def step(ci, c_):
            body(pl.ds(pl.multiple_of(ci * rows_per_step, rows_per_step), rows_per_step))
            return c_
        lax.fori_loop(0, t // rows_per_step, step, 0)

    stat_blocks = [slice(r0, r0 + STAT_ROWS) for r0 in range(0, t, STAT_ROWS)]

    def head_cols(base, hd):
        return slice(base + hd * REC_DK, base + (hd + 1) * REC_DK)

    def lane_tiles(width):
        return [slice(l0, l0 + LANES) for l0 in range(0, width, LANES)]

    def put_stat(ref, rows, stat):
        ref[rows, :] = jnp.broadcast_to(stat, (STAT_ROWS, LANES))

    vec_s[0:1, :] = gpre_ref[...] * (1.0 + sc1)
    vec_s[1:2, :] = sh1
    vec_s[2:3, :] = gt1 * gpost_ref[...]
    vec_s[3:4, :] = gffn_ref[...] * (1.0 + sc2)
    vec_s[4:5, :] = sh2

    for rows in stat_blocks:
        xc = x_ref[0, rows, :]
        put_stat(st1, rows, lax.rsqrt(jnp.mean(xc * xc, axis=-1, keepdims=True) + EPS))

    def prenorm(rows):
        inv = st1[rows, :]
        for lanes in lane_tiles(D_MODEL):
            hb_s[rows, lanes] = (x_ref[0, rows, lanes] * inv * vec_s[0:1, lanes] + vec_s[1:2, lanes]).astype(BF16)

    row_loop(ROW_STEP, prenorm)
    proj_s[...] = jnp.dot(hb_s[...], win_ref[...], preferred_element_type=F32)

    def glu(rows):
        cv = proj_s[rows, 0:D_CONV]
        cg = proj_s[rows, D_CONV:2 * D_CONV]
        ubuf[pl.ds(pl.multiple_of(rows.start + CONV_HALO, ROW_STEP), ROW_STEP), :] = cv * _sigmoid(cg)

    row_loop(ROW_STEP, glu)

    lead = CONV_HALO - (CONV_WIDTH - 1)
    win_rows = CONV_ROWS + CONV_HALO

    def conv(rows):
        for lt in range(D_CONV // LANES):
            lanes = slice(lt * LANES, (lt + 1) * LANES)
            win = ubuf[pl.ds(rows.start, win_rows), lanes]
            acc = jnp.broadcast_to(dwb_ref[:, lanes], (CONV_ROWS, LANES))
            for res in range(SUBLANES):
                shifted = win if res == 0 else pltpu.roll(win, win_rows - res, axis=0)
                for al in range(0, win_rows - CONV_ROWS + 1, SUBLANES):
                    kk = al + res - lead
                    if 0 <= kk < CONV_WIDTH and al + CONV_ROWS + res <= win_rows:
                        acc = acc + shifted[al:al + CONV_ROWS] * dwk_ref[kk:kk + 1, lanes]
            cv_s[rows, lanes] = acc

    row_loop(CONV_ROWS, conv)
    ubuf[0:CONV_HALO, :] = ubuf[t:t + CONV_HALO, :]

    for rows in stat_blocks:
        acc = cv_s[rows, :]
        mu = jnp.mean(acc, axis=-1, keepdims=True)
        xc = acc - mu
        put_stat(st1, rows, mu)
        put_stat(st2, rows, lax.rsqrt(jnp.mean(xc * xc, axis=-1, keepdims=True) + EPS))

    def conv_norm(rows):
        mu = st1[rows, :]
        inv = st2[rows, :]
        for lanes in lane_tiles(D_CONV):
            yc = (cv_s[rows, lanes] - mu) * inv * lng_ref[:, lanes] + lnb_ref[:, lanes]
            yb[rows, lanes] = _silu(yc).astype(BF16)

    row_loop(ROW_STEP, conv_norm)

    q0 = 2 * D_CONV
    f0 = q0 + D_REC
    i0 = f0 + D_REC
    g0 = i0 + D_REC
    l0 = lbl_ref[0:1, :]
    lmax = jnp.max(lbl_ref[...], axis=0, keepdims=True)
    lb = jnp.exp(l0 - lmax) / jnp.sum(jnp.exp(lbl_ref[...] - lmax), axis=0, keepdims=True)

    def gates(rows):
        for hd in heads:
            cols = head_cols(0, hd)
            forget = lb[:, cols] + (1.0 - lb[:, cols]) * _sigmoid(proj_s[rows, head_cols(f0, hd)])
            k_s[rows, cols] = 1.0 - forget
            lf = jnp.log(forget)
            hi = lf.astype(BF16)
            rem = lf - hi.astype(F32)
            mid = rem.astype(BF16)
            lfp_s[0, rows, cols] = hi
            lfp_s[1, rows, cols] = mid
            lfp_s[2, rows, cols] = (rem - mid.astype(F32)).astype(BF16)

    row_loop(CHUNK, gates)

    row = lax.broadcasted_iota(I32, (CHUNK, CHUNK), 0)
    col = lax.broadcasted_iota(I32, (CHUNK, CHUNK), 1)
    causal = row >= col
    tri = jnp.where(causal, 1.0, 0.0).astype(BF16)
    for ci in range(n_chunks):
        rows = slice(ci * CHUNK, (ci + 1) * CHUNK)
        b_s[rows, :] = (jnp.dot(tri, lfp_s[0, rows, :], preferred_element_type=F32)
                        + jnp.dot(tri, lfp_s[1, rows, :], preferred_element_type=F32)
                        + jnp.dot(tri, lfp_s[2, rows, :], preferred_element_type=F32))

    def decays(ci, c_):
        rows = pl.ds(pl.multiple_of(ci * CHUNK, CHUNK), CHUNK)
        for hd in heads:
            cols = head_cols(0, hd)
            bcum = b_s[rows, cols]
            k_c = k_s[rows, cols]
            a_last = jnp.exp(bcum[CHUNK - 1:CHUNK, :])
            k_inv = k_c * jnp.exp(-bcum)
            qd_s[rows, cols] = (_silu(proj_s[rows, head_cols(q0, hd)]) * jnp.exp(bcum)).astype(BF16)
            ki_s[rows, cols] = k_inv.astype(BF16)
            ke_s[rows, cols] = (k_inv * a_last).astype(BF16)
            v_s[rows, cols] = proj_s[rows, head_cols(i0, hd)].astype(BF16)
            a_s[ci, :, cols] = a_last
        return c_

    lax.fori_loop(0, n_chunks, decays, 0)

    nt_dims = (((1,), (1,)), ((), ()))
    pairs = [(ci, hd) for ci in range(n_chunks) for hd in heads]

    def blk(ci, hd):
        return slice(ci * CHUNK, (ci + 1) * CHUNK), head_cols(0, hd)

    scores = {}
    for p in pairs:
        rows, cols = blk(*p)
        sc = lax.dot_general(qd_s[rows, cols], ki_s[rows, cols], nt_dims, preferred_element_type=F32)
        scores[p] = jnp.where(causal, sc, 0.0).astype(BF16)
    for p in pairs:
        rows, cols = blk(*p)
        o_s[rows, cols] = jnp.dot(scores[p], v_s[rows, cols], preferred_element_type=F32)
    upd = {}
    for p in pairs:
        rows, cols = blk(*p)
        v_t = v_s[rows, cols].astype(F32).T.astype(BF16)
        upd[p] = jnp.dot(v_t, ke_s[rows, cols], preferred_element_type=F32)
    prev = {}
    for hd in heads:
        st = state[hd]
        for ci in range(n_chunks):
            prev[(ci, hd)] = st.astype(BF16)
            st = st * a_s[ci, :, head_cols(0, hd)] + upd[(ci, hd)]
        state[hd] = st
    for p in pairs:
        rows, cols = blk(*p)
        o_s[rows, cols] += lax.dot_general(qd_s[rows, cols], prev[p], nt_dims, preferred_element_type=F32)

    for rows in stat_blocks:
        for hd in heads:
            o = _rms(o_s[rows, head_cols(0, hd)], rng_ref[...]) * _silu(proj_s[rows, head_cols(g0, hd)])
            yb[rows, head_cols(D_CONV, hd)] = o.astype(BF16)

    y_s[...] = jnp.dot(yb[...], wout_ref[...], preferred_element_type=F32)
    for rows in stat_blocks:
        yc = y_s[rows, :]
        put_stat(st1, rows, lax.rsqrt(jnp.mean(yc * yc, axis=-1, keepdims=True) + EPS))

    def residual(rows):
        inv = st1[rows, :]
        for lanes in lane_tiles(D_MODEL):
            x1_ref[0, rows, lanes] = x_ref[0, rows, lanes] + y_s[rows, lanes] * inv * vec_s[2:3, lanes]

    row_loop(ROW_STEP, residual)
    for rows in stat_blocks:
        xc = x1_ref[0, rows, :]
        put_stat(st2, rows, lax.rsqrt(jnp.mean(xc * xc, axis=-1, keepdims=True) + EPS))

    def ffn_norm(rows):
        inv = st2[rows, :]
        for lanes in lane_tiles(D_PACK):
            upper = slice(lanes.start + D_PACK, lanes.stop + D_PACK)
            lo = x1_ref[0, rows, lanes] * inv * vec_s[3:4, lanes] + vec_s[4:5, lanes]
            hi = x1_ref[0, rows, upper] * inv * vec_s[3:4, upper] + vec_s[4:5, upper]
            h2_s[rows, lanes] = lo
            h2_s[rows, upper] = hi
            h2p_ref[0, rows, lanes] = _pack_pair(lo, hi)

    row_loop(ROW_STEP, ffn_norm)

    h2v = h2_s[...]
    h_hi = h2v.astype(BF16)
    h_lo = (h2v - h_hi.astype(F32)).astype(BF16)
    both = jnp.dot(h_hi, wr_ref[...], preferred_element_type=F32)
    logits = (both[:, 0:LANES] + both[:, LANES:2 * LANES]
              + jnp.dot(h_lo, wr_ref[:, 0:LANES], preferred_element_type=F32)) + br_ref[...]
    lt = logits.T
    neg = jnp.float32(-jnp.inf)
    r8 = lax.broadcasted_iota(I32, (SUBLANES, t), 0)
    gl = jnp.where(r8 < N_GROUPS, lt[GROUP_ROW0:GROUP_ROW0 + SUBLANES], neg)
    gmax = jnp.max(gl, axis=0, keepdims=True)
    gidx = jnp.min(jnp.where(gl == gmax, r8, SUBLANES), axis=0, keepdims=True)
    gprob = 1.0 / jnp.sum(jnp.exp(gl - gmax), axis=0, keepdims=True)
    re = lax.broadcasted_iota(I32, (N_EXPERTS, t), 0)
    el = jnp.where((re // EXPERTS_PER_GROUP) == gidx, lt[0:N_EXPERTS], neg)
    m1 = jnp.max(el, axis=0, keepdims=True)
    i1 = jnp.min(jnp.where(el == m1, re, N_EXPERTS), axis=0, keepdims=True)
    el2 = jnp.where(re == i1, neg, el)
    m2 = jnp.max(el2, axis=0, keepdims=True)
    i2 = jnp.min(jnp.where(el2 == m2, re, N_EXPERTS), axis=0, keepdims=True)
    r = jnp.exp(m2 - m1)
    w1 = gprob / (1.0 + r)
    w2 = gprob * r / (1.0 + r)
    hot1 = re == i1
    hot2 = re == i2
    hot = jnp.where(hot1 | hot2, 1.0, 0.0)
    prefix = jnp.dot(hot.astype(BF16), upper_ref[...], preferred_element_type=F32) + carry[...]
    rank1 = jnp.sum(jnp.where(hot1, prefix, 0.0), axis=0, keepdims=True)
    rank2 = jnp.sum(jnp.where(hot2, prefix, 0.0), axis=0, keepdims=True)
    carry[...] = carry[...] + jnp.sum(hot, axis=1, keepdims=True)
    zi = jnp.zeros((SUBLANES - 4, t), I32)
    ri_ref[0] = jnp.concatenate([i1, i2, rank1.astype(I32), rank2.astype(I32), zi], axis=0)
    rw_ref[0] = jnp.concatenate([w1, w2, jnp.zeros((SUBLANES - 2, t), F32)], axis=0)
    cnt_ref[...] = jnp.broadcast_to(carry[...], cnt_ref.shape)


def _mix(x, mod3, g_pre, g_post, g_ffn, w_in, dwk, dwb, lng, lnb, lbl, rng, w_out, w_r, b_r):
    bsz, s, d = x.shape
    t = SEQ_TILE
    nt = s // t
    tile = lambda b, j: (b, j, 0)
    rtile = lambda b, j: (b * nt + j, 0, 0)
    const2 = lambda b, j: (0, 0)

    def const_spec(shape):
        return pl.BlockSpec(shape, const2, pipeline_mode=pl.Buffered(1))

    upper = jnp.triu(jnp.ones((t, t), BF16), k=1)
    return pl.pallas_call(
        _mix_kernel,
        grid=(bsz, nt),
        in_specs=[
            pl.BlockSpec((1, t, d), tile),
            pl.BlockSpec((1, 6, d), lambda b, j: (b, 0, 0)),
            const_spec((1, d)),
            const_spec((1, d)),
            const_spec((1, d)),
            const_spec((d, D_IN)),
            const_spec((CONV_WIDTH, D_CONV)),
            const_spec((1, D_CONV)),
            const_spec((1, D_CONV)),
            const_spec((1, D_CONV)),
            const_spec((2, D_REC)),
            const_spec((1, REC_DV)),
            const_spec((d, d)),
            const_spec((d, 2 * LANES)),
            const_spec((1, LANES)),
            const_spec((t, t)),
        ],
        out_specs=[
            pl.BlockSpec((1, t, d), tile),
            pl.BlockSpec((1, t, D_PACK), tile),
            pl.BlockSpec((1, SUBLANES, t), rtile),
            pl.BlockSpec((1, SUBLANES, t), rtile),
            pl.BlockSpec((N_EXPERTS, LANES), const2),
        ],
        out_shape=[
            jax.ShapeDtypeStruct((bsz, s, d), F32),
            jax.ShapeDtypeStruct((bsz, s, D_PACK), U32),
            jax.ShapeDtypeStruct((bsz * nt, SUBLANES, t), I32),
            jax.ShapeDtypeStruct((bsz * nt, SUBLANES, t), F32),
            jax.ShapeDtypeStruct((N_EXPERTS, LANES), F32),
        ],
        scratch_shapes=[
            pltpu.VMEM((t, d), F32),
            pltpu.VMEM((t, LANES), F32),
            pltpu.VMEM((t, LANES), F32),
            pltpu.VMEM((SUBLANES, d), F32),
            pltpu.VMEM((t, d), BF16),
            pltpu.VMEM((t, D_IN), F32),
            pltpu.VMEM((CONV_HALO + t, D_CONV), F32),
            pltpu.VMEM((t, D_CONV), F32),
            pltpu.VMEM((3, t, D_REC), BF16),
            pltpu.VMEM((t, D_REC), F32),
            pltpu.VMEM((t, D_REC), F32),
            pltpu.VMEM((t, D_REC), BF16),
            pltpu.VMEM((t, D_REC), BF16),
            pltpu.VMEM((t, D_REC), BF16),
            pltpu.VMEM((t, D_REC), BF16),
            pltpu.VMEM((t // CHUNK, 1, D_REC), F32),
            pltpu.VMEM((t, D_REC), F32),
            pltpu.VMEM((t, d), BF16),
            pltpu.VMEM((t, d), F32),
            pltpu.VMEM((REC_HEADS, REC_DV, REC_DK), F32),
            pltpu.VMEM((N_EXPERTS, 1), F32),
        ],
        compiler_params=pltpu.CompilerParams(
            dimension_semantics=("arbitrary", "arbitrary"),
            vmem_limit_bytes=VMEM_LIMIT),
        name="mixer",
    )(x, mod3, g_pre, g_post, g_ffn, w_in, dwk, dwb, lng, lnb, lbl, rng, w_out, w_r, b_r, upper)


def _meta_kernel(cnt_ref, start_ref, bexp_ref, bvalid_ref, nblk_ref):
    shift = EXPERT_ROWS.bit_length() - 1
    n_blocks = bexp_ref.shape[0]

    def fill(e, blk0, cnt):
        def body(jb, c_):
            bexp_ref[jb] = e
            bvalid_ref[jb] = jnp.clip(cnt - ((jb - blk0) << shift), 0, EXPERT_ROWS)
            return c_
        return body

    def per_expert(e, blk0):
        nb = (cnt_ref[e] + (EXPERT_ROWS - 1)) >> shift
        start_ref[e] = blk0 << shift
        lax.fori_loop(blk0, blk0 + nb, fill(e, blk0, cnt_ref[e]), 0)
        return blk0 + nb

    used = lax.fori_loop(0, N_EXPERTS, per_expert, jnp.int32(0))
    lax.fori_loop(used, n_blocks, fill(N_EXPERTS - 1, used, 0), 0)
    nblk_ref[0] = used


def _meta(counts, n_blocks):
    smem = pl.BlockSpec(memory_space=pltpu.SMEM)
    return pl.pallas_call(
        _meta_kernel,
        in_specs=[smem],
        out_specs=[smem, smem, smem, smem],
        out_shape=[
            jax.ShapeDtypeStruct((N_EXPERTS,), I32),
            jax.ShapeDtypeStruct((n_blocks,), I32),
            jax.ShapeDtypeStruct((n_blocks,), I32),
            jax.ShapeDtypeStruct((1,), I32),
        ],
        name="moe_layout",
    )(counts)


def _positions_kernel(start_ref, ri_ref, pos_ref):
    e = ri_ref[:, 0:TOP_K, :]
    seg = jnp.zeros(e.shape, I32)
    for k in range(N_EXPERTS):
        seg = jnp.where(e == k, start_ref[k], seg)
    pos = seg + ri_ref[:, TOP_K:2 * TOP_K, :]
    pad = jnp.zeros((e.shape[0], SUBLANES - TOP_K, e.shape[2]), I32)
    pos_ref[...] = jnp.concatenate([pos, pad], axis=1)


def _positions(starts, ri):
    n_tiles, _, t = ri.shape
    tb = SUBLANES
    return pl.pallas_call(
        _positions_kernel,
        grid_spec=pltpu.PrefetchScalarGridSpec(
            num_scalar_prefetch=1,
            grid=(n_tiles // tb,),
            in_specs=[pl.BlockSpec((tb, SUBLANES, t), lambda i, s_: (i, 0, 0))],
            out_specs=pl.BlockSpec((tb, SUBLANES, t), lambda i, s_: (i, 0, 0)),
        ),
        out_shape=jax.ShapeDtypeStruct(ri.shape, I32),
        name="moe_positions",
    )(starts, ri)


_SC_MESH = dict(core_axis_name="core", subcore_axis_name="subcore")
SC_WORKERS = 32
SC_INDEX_CHUNK = 128


def _sc_worker_base(rows_per_worker):
    wid = lax.axis_index("core") * (SC_WORKERS // 2) + lax.axis_index("subcore")
    return wid * rows_per_worker


def _sc_dispatch(rows_in, pos_a, pos_b, n_rows):
    n, d = rows_in.shape
    w = SC_WINDOW
    per_worker = n // SC_WORKERS
    chunks = per_worker // SC_INDEX_CHUNK
    windows = SC_INDEX_CHUNK // w
    dma = pltpu.SemaphoreType.DMA

    @pl.kernel(out_type=jax.ShapeDtypeStruct((n_rows, d), rows_in.dtype),
               mesh=plsc.VectorSubcoreMesh(**_SC_MESH),
               scratch_types=[pltpu.VMEM((SC_INDEX_CHUNK,), I32), pltpu.VMEM((SC_INDEX_CHUNK,), I32)]
               + [pltpu.VMEM((w, d), rows_in.dtype)] * windows + [dma] * (3 * windows),
               name="moe_dispatch_sc")
    def run(x_hbm, ia_hbm, ib_hbm, o_hbm, ia_v, ib_v, *rest):
        bufs, sems = rest[:windows], rest[windows:]
        base = _sc_worker_base(per_worker)

        @pl.loop(0, chunks)
        def _(c):
            row0 = base + c * SC_INDEX_CHUNK
            loads = [pltpu.make_async_copy(x_hbm.at[pl.ds(row0 + k * w, w)], bufs[k], sems[3 * k])
                     for k in range(windows)]
            for cp in loads:
                cp.start()
            pltpu.sync_copy(ia_hbm.at[pl.ds(row0, SC_INDEX_CHUNK)], ia_v)
            pltpu.sync_copy(ib_hbm.at[pl.ds(row0, SC_INDEX_CHUNK)], ib_v)
            stores = []
            for k in range(windows):
                loads[k].wait()
                for idx_v, sem in ((ia_v, sems[3 * k + 1]), (ib_v, sems[3 * k + 2])):
                    cp = pltpu.make_async_copy(bufs[k], o_hbm.at[idx_v.at[pl.ds(k * w, w)]], sem)
                    cp.start()
                    stores.append(cp)
            for cp in stores:
                cp.wait()

    return run(rows_in, pos_a, pos_b)


def _sc_gather(table, idx):
    m = idx.shape[0]
    d = table.shape[1]
    w = SC_WINDOW
    per_worker = m // SC_WORKERS
    chunks = per_worker // SC_INDEX_CHUNK
    windows = SC_INDEX_CHUNK // w
    dma = pltpu.SemaphoreType.DMA

    @pl.kernel(out_type=jax.ShapeDtypeStruct((m, d), table.dtype),
               mesh=plsc.VectorSubcoreMesh(**_SC_MESH),
               scratch_types=[pltpu.VMEM((SC_INDEX_CHUNK,), I32)]
               + [pltpu.VMEM((w, d), table.dtype)] * windows + [dma] * (2 * windows),
               name="moe_gather_sc")
    def run(x_hbm, i_hbm, o_hbm, i_v, *rest):
        bufs, sems = rest[:windows], rest[windows:]
        base = _sc_worker_base(per_worker)

        @pl.loop(0, chunks)
        def _(c):
            row0 = base + c * SC_INDEX_CHUNK
            pltpu.sync_copy(i_hbm.at[pl.ds(row0, SC_INDEX_CHUNK)], i_v)
            gathers = [pltpu.make_async_copy(x_hbm.at[i_v.at[pl.ds(k * w, w)]], bufs[k], sems[2 * k])
                       for k in range(windows)]
            for cp in gathers:
                cp.start()
            stores = []
            for k in range(windows):
                gathers[k].wait()
                cp = pltpu.make_async_copy(bufs[k], o_hbm.at[pl.ds(row0 + k * w, w)], sems[2 * k + 1])
                cp.start()
                stores.append(cp)
            for cp in stores:
                cp.wait()

    return run(table, idx)


def _expert_kernel(bexp_ref, bvalid_ref, nblk_ref, x_ref, wg_ref, wu_ref, wd_ref, y_ref, wg_s, wu_s, wd_s):
    jb = pl.program_id(0)
    valid = bvalid_ref[jb]

    @pl.when((jb == 0) | (bexp_ref[jb] != bexp_ref[jnp.maximum(jb - 1, 0)]))
    def _():
        wg_s[...] = wg_ref[0].astype(BF16)
        wu_s[...] = wu_ref[0].astype(BF16)
        wd_s[...] = wd_ref[0].astype(BF16)

    for r0 in range(0, EXPERT_ROWS, EXPERT_SUB):
        rows = slice(r0, r0 + EXPERT_SUB)

        @pl.when(valid > r0)
        def _():
            live = lax.broadcasted_iota(I32, (EXPERT_SUB, 1), 0) + r0 < valid
            lo, hi = _unpack_pair(jnp.where(live, x_ref[rows, :], jnp.uint32(0)))
            lo = lo.astype(BF16)
            hi = hi.astype(BF16)

            def project(w_s):
                return (jnp.dot(lo, w_s[0:D_PACK, :], preferred_element_type=F32)
                        + jnp.dot(hi, w_s[D_PACK:D_MODEL, :], preferred_element_type=F32))

            hb = (_silu(project(wg_s)) * project(wu_s)).astype(BF16)
            y = jnp.dot(hb, wd_s[...], preferred_element_type=F32)
            y_ref[rows, :] = _pack_pair(y[:, 0:D_PACK], y[:, D_PACK:D_MODEL])

        @pl.when(valid <= r0)
        def _():
            y_ref[rows, :] = jnp.zeros((EXPERT_SUB, D_PACK), U32)


def _experts(bexp, bvalid, nblk, xs, w_gate, w_up, w_down):
    n_rows, dp = xs.shape
    d = 2 * dp
    n_blocks = n_rows // EXPERT_ROWS
    return pl.pallas_call(
        _expert_kernel,
        grid_spec=pltpu.PrefetchScalarGridSpec(
            num_scalar_prefetch=3,
            grid=(n_blocks,),
            in_specs=[
                pl.BlockSpec((EXPERT_ROWS, dp), lambda jb, be, bv, nb: (jnp.minimum(jb, nb[0] - 1), 0)),
                pl.BlockSpec((1, d, D_EXPERT), lambda jb, be, bv, nb: (be[jb], 0, 0)),
                pl.BlockSpec((1, d, D_EXPERT), lambda jb, be, bv, nb: (be[jb], 0, 0)),
                pl.BlockSpec((1, D_EXPERT, d), lambda jb, be, bv, nb: (be[jb], 0, 0)),
            ],
            out_specs=pl.BlockSpec((EXPERT_ROWS, dp), lambda jb, be, bv, nb: (jb, 0)),
            scratch_shapes=[
                pltpu.VMEM((d, D_EXPERT), BF16),
                pltpu.VMEM((d, D_EXPERT), BF16),
                pltpu.VMEM((D_EXPERT, d), BF16),
            ],
        ),
        out_shape=jax.ShapeDtypeStruct((n_rows, dp), U32),
        compiler_params=pltpu.CompilerParams(dimension_semantics=("arbitrary",)),
        name="moe_experts",
    )(bexp, bvalid, nblk, xs, w_gate, w_up, w_down)


def _combine_kernel(rw_ref, x1_ref, gt2_ref, gpost_ref, ya_ref, yb_ref, o_ref):
    wt = rw_ref[0].T
    a_lo, a_hi = _unpack_pair(ya_ref[...])
    b_lo, b_hi = _unpack_pair(yb_ref[...])
    y_lo = a_lo * wt[:, 0:1] + b_lo * wt[:, 1:2]
    y_hi = a_hi * wt[:, 0:1] + b_hi * wt[:, 1:2]
    ssq = jnp.sum(y_lo * y_lo, axis=-1, keepdims=True) + jnp.sum(y_hi * y_hi, axis=-1, keepdims=True)
    inv = lax.rsqrt(ssq * (1.0 / D_MODEL) + EPS)
    gate = gt2_ref[0] * gpost_ref[...]
    o_ref[:, 0:D_PACK] = x1_ref[:, 0:D_PACK] + y_lo * inv * gate[:, 0:D_PACK]
    o_ref[:, D_PACK:D_MODEL] = x1_ref[:, D_PACK:D_MODEL] + y_hi * inv * gate[:, D_PACK:D_MODEL]


def _combine(rw, x1, gt2, g_post, yg, seq):
    n, d = x1.shape
    t = SEQ_TILE
    per_batch = seq // t
    n_tiles = n // t
    return pl.pallas_call(
        _combine_kernel,
        grid=(n_tiles,),
        in_specs=[
            pl.BlockSpec((1, SUBLANES, t), lambda i: (i, 0, 0)),
            pl.BlockSpec((t, d), lambda i: (i, 0)),
            pl.BlockSpec((1, 1, d), lambda i: (i // per_batch, 0, 0)),
            pl.BlockSpec((1, d), lambda i: (0, 0)),
            pl.BlockSpec((t, D_PACK), lambda i: (i, 0)),
            pl.BlockSpec((t, D_PACK), lambda i: (i + n_tiles, 0)),
        ],
        out_specs=pl.BlockSpec((t, d), lambda i: (i, 0)),
        out_shape=jax.ShapeDtypeStruct((n, d), F32),
        compiler_params=pltpu.CompilerParams(dimension_semantics=("arbitrary",)),
        name="moe_combine",
    )(rw, x1, gt2, g_post, yg, yg)


def kernel(x, c, w_ada, b_ada, g_pre_mix, g_post_mix, w_in, dw_kernel, dw_bias, conv_ln_gain, conv_ln_bias, lb_logits, rec_norm_gain, w_out, g_pre_ffn, g_post_ffn, w_router_group, b_router_group, w_router_expert, b_router_expert, w_gate, w_up, w_down):
    bsz, s, d = x.shape
    depth = w_ada.shape[0]
    assert depth == 1 and lb_logits.shape[0] == 2
    n_tok = bsz * s
    n_rows = n_tok * TOP_K + N_EXPERTS * EXPERT_ROWS
    for l in range(depth):
        mod = _ada(c, w_ada[l], b_ada[l])
        mod3 = mod.reshape(bsz, 6, d)
        pad = LANES - N_EXPERTS - N_GROUPS
        w_r = jnp.concatenate([w_router_expert[l], w_router_group[l], jnp.zeros((d, pad), F32)], axis=1)
        w_r_hi = w_r.astype(BF16)
        w_r = jnp.concatenate([w_r_hi, (w_r - w_r_hi.astype(F32)).astype(BF16)], axis=1)
        b_r = jnp.concatenate([b_router_expert[l], b_router_group[l], jnp.zeros((pad,), F32)]).reshape(1, LANES)
        x1, h2p, ri, rw, cnt = _mix(
            x, mod3, g_pre_mix[l].reshape(1, d), g_post_mix[l].reshape(1, d), g_pre_ffn[l].reshape(1, d),
            w_in[l].astype(BF16), dw_kernel[l], dw_bias[l].reshape(1, D_CONV),
            conv_ln_gain[l].reshape(1, D_CONV), conv_ln_bias[l].reshape(1, D_CONV),
            lb_logits, rec_norm_gain[l].reshape(1, REC_DV), w_out[l].astype(BF16), w_r, b_r)
        counts = cnt[:, 0].astype(I32)
        starts, bexp, bvalid, nblk = _meta(counts, n_rows // EXPERT_ROWS)
        pos = _positions(starts, ri)
        pos_a = pos[:, 0, :].reshape(n_tok)
        pos_b = pos[:, 1, :].reshape(n_tok)
        xs = _sc_dispatch(h2p.reshape(n_tok, D_PACK), pos_a, pos_b, n_rows)
        ys = _experts(bexp, bvalid, nblk, xs, w_gate[l], w_up[l], w_down[l])
        yg = _sc_gather(ys, jnp.concatenate([pos_a, pos_b]))
        out = _combine(rw, x1.reshape(n_tok, d), mod3[:, 5:6, :], g_post_ffn[l].reshape(1, d), yg, s)
        x = out.reshape(bsz, s, d)
    return x
```

```python
import jax
import jax.numpy as jnp
from jax import lax
from jax.experimental import pallas as pl
from jax.experimental.pallas import tpu as pltpu
from jax.experimental.pallas import tpu_sc as plsc

D_MODEL = 1024
D_CONV = 512
D_REC = 512
CONV_WIDTH = 31
REC_HEADS = 4
REC_DK = 128
REC_DV = 128
CHUNK = 64
D_IN = 2 * D_CONV + 4 * D_REC
N_GROUPS = 4
EXPERTS_PER_GROUP = 8
N_EXPERTS = 32
TOP_K = 2
D_EXPERT = 256
EPS = 1e-6

LANES = 128
SUBLANES = 8
SEQ_TILE = 512
CONV_ROWS = 64
ROW_STEP = 32
STAT_ROWS = 64
CONV_HALO = 32
GROUP_ROW0 = 32
EXPERT_ROWS = 512
COMBINE_PIECES = 4
EXPERT_SUB = 256
SC_WINDOW = 64
D_PACK = D_MODEL // 2
VMEM_LIMIT = 56 * 1024 * 1024

F32 = jnp.float32
BF16 = jnp.bfloat16
I32 = jnp.int32
U32 = jnp.uint32
HI = lax.Precision.HIGHEST


def _sigmoid(v):
    return 0.5 * jnp.tanh(0.5 * v) + 0.5


def _silu(v):
    return v * _sigmoid(v)


def _rms(v, gain):
    return v * lax.rsqrt(jnp.mean(v * v, axis=-1, keepdims=True) + EPS) * gain


def _pack_pair(lo, hi):
    lo_bits = lax.bitcast_convert_type(lo.astype(BF16).astype(F32), U32)
    hi_bits = lax.bitcast_convert_type(hi.astype(BF16).astype(F32), U32)
    return (lo_bits >> 16) | (hi_bits & jnp.uint32(0xFFFF0000))


def _unpack_pair(packed):
    lo = lax.bitcast_convert_type(packed << 16, F32)
    hi = lax.bitcast_convert_type(packed & jnp.uint32(0xFFFF0000), F32)
    return lo, hi


def _ada_kernel(c_ref, w_ref, b_ref, o_ref):
    cond = _silu(c_ref[...])
    o_ref[...] = jnp.dot(cond, w_ref[...], precision=HI, preferred_element_type=F32) + b_ref[...]


def _ada(c, w, b):
    bsz, d = c.shape
    n = w.shape[1]
    return pl.pallas_call(
        _ada_kernel,
        grid=(n // d,),
        in_specs=[
            pl.BlockSpec((bsz, d), lambda j: (0, 0)),
            pl.BlockSpec((d, d), lambda j: (0, j)),
            pl.BlockSpec((1, d), lambda j: (0, j)),
        ],
        out_specs=pl.BlockSpec((bsz, d), lambda j: (0, j)),
        out_shape=jax.ShapeDtypeStruct((bsz, n), F32),
        name="ada_mod",
    )(c, w, b.reshape(1, n))


def _mix_kernel(x_ref, mod_ref, gpre_ref, gpost_ref, gffn_ref, win_ref, dwk_ref, dwb_ref,
                lng_ref, lnb_ref, lbl_ref, rng_ref, wout_ref, wr_ref, br_ref, upper_ref,
                x1_ref, h2p_ref, ri_ref, rw_ref, cnt_ref,
                h2_s, st1, st2, vec_s, hb_s, proj_s, ubuf, cv_s, lfp_s, k_s, b_s, qd_s, ki_s, ke_s, v_s, a_s, o_s, yb, y_s,
                state, carry):
    b = pl.program_id(0)
    j = pl.program_id(1)
    t = SEQ_TILE
    n_chunks = t // CHUNK
    heads = range(REC_HEADS)

    @pl.when(j == 0)
    def _():
        ubuf[0:CONV_HALO, :] = jnp.zeros((CONV_HALO, D_CONV), F32)
        state[...] = jnp.zeros(state.shape, F32)

    @pl.when((j == 0) & (b == 0))
    def _():
        carry[...] = jnp.zeros(carry.shape, F32)

    sh1 = mod_ref[0, 0:1, :]
    sc1 = mod_ref[0, 1:2, :]
    gt1 = mod_ref[0, 2:3, :]
    sh2 = mod_ref[0, 3:4, :]
    sc2 = mod_ref[0, 4:5, :]

    def row_loop(rows_per_step, body):
        def step(ci, c_):
            body(pl.ds(pl.multiple_of(ci * rows_per_step, rows_per_step), rows_per_step))
            return c_
        lax.fori_loop(0, t // rows_per_step, step, 0)

    stat_blocks = [slice(r0, r0 + STAT_ROWS) for r0 in range(0, t, STAT_ROWS)]

    def head_cols(base, hd):
        return slice(base + hd * REC_DK, base + (hd + 1) * REC_DK)

    def lane_tiles(width):
        return [slice(l0, l0 + LANES) for l0 in range(0, width, LANES)]

    def put_stat(ref, rows, stat):
        ref[rows, :] = jnp.broadcast_to(stat, (STAT_ROWS, LANES))

    vec_s[0:1, :] = gpre_ref[...] * (1.0 + sc1)
    vec_s[1:2, :] = sh1
    vec_s[2:3, :] = gt1 * gpost_ref[...]
    vec_s[3:4, :] = gffn_ref[...] * (1.0 + sc2)
    vec_s[4:5, :] = sh2

    for rows in stat_blocks:
        xc = x_ref[0, rows, :]
        put_stat(st1, rows, lax.rsqrt(jnp.mean(xc * xc, axis=-1, keepdims=True) + EPS))

    def prenorm(rows):
        inv = st1[rows, :]
        for lanes in lane_tiles(D_MODEL):
            hb_s[rows, lanes] = (x_ref[0, rows, lanes] * inv * vec_s[0:1, lanes] + vec_s[1:2, lanes]).astype(BF16)

    row_loop(ROW_STEP, prenorm)
    proj_s[...] = jnp.dot(hb_s[...], win_ref[...], preferred_element_type=F32)

    def glu(rows):
        cv = proj_s[rows, 0:D_CONV]
        cg = proj_s[rows, D_CONV:2 * D_CONV]
        ubuf[pl.ds(pl.multiple_of(rows.start + CONV_HALO, ROW_STEP), ROW_STEP), :] = cv * _sigmoid(cg)

    row_loop(ROW_STEP, glu)

    lead = CONV_HALO - (CONV_WIDTH - 1)
    win_rows = CONV_ROWS + CONV_HALO

    def conv(rows):
        for lt in range(D_CONV // LANES):
            lanes = slice(lt * LANES, (lt + 1) * LANES)
            win = ubuf[pl.ds(rows.start, win_rows), lanes]
            acc = jnp.broadcast_to(dwb_ref[:, lanes], (CONV_ROWS, LANES))
            for res in range(SUBLANES):
                shifted = win if res == 0 else pltpu.roll(win, win_rows - res, axis=0)
                for al in range(0, win_rows - CONV_ROWS + 1, SUBLANES):
                    kk = al + res - lead
                    if 0 <= kk < CONV_WIDTH and al + CONV_ROWS + res <= win_rows:
                        acc = acc + shifted[al:al + CONV_ROWS] * dwk_ref[kk:kk + 1, lanes]
            cv_s[rows, lanes] = acc

    row_loop(CONV_ROWS, conv)
    ubuf[0:CONV_HALO, :] = ubuf[t:t + CONV_HALO, :]

    for rows in stat_blocks:
        acc = cv_s[rows, :]
        mu = jnp.mean(acc, axis=-1, keepdims=True)
        xc = acc - mu
        put_stat(st1, rows, mu)
        put_stat(st2, rows, lax.rsqrt(jnp.mean(xc * xc, axis=-1, keepdims=True) + EPS))

    def conv_norm(rows):
        mu = st1[rows, :]
        inv = st2[rows, :]
        for lanes in lane_tiles(D_CONV):
            yc = (cv_s[rows, lanes] - mu) * inv * lng_ref[:, lanes] + lnb_ref[:, lanes]
            yb[rows, lanes] = _silu(yc).astype(BF16)

    row_loop(ROW_STEP, conv_norm)

    q0 = 2 * D_CONV
    f0 = q0 + D_REC
    i0 = f0 + D_REC
    g0 = i0 + D_REC
    l0 = lbl_ref[0:1, :]
    lmax = jnp.max(lbl_ref[...], axis=0, keepdims=True)
    lb = jnp.exp(l0 - lmax) / jnp.sum(jnp.exp(lbl_ref[...] - lmax), axis=0, keepdims=True)

    def gates(rows):
        for hd in heads:
            cols = head_cols(0, hd)
            forget = lb[:, cols] + (1.0 - lb[:, cols]) * _sigmoid(proj_s[rows, head_cols(f0, hd)])
            k_s[rows, cols] = 1.0 - forget
            lf = jnp.log(forget)
            hi = lf.astype(BF16)
            rem = lf - hi.astype(F32)
            mid = rem.astype(BF16)
            lfp_s[0, rows, cols] = hi
            lfp_s[1, rows, cols] = mid
            lfp_s[2, rows, cols] = (rem - mid.astype(F32)).astype(BF16)

    row_loop(CHUNK, gates)

    row = lax.broadcasted_iota(I32, (CHUNK, CHUNK), 0)
    col = lax.broadcasted_iota(I32, (CHUNK, CHUNK), 1)
    causal = row >= col
    tri = jnp.where(causal, 1.0, 0.0).astype(BF16)
    for ci in range(n_chunks):
        rows = slice(ci * CHUNK, (ci + 1) * CHUNK)
        b_s[rows, :] = (jnp.dot(tri, lfp_s[0, rows, :], preferred_element_type=F32)
                        + jnp.dot(tri, lfp_s[1, rows, :], preferred_element_type=F32)
                        + jnp.dot(tri, lfp_s[2, rows, :], preferred_element_type=F32))

    def decays(ci, c_):
        rows = pl.ds(pl.multiple_of(ci * CHUNK, CHUNK), CHUNK)
        for hd in heads:
            cols = head_cols(0, hd)
            bcum = b_s[rows, cols]
            k_c = k_s[rows, cols]
            a_last = jnp.exp(bcum[CHUNK - 1:CHUNK, :])
            k_inv = k_c * jnp.exp(-bcum)
            qd_s[rows, cols] = (_silu(proj_s[rows, head_cols(q0, hd)]) * jnp.exp(bcum)).astype(BF16)
            ki_s[rows, cols] = k_inv.astype(BF16)
            ke_s[rows, cols] = (k_inv * a_last).astype(BF16)
            v_s[rows, cols] = proj_s[rows, head_cols(i0, hd)].astype(BF16)
            a_s[ci, :, cols] = a_last
        return c_

    lax.fori_loop(0, n_chunks, decays, 0)

    nt_dims = (((1,), (1,)), ((), ()))
    pairs = [(ci, hd) for ci in range(n_chunks) for hd in heads]

    def blk(ci, hd):
        return slice(ci * CHUNK, (ci + 1) * CHUNK), head_cols(0, hd)

    scores = {}
    for p in pairs:
        rows, cols = blk(*p)
        sc = lax.dot_general(qd_s[rows, cols], ki_s[rows, cols], nt_dims, preferred_element_type=F32)
        scores[p] = jnp.where(causal, sc, 0.0).astype(BF16)
    for p in pairs:
        rows, cols = blk(*p)
        o_s[rows, cols] = jnp.dot(scores[p], v_s[rows, cols], preferred_element_type=F32)
    upd = {}
    for p in pairs:
        rows, cols = blk(*p)
        v_t = v_s[rows, cols].astype(F32).T.astype(BF16)
        upd[p] = jnp.dot(v_t, ke_s[rows, cols], preferred_element_type=F32)
    prev = {}
    for hd in heads:
        st = state[hd]
        for ci in range(n_chunks):
            prev[(ci, hd)] = st.astype(BF16)
            st = st * a_s[ci, :, head_cols(0, hd)] + upd[(ci, hd)]
        state[hd] = st
    for p in pairs:
        rows, cols = blk(*p)
        o_s[rows, cols] += lax.dot_general(qd_s[rows, cols], prev[p], nt_dims, preferred_element_type=F32)

    for rows in stat_blocks:
        for hd in heads:
            o = _rms(o_s[rows, head_cols(0, hd)], rng_ref[...]) * _silu(proj_s[rows, head_cols(g0, hd)])
            yb[rows, head_cols(D_CONV, hd)] = o.astype(BF16)

    y_s[...] = jnp.dot(yb[...], wout_ref[...], preferred_element_type=F32)
    for rows in stat_blocks:
        yc = y_s[rows, :]
        put_stat(st1, rows, lax.rsqrt(jnp.mean(yc * yc, axis=-1, keepdims=True) + EPS))

    def residual(rows):
        inv = st1[rows, :]
        for lanes in lane_tiles(D_MODEL):
            x1_ref[0, rows, lanes] = x_ref[0, rows, lanes] + y_s[rows, lanes] * inv * vec_s[2:3, lanes]

    row_loop(ROW_STEP, residual)
    for rows in stat_blocks:
        xc = x1_ref[0, rows, :]
        put_stat(st2, rows, lax.rsqrt(jnp.mean(xc * xc, axis=-1, keepdims=True) + EPS))

    def ffn_norm(rows):
        inv = st2[rows, :]
        for lanes in lane_tiles(D_PACK):
            upper = slice(lanes.start + D_PACK, lanes.stop + D_PACK)
            lo = x1_ref[0, rows, lanes] * inv * vec_s[3:4, lanes] + vec_s[4:5, lanes]
            hi = x1_ref[0, rows, upper] * inv * vec_s[3:4, upper] + vec_s[4:5, upper]
            h2_s[rows, lanes] = lo
            h2_s[rows, upper] = hi
            h2p_ref[0, rows, lanes] = _pack_pair(lo, hi)

    row_loop(ROW_STEP, ffn_norm)

    h2v = h2_s[...]
    h_hi = h2v.astype(BF16)
    h_lo = (h2v - h_hi.astype(F32)).astype(BF16)
    both = jnp.dot(h_hi, wr_ref[...], preferred_element_type=F32)
    logits = (both[:, 0:LANES] + both[:, LANES:2 * LANES]
              + jnp.dot(h_lo, wr_ref[:, 0:LANES], preferred_element_type=F32)) + br_ref[...]
    lt = logits.T
    neg = jnp.float32(-jnp.inf)
    r8 = lax.broadcasted_iota(I32, (SUBLANES, t), 0)
    gl = jnp.where(r8 < N_GROUPS, lt[GROUP_ROW0:GROUP_ROW0 + SUBLANES], neg)
    gmax = jnp.max(gl, axis=0, keepdims=True)
    gidx = jnp.min(jnp.where(gl == gmax, r8, SUBLANES), axis=0, keepdims=True)
    gprob = 1.0 / jnp.sum(jnp.exp(gl - gmax), axis=0, keepdims=True)
    re = lax.broadcasted_iota(I32, (N_EXPERTS, t), 0)
    el = jnp.where((re // EXPERTS_PER_GROUP) == gidx, lt[0:N_EXPERTS], neg)
    m1 = jnp.max(el, axis=0, keepdims=True)
    i1 = jnp.min(jnp.where(el == m1, re, N_EXPERTS), axis=0, keepdims=True)
    el2 = jnp.where(re == i1, neg, el)
    m2 = jnp.max(el2, axis=0, keepdims=True)
    i2 = jnp.min(jnp.where(el2 == m2, re, N_EXPERTS), axis=0, keepdims=True)
    r = jnp.exp(m2 - m1)
    w1 = gprob / (1.0 + r)
    w2 = gprob * r / (1.0 + r)
    hot1 = re == i1
    hot2 = re == i2
    hot = jnp.where(hot1 | hot2, 1.0, 0.0)
    prefix = jnp.dot(hot.astype(BF16), upper_ref[...], preferred_element_type=F32) + carry[...]
    rank1 = jnp.sum(jnp.where(hot1, prefix, 0.0), axis=0, keepdims=True)
    rank2 = jnp.sum(jnp.where(hot2, prefix, 0.0), axis=0, keepdims=True)
    carry[...] = carry[...] + jnp.sum(hot, axis=1, keepdims=True)
    zi = jnp.zeros((SUBLANES - 4, t), I32)
    ri_ref[0] = jnp.concatenate([i1, i2, rank1.astype(I32), rank2.astype(I32), zi], axis=0)
    rw_ref[0] = jnp.concatenate([w1, w2, jnp.zeros((SUBLANES - 2, t), F32)], axis=0)
    cnt_ref[...] = jnp.broadcast_to(carry[...], cnt_ref.shape)


def _mix(x, mod3, g_pre, g_post, g_ffn, w_in, dwk, dwb, lng, lnb, lbl, rng, w_out, w_r, b_r):
    bsz, s, d = x.shape
    t = SEQ_TILE
    nt = s // t
    tile = lambda b, j: (b, j, 0)
    rtile = lambda b, j: (b * nt + j, 0, 0)
    const2 = lambda b, j: (0, 0)

    def const_spec(shape):
        return pl.BlockSpec(shape, const2, pipeline_mode=pl.Buffered(1))

    upper = jnp.triu(jnp.ones((t, t), BF16), k=1)
    return pl.pallas_call(
        _mix_kernel,
        grid=(bsz, nt),
        in_specs=[
            pl.BlockSpec((1, t, d), tile),
            pl.BlockSpec((1, 6, d), lambda b, j: (b, 0, 0)),
            const_spec((1, d)),
            const_spec((1, d)),
            const_spec((1, d)),
            const_spec((d, D_IN)),
            const_spec((CONV_WIDTH, D_CONV)),
            const_spec((1, D_CONV)),
            const_spec((1, D_CONV)),
            const_spec((1, D_CONV)),
            const_spec((2, D_REC)),
            const_spec((1, REC_DV)),
            const_spec((d, d)),
            const_spec((d, 2 * LANES)),
            const_spec((1, LANES)),
            const_spec((t, t)),
        ],
        out_specs=[
            pl.BlockSpec((1, t, d), tile),
            pl.BlockSpec((1, t, D_PACK), tile),
            pl.BlockSpec((1, SUBLANES, t), rtile),
            pl.BlockSpec((1, SUBLANES, t), rtile),
            pl.BlockSpec((N_EXPERTS, LANES), const2),
        ],
        out_shape=[
            jax.ShapeDtypeStruct((bsz, s, d), F32),
            jax.ShapeDtypeStruct((bsz, s, D_PACK), U32),
            jax.ShapeDtypeStruct((bsz * nt, SUBLANES, t), I32),
            jax.ShapeDtypeStruct((bsz * nt, SUBLANES, t), F32),
            jax.ShapeDtypeStruct((N_EXPERTS, LANES), F32),
        ],
        scratch_shapes=[
            pltpu.VMEM((t, d), F32),
            pltpu.VMEM((t, LANES), F32),
            pltpu.VMEM((t, LANES), F32),
            pltpu.VMEM((SUBLANES, d), F32),
            pltpu.VMEM((t, d), BF16),
            pltpu.VMEM((t, D_IN), F32),
            pltpu.VMEM((CONV_HALO + t, D_CONV), F32),
            pltpu.VMEM((t, D_CONV), F32),
            pltpu.VMEM((3, t, D_REC), BF16),
            pltpu.VMEM((t, D_REC), F32),
            pltpu.VMEM((t, D_REC), F32),
            pltpu.VMEM((t, D_REC), BF16),
            pltpu.VMEM((t, D_REC), BF16),
            pltpu.VMEM((t, D_REC), BF16),
            pltpu.VMEM((t, D_REC), BF16),
            pltpu.VMEM((t // CHUNK, 1, D_REC), F32),
            pltpu.VMEM((t, D_REC), F32),
            pltpu.VMEM((t, d), BF16),
            pltpu.VMEM((t, d), F32),
            pltpu.VMEM((REC_HEADS, REC_DV, REC_DK), F32),
            pltpu.VMEM((N_EXPERTS, 1), F32),
        ],
        compiler_params=pltpu.CompilerParams(
            dimension_semantics=("arbitrary", "arbitrary"),
            vmem_limit_bytes=VMEM_LIMIT),
        name="mixer",
    )(x, mod3, g_pre, g_post, g_ffn, w_in, dwk, dwb, lng, lnb, lbl, rng, w_out, w_r, b_r, upper)


def _meta_kernel(cnt_ref, start_ref, bexp_ref, bvalid_ref, nblk_ref):
    shift = EXPERT_ROWS.bit_length() - 1
    n_blocks = bexp_ref.shape[0]

    def fill(e, blk0, cnt):
        def body(jb, c_):
            bexp_ref[jb] = e
            bvalid_ref[jb] = jnp.clip(cnt - ((jb - blk0) << shift), 0, EXPERT_ROWS)
            return c_
        return body

    def per_expert(e, blk0):
        nb = (cnt_ref[e] + (EXPERT_ROWS - 1)) >> shift
        start_ref[e] = blk0 << shift
        lax.fori_loop(blk0, blk0 + nb, fill(e, blk0, cnt_ref[e]), 0)
        return blk0 + nb

    used = lax.fori_loop(0, N_EXPERTS, per_expert, jnp.int32(0))
    lax.fori_loop(used, n_blocks, fill(N_EXPERTS - 1, used, 0), 0)
    nblk_ref[0] = used


def _meta(counts, n_blocks):
    smem = pl.BlockSpec(memory_space=pltpu.SMEM)
    return pl.pallas_call(
        _meta_kernel,
        in_specs=[smem],
        out_specs=[smem, smem, smem, smem],
        out_shape=[
            jax.ShapeDtypeStruct((N_EXPERTS,), I32),
            jax.ShapeDtypeStruct((n_blocks,), I32),
            jax.ShapeDtypeStruct((n_blocks,), I32),
            jax.ShapeDtypeStruct((1,), I32),
        ],
        name="moe_layout",
    )(counts)


def _positions_kernel(start_ref, ri_ref, pos_ref):
    e = ri_ref[:, 0:TOP_K, :]
    seg = jnp.zeros(e.shape, I32)
    for k in range(N_EXPERTS):
        seg = jnp.where(e == k, start_ref[k], seg)
    pos = seg + ri_ref[:, TOP_K:2 * TOP_K, :]
    pad = jnp.zeros((e.shape[0], SUBLANES - TOP_K, e.shape[2]), I32)
    pos_ref[...] = jnp.concatenate([pos, pad], axis=1)


def _positions(starts, ri):
    n_tiles, _, t = ri.shape
    tb = SUBLANES
    return pl.pallas_call(
        _positions_kernel,
        grid_spec=pltpu.PrefetchScalarGridSpec(
            num_scalar_prefetch=1,
            grid=(n_tiles // tb,),
            in_specs=[pl.BlockSpec((tb, SUBLANES, t), lambda i, s_: (i, 0, 0))],
            out_specs=pl.BlockSpec((tb, SUBLANES, t), lambda i, s_: (i, 0, 0)),
        ),
        out_shape=jax.ShapeDtypeStruct(ri.shape, I32),
        name="moe_positions",
    )(starts, ri)


_SC_MESH = dict(core_axis_name="core", subcore_axis_name="subcore")
SC_WORKERS = 32
SC_INDEX_CHUNK = 128


def _sc_worker_base(rows_per_worker):
    wid = lax.axis_index("core") * (SC_WORKERS // 2) + lax.axis_index("subcore")
    return wid * rows_per_worker


def _sc_dispatch(rows_in, pos_a, pos_b, n_rows):
    n, d = rows_in.shape
    w = SC_WINDOW
    per_worker = n // SC_WORKERS
    chunks = per_worker // SC_INDEX_CHUNK
    windows = SC_INDEX_CHUNK // w
    dma = pltpu.SemaphoreType.DMA

    @pl.kernel(out_type=jax.ShapeDtypeStruct((n_rows, d), rows_in.dtype),
               mesh=plsc.VectorSubcoreMesh(**_SC_MESH),
               scratch_types=[pltpu.VMEM((SC_INDEX_CHUNK,), I32), pltpu.VMEM((SC_INDEX_CHUNK,), I32)]
               + [pltpu.VMEM((w, d), rows_in.dtype)] * windows + [dma] * (3 * windows),
               name="moe_dispatch_sc")
    def run(x_hbm, ia_hbm, ib_hbm, o_hbm, ia_v, ib_v, *rest):
        bufs, sems = rest[:windows], rest[windows:]
        base = _sc_worker_base(per_worker)

        @pl.loop(0, chunks)
        def _(c):
            row0 = base + c * SC_INDEX_CHUNK
            loads = [pltpu.make_async_copy(x_hbm.at[pl.ds(row0 + k * w, w)], bufs[k], sems[3 * k])
                     for k in range(windows)]
            for cp in loads:
                cp.start()
            pltpu.sync_copy(ia_hbm.at[pl.ds(row0, SC_INDEX_CHUNK)], ia_v)
            pltpu.sync_copy(ib_hbm.at[pl.ds(row0, SC_INDEX_CHUNK)], ib_v)
            stores = []
            for k in range(windows):
                loads[k].wait()
                for idx_v, sem in ((ia_v, sems[3 * k + 1]), (ib_v, sems[3 * k + 2])):
                    cp = pltpu.make_async_copy(bufs[k], o_hbm.at[idx_v.at[pl.ds(k * w, w)]], sem)
                    cp.start()
                    stores.append(cp)
            for cp in stores:
                cp.wait()

    return run(rows_in, pos_a, pos_b)


def _sc_gather(table, idx):
    m = idx.shape[0]
    d = table.shape[1]
    w = SC_WINDOW
    per_worker = m // SC_WORKERS
    chunks = per_worker // SC_INDEX_CHUNK
    windows = SC_INDEX_CHUNK // w
    dma = pltpu.SemaphoreType.DMA

    @pl.kernel(out_type=jax.ShapeDtypeStruct((m, d), table.dtype),
               mesh=plsc.VectorSubcoreMesh(**_SC_MESH),
               scratch_types=[pltpu.VMEM((SC_INDEX_CHUNK,), I32)]
               + [pltpu.VMEM((w, d), table.dtype)] * windows + [dma] * (2 * windows),
               name="moe_gather_sc")
    def run(x_hbm, i_hbm, o_hbm, i_v, *rest):
        bufs, sems = rest[:windows], rest[windows:]
        base = _sc_worker_base(per_worker)

        @pl.loop(0, chunks)
        def _(c):
            row0 = base + c * SC_INDEX_CHUNK
            pltpu.sync_copy(i_hbm.at[pl.ds(row0, SC_INDEX_CHUNK)], i_v)
            gathers = [pltpu.make_async_copy(x_hbm.at[i_v.at[pl.ds(k * w, w)]], bufs[k], sems[2 * k])
                       for k in range(windows)]
            for cp in gathers:
                cp.start()
            stores = []
            for k in range(windows):
                gathers[k].wait()
                cp = pltpu.make_async_copy(bufs[k], o_hbm.at[pl.ds(row0 + k * w, w)], sems[2 * k + 1])
                cp.start()
                stores.append(cp)
            for cp in stores:
                cp.wait()

    return run(table, idx)


def _expert_kernel(bexp_ref, bvalid_ref, nblk_ref, x_ref, wg_ref, wu_ref, wd_ref, y_ref, wg_s, wu_s, wd_s):
    jb = pl.program_id(0)
    valid = bvalid_ref[jb]

    @pl.when((jb == 0) | (bexp_ref[jb] != bexp_ref[jnp.maximum(jb - 1, 0)]))
    def _():
        wg_s[...] = wg_ref[0].astype(BF16)
        wu_s[...] = wu_ref[0].astype(BF16)
        wd_s[...] = wd_ref[0].astype(BF16)

    for r0 in range(0, EXPERT_ROWS, EXPERT_SUB):
        rows = slice(r0, r0 + EXPERT_SUB)

        @pl.when(valid > r0)
        def _():
            live = lax.broadcasted_iota(I32, (EXPERT_SUB, 1), 0) + r0 < valid
            lo, hi = _unpack_pair(jnp.where(live, x_ref[rows, :], jnp.uint32(0)))
            lo = lo.astype(BF16)
            hi = hi.astype(BF16)

            def project(w_s):
                return (jnp.dot(lo, w_s[0:D_PACK, :], preferred_element_type=F32)
                        + jnp.dot(hi, w_s[D_PACK:D_MODEL, :], preferred_element_type=F32))

            hb = (_silu(project(wg_s)) * project(wu_s)).astype(BF16)
            y = jnp.dot(hb, wd_s[...], preferred_element_type=F32)
            y_ref[rows, :] = _pack_pair(y[:, 0:D_PACK], y[:, D_PACK:D_MODEL])

        @pl.when(valid <= r0)
        def _():
            y_ref[rows, :] = jnp.zeros((EXPERT_SUB, D_PACK), U32)


def _experts(bexp, bvalid, nblk, xs, w_gate, w_up, w_down):
    n_rows, dp = xs.shape
    d = 2 * dp
    n_blocks = n_rows // EXPERT_ROWS
    return pl.pallas_call(
        _expert_kernel,
        grid_spec=pltpu.PrefetchScalarGridSpec(
            num_scalar_prefetch=3,
            grid=(n_blocks,),
            in_specs=[
                pl.BlockSpec((EXPERT_ROWS, dp), lambda jb, be, bv, nb: (jnp.minimum(jb, nb[0] - 1), 0)),
                pl.BlockSpec((1, d, D_EXPERT), lambda jb, be, bv, nb: (be[jb], 0, 0)),
                pl.BlockSpec((1, d, D_EXPERT), lambda jb, be, bv, nb: (be[jb], 0, 0)),
                pl.BlockSpec((1, D_EXPERT, d), lambda jb, be, bv, nb: (be[jb], 0, 0)),
            ],
            out_specs=pl.BlockSpec((EXPERT_ROWS, dp), lambda jb, be, bv, nb: (jb, 0)),
            scratch_shapes=[
                pltpu.VMEM((d, D_EXPERT), BF16),
                pltpu.VMEM((d, D_EXPERT), BF16),
                pltpu.VMEM((D_EXPERT, d), BF16),
            ],
        ),
        out_shape=jax.ShapeDtypeStruct((n_rows, dp), U32),
        compiler_params=pltpu.CompilerParams(dimension_semantics=("arbitrary",)),
        name="moe_experts",
    )(bexp, bvalid, nblk, xs, w_gate, w_up, w_down)


def _combine_kernel(rw_ref, x1_ref, gt2_ref, gpost_ref, ya_ref, yb_ref, o_ref):
    wt = rw_ref[0].T
    a_lo, a_hi = _unpack_pair(ya_ref[...])
    b_lo, b_hi = _unpack_pair(yb_ref[...])
    y_lo = a_lo * wt[:, 0:1] + b_lo * wt[:, 1:2]
    y_hi = a_hi * wt[:, 0:1] + b_hi * wt[:, 1:2]
    ssq = jnp.sum(y_lo * y_lo, axis=-1, keepdims=True) + jnp.sum(y_hi * y_hi, axis=-1, keepdims=True)
    inv = lax.rsqrt(ssq * (1.0 / D_MODEL) + EPS)
    gate = gt2_ref[0] * gpost_ref[...]
    o_ref[:, 0:D_PACK] = x1_ref[:, 0:D_PACK] + y_lo * inv * gate[:, 0:D_PACK]
    o_ref[:, D_PACK:D_MODEL] = x1_ref[:, D_PACK:D_MODEL] + y_hi * inv * gate[:, D_PACK:D_MODEL]


def _combine_kernel_into(prev_ref, *refs):
    del prev_ref
    _combine_kernel(*refs)


def _combine(rw, x1, gt2, g_post, yg, seq, piece, prev):
    n, d = x1.shape
    t = SEQ_TILE
    per_batch = seq // t
    tiles = yg.shape[0] // (TOP_K * t)
    tile0 = piece * tiles
    in_specs = [
        pl.BlockSpec((1, SUBLANES, t), lambda i: (i + tile0, 0, 0)),
        pl.BlockSpec((t, d), lambda i: (i + tile0, 0)),
        pl.BlockSpec((1, 1, d), lambda i: ((i + tile0) // per_batch, 0, 0)),
        pl.BlockSpec((1, d), lambda i: (0, 0)),
        pl.BlockSpec((t, D_PACK), lambda i: (i, 0)),
        pl.BlockSpec((t, D_PACK), lambda i: (i + tiles, 0)),
    ]
    args = (rw, x1, gt2, g_post, yg, yg)
    body, aliases = _combine_kernel, {}
    if prev is not None:
        in_specs = [pl.BlockSpec(memory_space=pl.ANY)] + in_specs
        args = (prev,) + args
        body, aliases = _combine_kernel_into, {0: 0}
    return pl.pallas_call(
        body,
        grid=(tiles,),
        in_specs=in_specs,
        out_specs=pl.BlockSpec((t, d), lambda i: (i + tile0, 0)),
        out_shape=jax.ShapeDtypeStruct((n, d), F32),
        input_output_aliases=aliases,
        compiler_params=pltpu.CompilerParams(dimension_semantics=("arbitrary",)),
        name="moe_combine",
    )(*args)


def kernel(x, c, w_ada, b_ada, g_pre_mix, g_post_mix, w_in, dw_kernel, dw_bias, conv_ln_gain, conv_ln_bias, lb_logits, rec_norm_gain, w_out, g_pre_ffn, g_post_ffn, w_router_group, b_router_group, w_router_expert, b_router_expert, w_gate, w_up, w_down):
    bsz, s, d = x.shape
    depth = w_ada.shape[0]
    assert depth == 1 and lb_logits.shape[0] == 2
    n_tok = bsz * s
    n_rows = n_tok * TOP_K + N_EXPERTS * EXPERT_ROWS
    for l in range(depth):
        mod = _ada(c, w_ada[l], b_ada[l])
        mod3 = mod.reshape(bsz, 6, d)
        pad = LANES - N_EXPERTS - N_GROUPS
        w_r = jnp.concatenate([w_router_expert[l], w_router_group[l], jnp.zeros((d, pad), F32)], axis=1)
        w_r_hi = w_r.astype(BF16)
        w_r = jnp.concatenate([w_r_hi, (w_r - w_r_hi.astype(F32)).astype(BF16)], axis=1)
        b_r = jnp.concatenate([b_router_expert[l], b_router_group[l], jnp.zeros((pad,), F32)]).reshape(1, LANES)
        x1, h2p, ri, rw, cnt = _mix(
            x, mod3, g_pre_mix[l].reshape(1, d), g_post_mix[l].reshape(1, d), g_pre_ffn[l].reshape(1, d),
            w_in[l].astype(BF16), dw_kernel[l], dw_bias[l].reshape(1, D_CONV),
            conv_ln_gain[l].reshape(1, D_CONV), conv_ln_bias[l].reshape(1, D_CONV),
            lb_logits, rec_norm_gain[l].reshape(1, REC_DV), w_out[l].astype(BF16), w_r, b_r)
        counts = cnt[:, 0].astype(I32)
        starts, bexp, bvalid, nblk = _meta(counts, n_rows // EXPERT_ROWS)
        pos = _positions(starts, ri)
        pos_a = pos[:, 0, :].reshape(n_tok)
        pos_b = pos[:, 1, :].reshape(n_tok)
        xs = _sc_dispatch(h2p.reshape(n_tok, D_PACK), pos_a, pos_b, n_rows)
        ys = _experts(bexp, bvalid, nblk, xs, w_gate[l], w_up[l], w_down[l])
        piece_tok = n_tok // COMBINE_PIECES
        out = None
        for p in range(COMBINE_PIECES):
            tok = slice(p * piece_tok, (p + 1) * piece_tok)
            yg = _sc_gather(ys, jnp.concatenate([pos_a[tok], pos_b[tok]]))
            out = _combine(rw, x1.reshape(n_tok, d), mod3[:, 5:6, :], g_post_ffn[l].reshape(1, d), yg, s, p, out)
        x = out.reshape(bsz, s, d)
    return x
```

```python
import jax
import jax.numpy as jnp
from jax import lax
from jax.experimental import pallas as pl
from jax.experimental.pallas import tpu as pltpu
from jax.experimental.pallas import tpu_sc as plsc

D_MODEL = 1024
D_CONV = 512
D_REC = 512
CONV_WIDTH = 31
REC_HEADS = 4
REC_DK = 128
REC_DV = 128
CHUNK = 64
D_IN = 2 * D_CONV + 4 * D_REC
N_GROUPS = 4
EXPERTS_PER_GROUP = 8
N_EXPERTS = 32
TOP_K = 2
D_EXPERT = 256
EPS = 1e-6

LANES = 128
SUBLANES = 8
SEQ_TILE = 512
CONV_ROWS = 64
ROW_STEP = 32
STAT_ROWS = 64
CONV_HALO = 32
GROUP_ROW0 = 32
EXPERT_ROWS = 512
COMBINE_PIECES = 4
EXPERT_SUB = 256
SC_WINDOW = 64
D_PACK = D_MODEL // 2
VMEM_LIMIT = 56 * 1024 * 1024

F32 = jnp.float32
BF16 = jnp.bfloat16
I32 = jnp.int32
U32 = jnp.uint32
HI = lax.Precision.HIGHEST


def _sigmoid(v):
    return 0.5 * jnp.tanh(0.5 * v) + 0.5


def _silu(v):
    return v * _sigmoid(v)


def _rms(v, gain):
    return v * lax.rsqrt(jnp.mean(v * v, axis=-1, keepdims=True) + EPS) * gain


def _pack_pair(lo, hi):
    lo_bits = lax.bitcast_convert_type(lo.astype(BF16).astype(F32), U32)
    hi_bits = lax.bitcast_convert_type(hi.astype(BF16).astype(F32), U32)
    return (lo_bits >> 16) | (hi_bits & jnp.uint32(0xFFFF0000))


def _unpack_pair(packed):
    lo = lax.bitcast_convert_type(packed << 16, F32)
    hi = lax.bitcast_convert_type(packed & jnp.uint32(0xFFFF0000), F32)
    return lo, hi


def _ada_kernel(c_ref, w_ref, b_ref, o_ref):
    cond = _silu(c_ref[...])
    o_ref[...] = jnp.dot(cond, w_ref[...], precision=HI, preferred_element_type=F32) + b_ref[...]


def _ada(c, w, b):
    bsz, d = c.shape
    n = w.shape[1]
    return pl.pallas_call(
        _ada_kernel,
        grid=(n // d,),
        in_specs=[
            pl.BlockSpec((bsz, d), lambda j: (0, 0)),
            pl.BlockSpec((d, d), lambda j: (0, j)),
            pl.BlockSpec((1, d), lambda j: (0, j)),
        ],
        out_specs=pl.BlockSpec((bsz, d), lambda j: (0, j)),
        out_shape=jax.ShapeDtypeStruct((bsz, n), F32),
        name="ada_mod",
    )(c, w, b.reshape(1, n))


def _mix_kernel(x_ref, mod_ref, gpre_ref, gpost_ref, gffn_ref, win_ref, dwk_ref, dwb_ref,
                lng_ref, lnb_ref, lbl_ref, rng_ref, wout_ref, wr_ref, br_ref, upper_ref,
                x1_ref, h2p_ref, ri_ref, rw_ref, cnt_ref,
                h2_s, st1, st2, vec_s, hb_s, proj_s, ubuf, cv_s, lfp_s, k_s, b_s, qd_s, ki_s, ke_s, v_s, a_s, o_s, yb, y_s,
                state, carry):
    b = pl.program_id(0)
    j = pl.program_id(1)
    t = SEQ_TILE
    n_chunks = t // CHUNK
    heads = range(REC_HEADS)

    @pl.when(j == 0)
    def _():
        ubuf[0:CONV_HALO, :] = jnp.zeros((CONV_HALO, D_CONV), F32)
        state[...] = jnp.zeros(state.shape, F32)

    @pl.when((j == 0) & (b == 0))
    def _():
        carry[...] = jnp.zeros(carry.shape, F32)

    sh1 = mod_ref[0, 0:1, :]
    sc1 = mod_ref[0, 1:2, :]
    gt1 = mod_ref[0, 2:3, :]
    sh2 = mod_ref[0, 3:4, :]
    sc2 = mod_ref[0, 4:5, :]

    def row_loop(rows_per_step, body):
        def step(ci, c_):
            body(pl.ds(pl.multiple_of(ci * rows_per_step, rows_per_step), rows_per_step))
            return c_
        lax.fori_loop(0, t // rows_per_step, step, 0)

    stat_blocks = [slice(r0, r0 + STAT_ROWS) for r0 in range(0, t, STAT_ROWS)]

    def head_cols(base, hd):
        return slice(base + hd * REC_DK, base + (hd + 1) * REC_DK)

    def lane_tiles(width):
        return [slice(l0, l0 + LANES) for l0 in range(0, width, LANES)]

    def put_stat(ref, rows, stat):
        ref[rows, :] = jnp.broadcast_to(stat, (STAT_ROWS, LANES))

    vec_s[0:1, :] = gpre_ref[...] * (1.0 + sc1)
    vec_s[1:2, :] = sh1
    vec_s[2:3, :] = gt1 * gpost_ref[...]
    vec_s[3:4, :] = gffn_ref[...] * (1.0 + sc2)
    vec_s[4:5, :] = sh2

    for rows in stat_blocks:
        xc = x_ref[0, rows, :]
        put_stat(st1, rows, lax.rsqrt(jnp.mean(xc * xc, axis=-1, keepdims=True) + EPS))

    def prenorm(rows):
        inv = st1[rows, :]
        for lanes in lane_tiles(D_MODEL):
            hb_s[rows, lanes] = (x_ref[0, rows, lanes] * inv * vec_s[0:1, lanes] + vec_s[1:2, lanes]).astype(BF16)

    row_loop(ROW_STEP, prenorm)
    proj_s[...] = jnp.dot(hb_s[...], win_ref[...], preferred_element_type=F32)

    def glu(rows):
        cv = proj_s[rows, 0:D_CONV]
        cg = proj_s[rows, D_CONV:2 * D_CONV]
        ubuf[pl.ds(pl.multiple_of(rows.start + CONV_HALO, ROW_STEP), ROW_STEP), :] = cv * _sigmoid(cg)

    row_loop(ROW_STEP, glu)

    lead = CONV_HALO - (CONV_WIDTH - 1)
    win_rows = CONV_ROWS + CONV_HALO

    def conv(rows):
        for lt in range(D_CONV // LANES):
            lanes = slice(lt * LANES, (lt + 1) * LANES)
            win = ubuf[pl.ds(rows.start, win_rows), lanes]
            acc = jnp.broadcast_to(dwb_ref[:, lanes], (CONV_ROWS, LANES))
            for res in range(SUBLANES):
                shifted = win if res == 0 else pltpu.roll(win, win_rows - res, axis=0)
                for al in range(0, win_rows - CONV_ROWS + 1, SUBLANES):
                    kk = al + res - lead
                    if 0 <= kk < CONV_WIDTH and al + CONV_ROWS + res <= win_rows:
                        acc = acc + shifted[al:al + CONV_ROWS] * dwk_ref[kk:kk + 1, lanes]
            cv_s[rows, lanes] = acc

    row_loop(CONV_ROWS, conv)
    ubuf[0:CONV_HALO, :] = ubuf[t:t + CONV_HALO, :]

    for rows in stat_blocks:
        acc = cv_s[rows, :]
        mu = jnp.mean(acc, axis=-1, keepdims=True)
        xc = acc - mu
        put_stat(st1, rows, mu)
        put_stat(st2, rows, lax.rsqrt(jnp.mean(xc * xc, axis=-1, keepdims=True) + EPS))

    def conv_norm(rows):
        mu = st1[rows, :]
        inv = st2[rows, :]
        for lanes in lane_tiles(D_CONV):
            yc = (cv_s[rows, lanes] - mu) * inv * lng_ref[:, lanes] + lnb_ref[:, lanes]
            yb[rows, lanes] = _silu(yc).astype(BF16)

    row_loop(ROW_STEP, conv_norm)

    q0 = 2 * D_CONV
    f0 = q0 + D_REC
    i0 = f0 + D_REC
    g0 = i0 + D_REC
    l0 = lbl_ref[0:1, :]
    lmax = jnp.max(lbl_ref[...], axis=0, keepdims=True)
    lb = jnp.exp(l0 - lmax) / jnp.sum(jnp.exp(lbl_ref[...] - lmax), axis=0, keepdims=True)

    def gates(rows):
        for hd in heads:
            cols = head_cols(0, hd)
            forget = lb[:, cols] + (1.0 - lb[:, cols]) * _sigmoid(proj_s[rows, head_cols(f0, hd)])
            k_s[rows, cols] = 1.0 - forget
            lf = jnp.log(forget)
            hi = lf.astype(BF16)
            rem = lf - hi.astype(F32)
            mid = rem.astype(BF16)
            lfp_s[0, rows, cols] = hi
            lfp_s[1, rows, cols] = mid
            lfp_s[2, rows, cols] = (rem - mid.astype(F32)).astype(BF16)

    row_loop(CHUNK, gates)

    row = lax.broadcasted_iota(I32, (CHUNK, CHUNK), 0)
    col = lax.broadcasted_iota(I32, (CHUNK, CHUNK), 1)
    causal = row >= col
    tri = jnp.where(causal, 1.0, 0.0).astype(BF16)
    for ci in range(n_chunks):
        rows = slice(ci * CHUNK, (ci + 1) * CHUNK)
        b_s[rows, :] = (jnp.dot(tri, lfp_s[0, rows, :], preferred_element_type=F32)
                        + jnp.dot(tri, lfp_s[1, rows, :], preferred_element_type=F32)
                        + jnp.dot(tri, lfp_s[2, rows, :], preferred_element_type=F32))

    def decays(ci, c_):
        rows = pl.ds(pl.multiple_of(ci * CHUNK, CHUNK), CHUNK)
        for hd in heads:
            cols = head_cols(0, hd)
            bcum = b_s[rows, cols]
            k_c = k_s[rows, cols]
            a_last = jnp.exp(bcum[CHUNK - 1:CHUNK, :])
            k_inv = k_c * jnp.exp(-bcum)
            qd_s[rows, cols] = (_silu(proj_s[rows, head_cols(q0, hd)]) * jnp.exp(bcum)).astype(BF16)
            ki_s[rows, cols] = k_inv.astype(BF16)
            ke_s[rows, cols] = (k_inv * a_last).astype(BF16)
            v_s[rows, cols] = proj_s[rows, head_cols(i0, hd)].astype(BF16)
            a_s[ci, :, cols] = a_last
        return c_

    lax.fori_loop(0, n_chunks, decays, 0)

    nt_dims = (((1,), (1,)), ((), ()))
    pairs = [(ci, hd) for ci in range(n_chunks) for hd in heads]

    def blk(ci, hd):
        return slice(ci * CHUNK, (ci + 1) * CHUNK), head_cols(0, hd)

    scores = {}
    for p in pairs:
        rows, cols = blk(*p)
        sc = lax.dot_general(qd_s[rows, cols], ki_s[rows, cols], nt_dims, preferred_element_type=F32)
        scores[p] = jnp.where(causal, sc, 0.0).astype(BF16)
    for p in pairs:
        rows, cols = blk(*p)
        o_s[rows, cols] = jnp.dot(scores[p], v_s[rows, cols], preferred_element_type=F32)
    upd = {}
    for p in pairs:
        rows, cols = blk(*p)
        v_t = v_s[rows, cols].astype(F32).T.astype(BF16)
        upd[p] = jnp.dot(v_t, ke_s[rows, cols], preferred_element_type=F32)
    prev = {}
    for hd in heads:
        st = state[hd]
        for ci in range(n_chunks):
            prev[(ci, hd)] = st.astype(BF16)
            st = st * a_s[ci, :, head_cols(0, hd)] + upd[(ci, hd)]
        state[hd] = st
    for p in pairs:
        rows, cols = blk(*p)
        o_s[rows, cols] += lax.dot_general(qd_s[rows, cols], prev[p], nt_dims, preferred_element_type=F32)

    for rows in stat_blocks:
        for hd in heads:
            o = _rms(o_s[rows, head_cols(0, hd)], rng_ref[...]) * _silu(proj_s[rows, head_cols(g0, hd)])
            yb[rows, head_cols(D_CONV, hd)] = o.astype(BF16)

    y_s[...] = jnp.dot(yb[...], wout_ref[...], preferred_element_type=F32)
    for rows in stat_blocks:
        yc = y_s[rows, :]
        put_stat(st1, rows, lax.rsqrt(jnp.mean(yc * yc, axis=-1, keepdims=True) + EPS))

    def residual(rows):
        inv = st1[rows, :]
        for lanes in lane_tiles(D_MODEL):
            x1_ref[0, rows, lanes] = x_ref[0, rows, lanes] + y_s[rows, lanes] * inv * vec_s[2:3, lanes]

    row_loop(ROW_STEP, residual)
    for rows in stat_blocks:
        xc = x1_ref[0, rows, :]
        put_stat(st2, rows, lax.rsqrt(jnp.mean(xc * xc, axis=-1, keepdims=True) + EPS))

    def ffn_norm(rows):
        inv = st2[rows, :]
        for lanes in lane_tiles(D_PACK):
            upper = slice(lanes.start + D_PACK, lanes.stop + D_PACK)
            lo = x1_ref[0, rows, lanes] * inv * vec_s[3:4, lanes] + vec_s[4:5, lanes]
            hi = x1_ref[0, rows, upper] * inv * vec_s[3:4, upper] + vec_s[4:5, upper]
            h2_s[rows, lanes] = lo
            h2_s[rows, upper] = hi
            h2p_ref[0, rows, lanes] = _pack_pair(lo, hi)

    row_loop(ROW_STEP, ffn_norm)

    h2v = h2_s[...]
    h_hi = h2v.astype(BF16)
    h_lo = (h2v - h_hi.astype(F32)).astype(BF16)
    both = jnp.dot(h_hi, wr_ref[...], preferred_element_type=F32)
    logits = (both[:, 0:LANES] + both[:, LANES:2 * LANES]
              + jnp.dot(h_lo, wr_ref[:, 0:LANES], preferred_element_type=F32)) + br_ref[...]
    lt = logits.T
    neg = jnp.float32(-jnp.inf)
    r8 = lax.broadcasted_iota(I32, (SUBLANES, t), 0)
    gl = jnp.where(r8 < N_GROUPS, lt[GROUP_ROW0:GROUP_ROW0 + SUBLANES], neg)
    gmax = jnp.max(gl, axis=0, keepdims=True)
    gidx = jnp.min(jnp.where(gl == gmax, r8, SUBLANES), axis=0, keepdims=True)
    gprob = 1.0 / jnp.sum(jnp.exp(gl - gmax), axis=0, keepdims=True)
    re = lax.broadcasted_iota(I32, (N_EXPERTS, t), 0)
    el = jnp.where((re // EXPERTS_PER_GROUP) == gidx, lt[0:N_EXPERTS], neg)
    m1 = jnp.max(el, axis=0, keepdims=True)
    i1 = jnp.min(jnp.where(el == m1, re, N_EXPERTS), axis=0, keepdims=True)
    el2 = jnp.where(re == i1, neg, el)
    m2 = jnp.max(el2, axis=0, keepdims=True)
    i2 = jnp.min(jnp.where(el2 == m2, re, N_EXPERTS), axis=0, keepdims=True)
    r = jnp.exp(m2 - m1)
    w1 = gprob / (1.0 + r)
    w2 = gprob * r / (1.0 + r)
    hot1 = re == i1
    hot2 = re == i2
    hot = jnp.where(hot1 | hot2, 1.0, 0.0)
    prefix = jnp.dot(hot.astype(BF16), upper_ref[...], preferred_element_type=F32) + carry[...]
    rank1 = jnp.sum(jnp.where(hot1, prefix, 0.0), axis=0, keepdims=True)
    rank2 = jnp.sum(jnp.where(hot2, prefix, 0.0), axis=0, keepdims=True)
    carry[...] = carry[...] + jnp.sum(hot, axis=1, keepdims=True)
    zi = jnp.zeros((SUBLANES - 4, t), I32)
    ri_ref[0] = jnp.concatenate([i1, i2, rank1.astype(I32), rank2.astype(I32), zi], axis=0)
    rw_ref[0] = jnp.concatenate([w1, w2, jnp.zeros((SUBLANES - 2, t), F32)], axis=0)
    cnt_ref[...] = jnp.broadcast_to(carry[...], cnt_ref.shape)


def _mix(x, mod3, g_pre, g_post, g_ffn, w_in, dwk, dwb, lng, lnb, lbl, rng, w_out, w_r, b_r):
    bsz, s, d = x.shape
    t = SEQ_TILE
    nt = s // t
    tile = lambda b, j: (b, j, 0)
    rtile = lambda b, j: (b * nt + j, 0, 0)
    const2 = lambda b, j: (0, 0)

    def const_spec(shape):
        return pl.BlockSpec(shape, const2, pipeline_mode=pl.Buffered(1))

    upper = jnp.triu(jnp.ones((t, t), BF16), k=1)
    return pl.pallas_call(
        _mix_kernel,
        grid=(bsz, nt),
        in_specs=[
            pl.BlockSpec((1, t, d), tile),
            pl.BlockSpec((1, 6, d), lambda b, j: (b, 0, 0)),
            const_spec((1, d)),
            const_spec((1, d)),
            const_spec((1, d)),
            const_spec((d, D_IN)),
            const_spec((CONV_WIDTH, D_CONV)),
            const_spec((1, D_CONV)),
            const_spec((1, D_CONV)),
            const_spec((1, D_CONV)),
            const_spec((2, D_REC)),
            const_spec((1, REC_DV)),
            const_spec((d, d)),
            const_spec((d, 2 * LANES)),
            const_spec((1, LANES)),
            const_spec((t, t)),
        ],
        out_specs=[
            pl.BlockSpec((1, t, d), tile),
            pl.BlockSpec((1, t, D_PACK), tile),
            pl.BlockSpec((1, SUBLANES, t), rtile),
            pl.BlockSpec((1, SUBLANES, t), rtile),
            pl.BlockSpec((N_EXPERTS, LANES), const2),
        ],
        out_shape=[
            jax.ShapeDtypeStruct((bsz, s, d), F32),
            jax.ShapeDtypeStruct((bsz, s, D_PACK), U32),
            jax.ShapeDtypeStruct((bsz * nt, SUBLANES, t), I32),
            jax.ShapeDtypeStruct((bsz * nt, SUBLANES, t), F32),
            jax.ShapeDtypeStruct((N_EXPERTS, LANES), F32),
        ],
        scratch_shapes=[
            pltpu.VMEM((t, d), F32),
            pltpu.VMEM((t, LANES), F32),
            pltpu.VMEM((t, LANES), F32),
            pltpu.VMEM((SUBLANES, d), F32),
            pltpu.VMEM((t, d), BF16),
            pltpu.VMEM((t, D_IN), F32),
            pltpu.VMEM((CONV_HALO + t, D_CONV), F32),
            pltpu.VMEM((t, D_CONV), F32),
            pltpu.VMEM((3, t, D_REC), BF16),
            pltpu.VMEM((t, D_REC), F32),
            pltpu.VMEM((t, D_REC), F32),
            pltpu.VMEM((t, D_REC), BF16),
            pltpu.VMEM((t, D_REC), BF16),
            pltpu.VMEM((t, D_REC), BF16),
            pltpu.VMEM((t, D_REC), BF16),
            pltpu.VMEM((t // CHUNK, 1, D_REC), F32),
            pltpu.VMEM((t, D_REC), F32),
            pltpu.VMEM((t, d), BF16),
            pltpu.VMEM((t, d), F32),
            pltpu.VMEM((REC_HEADS, REC_DV, REC_DK), F32),
            pltpu.VMEM((N_EXPERTS, 1), F32),
        ],
        compiler_params=pltpu.CompilerParams(
            dimension_semantics=("arbitrary", "arbitrary"),
            vmem_limit_bytes=VMEM_LIMIT),
        name="mixer",
    )(x, mod3, g_pre, g_post, g_ffn, w_in, dwk, dwb, lng, lnb, lbl, rng, w_out, w_r, b_r, upper)


def _meta_kernel(cnt_ref, start_ref, bexp_ref, bvalid_ref, nblk_ref):
    shift = EXPERT_ROWS.bit_length() - 1
    n_blocks = bexp_ref.shape[0]

    def fill(e, blk0, cnt):
        def body(jb, c_):
            bexp_ref[jb] = e
            bvalid_ref[jb] = jnp.clip(cnt - ((jb - blk0) << shift), 0, EXPERT_ROWS)
            return c_
        return body

    def per_expert(e, blk0):
        nb = (cnt_ref[e] + (EXPERT_ROWS - 1)) >> shift
        start_ref[e] = blk0 << shift
        lax.fori_loop(blk0, blk0 + nb, fill(e, blk0, cnt_ref[e]), 0)
        return blk0 + nb

    used = lax.fori_loop(0, N_EXPERTS, per_expert, jnp.int32(0))
    lax.fori_loop(used, n_blocks, fill(N_EXPERTS - 1, used, 0), 0)
    nblk_ref[0] = used


def _meta(counts, n_blocks):
    smem = pl.BlockSpec(memory_space=pltpu.SMEM)
    return pl.pallas_call(
        _meta_kernel,
        in_specs=[smem],
        out_specs=[smem, smem, smem, smem],
        out_shape=[
            jax.ShapeDtypeStruct((N_EXPERTS,), I32),
            jax.ShapeDtypeStruct((n_blocks,), I32),
            jax.ShapeDtypeStruct((n_blocks,), I32),
            jax.ShapeDtypeStruct((1,), I32),
        ],
        name="moe_layout",
    )(counts)


def _positions_kernel(start_ref, ri_ref, pos_ref):
    e = ri_ref[:, 0:TOP_K, :]
    seg = jnp.zeros(e.shape, I32)
    for k in range(N_EXPERTS):
        seg = jnp.where(e == k, start_ref[k], seg)
    pos = seg + ri_ref[:, TOP_K:2 * TOP_K, :]
    pad = jnp.zeros((e.shape[0], SUBLANES - TOP_K, e.shape[2]), I32)
    pos_ref[...] = jnp.concatenate([pos, pad], axis=1)


def _positions(starts, ri):
    n_tiles, _, t = ri.shape
    tb = SUBLANES
    return pl.pallas_call(
        _positions_kernel,
        grid_spec=pltpu.PrefetchScalarGridSpec(
            num_scalar_prefetch=1,
            grid=(n_tiles // tb,),
            in_specs=[pl.BlockSpec((tb, SUBLANES, t), lambda i, s_: (i, 0, 0))],
            out_specs=pl.BlockSpec((tb, SUBLANES, t), lambda i, s_: (i, 0, 0)),
        ),
        out_shape=jax.ShapeDtypeStruct(ri.shape, I32),
        name="moe_positions",
    )(starts, ri)


_SC_MESH = dict(core_axis_name="core", subcore_axis_name="subcore")
SC_WORKERS = 32
SC_INDEX_CHUNK = 128


def _sc_worker_base(rows_per_worker):
    wid = lax.axis_index("core") * (SC_WORKERS // 2) + lax.axis_index("subcore")
    return wid * rows_per_worker


def _sc_dispatch(rows_in, pos_a, pos_b, n_rows):
    n, d = rows_in.shape
    w = SC_WINDOW
    per_worker = n // SC_WORKERS
    chunks = per_worker // SC_INDEX_CHUNK
    windows = SC_INDEX_CHUNK // w
    dma = pltpu.SemaphoreType.DMA

    @pl.kernel(out_type=jax.ShapeDtypeStruct((n_rows, d), rows_in.dtype),
               mesh=plsc.VectorSubcoreMesh(**_SC_MESH),
               scratch_types=[pltpu.VMEM((SC_INDEX_CHUNK,), I32), pltpu.VMEM((SC_INDEX_CHUNK,), I32)]
               + [pltpu.VMEM((w, d), rows_in.dtype)] * windows + [dma] * (3 * windows),
               name="moe_dispatch_sc")
    def run(x_hbm, ia_hbm, ib_hbm, o_hbm, ia_v, ib_v, *rest):
        bufs, sems = rest[:windows], rest[windows:]
        base = _sc_worker_base(per_worker)

        @pl.loop(0, chunks)
        def _(c):
            row0 = base + c * SC_INDEX_CHUNK
            loads = [pltpu.make_async_copy(x_hbm.at[pl.ds(row0 + k * w, w)], bufs[k], sems[3 * k])
                     for k in range(windows)]
            for cp in loads:
                cp.start()
            pltpu.sync_copy(ia_hbm.at[pl.ds(row0, SC_INDEX_CHUNK)], ia_v)
            pltpu.sync_copy(ib_hbm.at[pl.ds(row0, SC_INDEX_CHUNK)], ib_v)
            stores = []
            for k in range(windows):
                loads[k].wait()
                for idx_v, sem in ((ia_v, sems[3 * k + 1]), (ib_v, sems[3 * k + 2])):
                    cp = pltpu.make_async_copy(bufs[k], o_hbm.at[idx_v.at[pl.ds(k * w, w)]], sem)
                    cp.start()
                    stores.append(cp)
            for cp in stores:
                cp.wait()

    return run(rows_in, pos_a, pos_b)


def _sc_gather(table, idx):
    m = idx.shape[0]
    d = table.shape[1]
    w = SC_WINDOW
    per_worker = m // SC_WORKERS
    chunks = per_worker // SC_INDEX_CHUNK
    windows = SC_INDEX_CHUNK // w
    dma = pltpu.SemaphoreType.DMA

    @pl.kernel(out_type=jax.ShapeDtypeStruct((m, d), table.dtype),
               mesh=plsc.VectorSubcoreMesh(**_SC_MESH),
               scratch_types=[pltpu.VMEM((SC_INDEX_CHUNK,), I32)]
               + [pltpu.VMEM((w, d), table.dtype)] * windows + [dma] * (2 * windows),
               name="moe_gather_sc")
    def run(x_hbm, i_hbm, o_hbm, i_v, *rest):
        bufs, sems = rest[:windows], rest[windows:]
        base = _sc_worker_base(per_worker)

        @pl.loop(0, chunks)
        def _(c):
            row0 = base + c * SC_INDEX_CHUNK
            pltpu.sync_copy(i_hbm.at[pl.ds(row0, SC_INDEX_CHUNK)], i_v)
            gathers = [pltpu.make_async_copy(x_hbm.at[i_v.at[pl.ds(k * w, w)]], bufs[k], sems[2 * k])
                       for k in range(windows)]
            for cp in gathers:
                cp.start()
            stores = []
            for k in range(windows):
                gathers[k].wait()
                cp = pltpu.make_async_copy(bufs[k], o_hbm.at[pl.ds(row0 + k * w, w)], sems[2 * k + 1])
                cp.start()
                stores.append(cp)
            for cp in stores:
                cp.wait()

    return run(table, idx)


def _expert_kernel(bexp_ref, bvalid_ref, nblk_ref, x_ref, wg_ref, wu_ref, wd_ref, y_ref, wg_s, wu_s, wd_s):
    jb = pl.program_id(0)
    valid = bvalid_ref[jb]

    @pl.when((jb == 0) | (bexp_ref[jb] != bexp_ref[jnp.maximum(jb - 1, 0)]))
    def _():
        wg_s[...] = wg_ref[0].astype(BF16)
        wu_s[...] = wu_ref[0].astype(BF16)
        wd_s[...] = wd_ref[0].astype(BF16)

    subs = [slice(r0, r0 + EXPERT_SUB) for r0 in range(0, EXPERT_ROWS, EXPERT_SUB)]
    passes = (valid + (EXPERT_SUB - 1)) // EXPERT_SUB

    def run(n_live):
        halves = []
        for rows in subs[:n_live]:
            live = lax.broadcasted_iota(I32, (EXPERT_SUB, 1), 0) + rows.start < valid
            lo, hi = _unpack_pair(jnp.where(live, x_ref[rows, :], jnp.uint32(0)))
            halves.append((lo.astype(BF16), hi.astype(BF16)))

        def project(w_s):
            return [jnp.dot(lo, w_s[0:D_PACK, :], preferred_element_type=F32)
                    + jnp.dot(hi, w_s[D_PACK:D_MODEL, :], preferred_element_type=F32) for lo, hi in halves]

        hidden = [(_silu(g) * u).astype(BF16) for g, u in zip(project(wg_s), project(wu_s))]
        outs = [jnp.dot(hb, wd_s[...], preferred_element_type=F32) for hb in hidden]
        for rows, y in zip(subs, outs):
            y_ref[rows, :] = _pack_pair(y[:, 0:D_PACK], y[:, D_PACK:D_MODEL])
        for rows in subs[n_live:]:
            y_ref[rows, :] = jnp.zeros((EXPERT_SUB, D_PACK), U32)

    for n_live in range(len(subs) + 1):
        pl.when(passes == n_live)(lambda n_live=n_live: run(n_live))


def _experts(bexp, bvalid, nblk, xs, w_gate, w_up, w_down):
    n_rows, dp = xs.shape
    d = 2 * dp
    n_blocks = n_rows // EXPERT_ROWS
    return pl.pallas_call(
        _expert_kernel,
        grid_spec=pltpu.PrefetchScalarGridSpec(
            num_scalar_prefetch=3,
            grid=(n_blocks,),
            in_specs=[
                pl.BlockSpec((EXPERT_ROWS, dp), lambda jb, be, bv, nb: (jnp.minimum(jb, nb[0] - 1), 0)),
                pl.BlockSpec((1, d, D_EXPERT), lambda jb, be, bv, nb: (be[jb], 0, 0)),
                pl.BlockSpec((1, d, D_EXPERT), lambda jb, be, bv, nb: (be[jb], 0, 0)),
                pl.BlockSpec((1, D_EXPERT, d), lambda jb, be, bv, nb: (be[jb], 0, 0)),
            ],
            out_specs=pl.BlockSpec((EXPERT_ROWS, dp), lambda jb, be, bv, nb: (jb, 0)),
            scratch_shapes=[
                pltpu.VMEM((d, D_EXPERT), BF16),
                pltpu.VMEM((d, D_EXPERT), BF16),
                pltpu.VMEM((D_EXPERT, d), BF16),
            ],
        ),
        out_shape=jax.ShapeDtypeStruct((n_rows, dp), U32),
        compiler_params=pltpu.CompilerParams(dimension_semantics=("arbitrary",)),
        name="moe_experts",
    )(bexp, bvalid, nblk, xs, w_gate, w_up, w_down)


def _combine_kernel(rw_ref, x1_ref, gt2_ref, gpost_ref, ya_ref, yb_ref, o_ref):
    wt = rw_ref[0].T
    a_lo, a_hi = _unpack_pair(ya_ref[...])
    b_lo, b_hi = _unpack_pair(yb_ref[...])
    y_lo = a_lo * wt[:, 0:1] + b_lo * wt[:, 1:2]
    y_hi = a_hi * wt[:, 0:1] + b_hi * wt[:, 1:2]
    ssq = jnp.sum(y_lo * y_lo, axis=-1, keepdims=True) + jnp.sum(y_hi * y_hi, axis=-1, keepdims=True)
    inv = lax.rsqrt(ssq * (1.0 / D_MODEL) + EPS)
    gate = gt2_ref[0] * gpost_ref[...]
    o_ref[:, 0:D_PACK] = x1_ref[:, 0:D_PACK] + y_lo * inv * gate[:, 0:D_PACK]
    o_ref[:, D_PACK:D_MODEL] = x1_ref[:, D_PACK:D_MODEL] + y_hi * inv * gate[:, D_PACK:D_MODEL]


def _combine_kernel_into(prev_ref, *refs):
    del prev_ref
    _combine_kernel(*refs)


def _combine(rw, x1, gt2, g_post, yg, seq, piece, prev):
    n, d = x1.shape
    t = SEQ_TILE
    per_batch = seq // t
    tiles = yg.shape[0] // (TOP_K * t)
    tile0 = piece * tiles
    in_specs = [
        pl.BlockSpec((1, SUBLANES, t), lambda i: (i + tile0, 0, 0)),
        pl.BlockSpec((t, d), lambda i: (i + tile0, 0)),
        pl.BlockSpec((1, 1, d), lambda i: ((i + tile0) // per_batch, 0, 0)),
        pl.BlockSpec((1, d), lambda i: (0, 0)),
        pl.BlockSpec((t, D_PACK), lambda i: (i, 0)),
        pl.BlockSpec((t, D_PACK), lambda i: (i + tiles, 0)),
    ]
    args = (rw, x1, gt2, g_post, yg, yg)
    body, aliases = _combine_kernel, {}
    if prev is not None:
        in_specs = [pl.BlockSpec(memory_space=pl.ANY)] + in_specs
        args = (prev,) + args
        body, aliases = _combine_kernel_into, {0: 0}
    return pl.pallas_call(
        body,
        grid=(tiles,),
        in_specs=in_specs,
        out_specs=pl.BlockSpec((t, d), lambda i: (i + tile0, 0)),
        out_shape=jax.ShapeDtypeStruct((n, d), F32),
        input_output_aliases=aliases,
        compiler_params=pltpu.CompilerParams(dimension_semantics=("arbitrary",)),
        name="moe_combine",
    )(*args)


def kernel(x, c, w_ada, b_ada, g_pre_mix, g_post_mix, w_in, dw_kernel, dw_bias, conv_ln_gain, conv_ln_bias, lb_logits, rec_norm_gain, w_out, g_pre_ffn, g_post_ffn, w_router_group, b_router_group, w_router_expert, b_router_expert, w_gate, w_up, w_down):
    bsz, s, d = x.shape
    depth = w_ada.shape[0]
    assert depth == 1 and lb_logits.shape[0] == 2
    n_tok = bsz * s
    n_rows = n_tok * TOP_K + N_EXPERTS * EXPERT_ROWS
    for l in range(depth):
        mod = _ada(c, w_ada[l], b_ada[l])
        mod3 = mod.reshape(bsz, 6, d)
        pad = LANES - N_EXPERTS - N_GROUPS
        w_r = jnp.concatenate([w_router_expert[l], w_router_group[l], jnp.zeros((d, pad), F32)], axis=1)
        w_r_hi = w_r.astype(BF16)
        w_r = jnp.concatenate([w_r_hi, (w_r - w_r_hi.astype(F32)).astype(BF16)], axis=1)
        b_r = jnp.concatenate([b_router_expert[l], b_router_group[l], jnp.zeros((pad,), F32)]).reshape(1, LANES)
        x1, h2p, ri, rw, cnt = _mix(
            x, mod3, g_pre_mix[l].reshape(1, d), g_post_mix[l].reshape(1, d), g_pre_ffn[l].reshape(1, d),
            w_in[l].astype(BF16), dw_kernel[l], dw_bias[l].reshape(1, D_CONV),
            conv_ln_gain[l].reshape(1, D_CONV), conv_ln_bias[l].reshape(1, D_CONV),
            lb_logits, rec_norm_gain[l].reshape(1, REC_DV), w_out[l].astype(BF16), w_r, b_r)
        counts = cnt[:, 0].astype(I32)
        starts, bexp, bvalid, nblk = _meta(counts, n_rows // EXPERT_ROWS)
        pos = _positions(starts, ri)
        pos_a = pos[:, 0, :].reshape(n_tok)
        pos_b = pos[:, 1, :].reshape(n_tok)
        xs = _sc_dispatch(h2p.reshape(n_tok, D_PACK), pos_a, pos_b, n_rows)
        ys = _experts(bexp, bvalid, nblk, xs, w_gate[l], w_up[l], w_down[l])
        piece_tok = n_tok // COMBINE_PIECES
        out = None
        for p in range(COMBINE_PIECES):
            tok = slice(p * piece_tok, (p + 1) * piece_tok)
            yg = _sc_gather(ys, jnp.concatenate([pos_a[tok], pos_b[tok]]))
            out = _combine(rw, x1.reshape(n_tok, d), mod3[:, 5:6, :], g_post_ffn[l].reshape(1, d), yg, s, p, out)
        x = out.reshape(bsz, s, d)
    return x
```

```python
import jax
import jax.numpy as jnp
from jax import lax
from jax.experimental import pallas as pl
from jax.experimental.pallas import tpu as pltpu
from jax.experimental.pallas import tpu_sc as plsc

D_MODEL = 1024
D_CONV = 512
D_REC = 512
CONV_WIDTH = 31
REC_HEADS = 4
REC_DK = 128
REC_DV = 128
CHUNK = 64
D_IN = 2 * D_CONV + 4 * D_REC
N_GROUPS = 4
EXPERTS_PER_GROUP = 8
N_EXPERTS = 32
TOP_K = 2
D_EXPERT = 256
EPS = 1e-6

LANES = 128
SUBLANES = 8
SEQ_TILE = 512
CONV_ROWS = 64
ROW_STEP = 32
STAT_ROWS = 64
CONV_HALO = 32
GROUP_ROW0 = 32
EXPERT_ROWS = 512
COMBINE_PIECES = 4
EXPERT_SUB = 256
SC_WINDOW = 64
D_PACK = D_MODEL // 2
VMEM_LIMIT = 56 * 1024 * 1024

F32 = jnp.float32
BF16 = jnp.bfloat16
I32 = jnp.int32
U32 = jnp.uint32
HI = lax.Precision.HIGHEST


def _sigmoid(v):
    return 0.5 * jnp.tanh(0.5 * v) + 0.5


def _silu(v):
    return v * _sigmoid(v)


def _rms(v, gain):
    return v * lax.rsqrt(jnp.mean(v * v, axis=-1, keepdims=True) + EPS) * gain


def _pack_pair(lo, hi):
    lo_bits = lax.bitcast_convert_type(lo.astype(BF16).astype(F32), U32)
    hi_bits = lax.bitcast_convert_type(hi.astype(BF16).astype(F32), U32)
    return (lo_bits >> 16) | (hi_bits & jnp.uint32(0xFFFF0000))


def _unpack_pair(packed):
    lo = lax.bitcast_convert_type(packed << 16, F32)
    hi = lax.bitcast_convert_type(packed & jnp.uint32(0xFFFF0000), F32)
    return lo, hi


def _ada_kernel(c_ref, w_ref, b_ref, o_ref):
    cond = _silu(c_ref[...])
    o_ref[...] = jnp.dot(cond, w_ref[...], precision=HI, preferred_element_type=F32) + b_ref[...]


def _ada(c, w, b):
    bsz, d = c.shape
    n = w.shape[1]
    return pl.pallas_call(
        _ada_kernel,
        grid=(n // d,),
        in_specs=[
            pl.BlockSpec((bsz, d), lambda j: (0, 0)),
            pl.BlockSpec((d, d), lambda j: (0, j)),
            pl.BlockSpec((1, d), lambda j: (0, j)),
        ],
        out_specs=pl.BlockSpec((bsz, d), lambda j: (0, j)),
        out_shape=jax.ShapeDtypeStruct((bsz, n), F32),
        name="ada_mod",
    )(c, w, b.reshape(1, n))


def _mix_kernel(x_ref, mod_ref, gpre_ref, gpost_ref, gffn_ref, win_ref, dwk_ref, dwb_ref,
                lng_ref, lnb_ref, lbl_ref, rng_ref, wout_ref, wr_ref, br_ref, upper_ref,
                x1_ref, h2p_ref, ri_ref, rw_ref, cnt_ref,
                st1, st2, vec_s, hb_s, qf_s, sg_s, ubuf, cv_s, lfp_s, k_s, b_s, qd_s, ki_s, ke_s, v_s, a_s, o_s, yb,
                state, carry):
    b = pl.program_id(0)
    j = pl.program_id(1)
    t = SEQ_TILE
    n_chunks = t // CHUNK
    heads = range(REC_HEADS)

    @pl.when(j == 0)
    def _():
        ubuf[0:CONV_HALO, :] = jnp.zeros((CONV_HALO, D_CONV), F32)
        state[...] = jnp.zeros(state.shape, F32)

    @pl.when((j == 0) & (b == 0))
    def _():
        carry[...] = jnp.zeros(carry.shape, F32)

    sh1 = mod_ref[0, 0:1, :]
    sc1 = mod_ref[0, 1:2, :]
    gt1 = mod_ref[0, 2:3, :]
    sh2 = mod_ref[0, 3:4, :]
    sc2 = mod_ref[0, 4:5, :]

    def row_loop(rows_per_step, body):
        def step(ci, c_):
            body(pl.ds(pl.multiple_of(ci * rows_per_step, rows_per_step), rows_per_step))
            return c_
        lax.fori_loop(0, t // rows_per_step, step, 0)

    stat_blocks = [slice(r0, r0 + STAT_ROWS) for r0 in range(0, t, STAT_ROWS)]

    def head_cols(base, hd):
        return slice(base + hd * REC_DK, base + (hd + 1) * REC_DK)

    def lane_tiles(width):
        return [slice(l0, l0 + LANES) for l0 in range(0, width, LANES)]

    def put_stat(ref, rows, stat):
        ref[rows, :] = jnp.broadcast_to(stat, (STAT_ROWS, LANES))

    vec_s[0:1, :] = gpre_ref[...] * (1.0 + sc1)
    vec_s[1:2, :] = sh1
    vec_s[2:3, :] = gt1 * gpost_ref[...]
    vec_s[3:4, :] = gffn_ref[...] * (1.0 + sc2)
    vec_s[4:5, :] = sh2

    for rows in stat_blocks:
        xc = x_ref[0, rows, :]
        put_stat(st1, rows, lax.rsqrt(jnp.mean(xc * xc, axis=-1, keepdims=True) + EPS))

    def prenorm(rows):
        inv = st1[rows, :]
        for lanes in lane_tiles(D_MODEL):
            hb_s[rows, lanes] = (x_ref[0, rows, lanes] * inv * vec_s[0:1, lanes] + vec_s[1:2, lanes]).astype(BF16)

    row_loop(ROW_STEP, prenorm)
    q0 = 2 * D_CONV
    f0 = q0 + D_REC
    i0 = f0 + D_REC
    g0 = i0 + D_REC
    hb = hb_s[...]

    def proj(c0, width):
        return jnp.dot(hb, win_ref[:, c0:c0 + width], preferred_element_type=F32)

    conv_in = proj(0, 2 * D_CONV)
    ubuf[CONV_HALO:CONV_HALO + t, :] = conv_in[:, 0:D_CONV] * _sigmoid(conv_in[:, D_CONV:2 * D_CONV])

    lead = CONV_HALO - (CONV_WIDTH - 1)
    win_rows = CONV_ROWS + CONV_HALO

    def conv(rows):
        for lt in range(D_CONV // LANES):
            lanes = slice(lt * LANES, (lt + 1) * LANES)
            win = ubuf[pl.ds(rows.start, win_rows), lanes]
            acc = jnp.broadcast_to(dwb_ref[:, lanes], (CONV_ROWS, LANES))
            for res in range(SUBLANES):
                shifted = win if res == 0 else pltpu.roll(win, win_rows - res, axis=0)
                for al in range(0, win_rows - CONV_ROWS + 1, SUBLANES):
                    kk = al + res - lead
                    if 0 <= kk < CONV_WIDTH and al + CONV_ROWS + res <= win_rows:
                        acc = acc + shifted[al:al + CONV_ROWS] * dwk_ref[kk:kk + 1, lanes]
            cv_s[rows, lanes] = acc

    row_loop(CONV_ROWS, conv)
    ubuf[0:CONV_HALO, :] = ubuf[t:t + CONV_HALO, :]

    for rows in stat_blocks:
        acc = cv_s[rows, :]
        mu = jnp.mean(acc, axis=-1, keepdims=True)
        xc = acc - mu
        put_stat(st1, rows, mu)
        put_stat(st2, rows, lax.rsqrt(jnp.mean(xc * xc, axis=-1, keepdims=True) + EPS))

    def conv_norm(rows):
        mu = st1[rows, :]
        inv = st2[rows, :]
        for lanes in lane_tiles(D_CONV):
            yc = (cv_s[rows, lanes] - mu) * inv * lng_ref[:, lanes] + lnb_ref[:, lanes]
            yb[rows, lanes] = _silu(yc).astype(BF16)

    row_loop(ROW_STEP, conv_norm)

    l0 = lbl_ref[0:1, :]
    lmax = jnp.max(lbl_ref[...], axis=0, keepdims=True)
    lb = jnp.exp(l0 - lmax) / jnp.sum(jnp.exp(lbl_ref[...] - lmax), axis=0, keepdims=True)

    qf_s[...] = _silu(proj(q0, D_REC))
    forget = lb + (1.0 - lb) * _sigmoid(proj(f0, D_REC))
    k_s[...] = 1.0 - forget
    lf = jnp.log(forget)
    hi = lf.astype(BF16)
    rem = lf - hi.astype(F32)
    mid = rem.astype(BF16)
    lfp_s[0] = hi
    lfp_s[1] = mid
    lfp_s[2] = (rem - mid.astype(F32)).astype(BF16)
    v_s[...] = proj(i0, D_REC).astype(BF16)
    sg_s[...] = _silu(proj(g0, D_REC))

    row = lax.broadcasted_iota(I32, (CHUNK, CHUNK), 0)
    col = lax.broadcasted_iota(I32, (CHUNK, CHUNK), 1)
    causal = row >= col
    tri = jnp.where(causal, 1.0, 0.0).astype(BF16)
    for ci in range(n_chunks):
        rows = slice(ci * CHUNK, (ci + 1) * CHUNK)
        b_s[rows, :] = (jnp.dot(tri, lfp_s[0, rows, :], preferred_element_type=F32)
                        + jnp.dot(tri, lfp_s[1, rows, :], preferred_element_type=F32)
                        + jnp.dot(tri, lfp_s[2, rows, :], preferred_element_type=F32))

    def decays(ci, c_):
        rows = pl.ds(pl.multiple_of(ci * CHUNK, CHUNK), CHUNK)
        for hd in heads:
            cols = head_cols(0, hd)
            bcum = b_s[rows, cols]
            k_c = k_s[rows, cols]
            a_last = jnp.exp(bcum[CHUNK - 1:CHUNK, :])
            k_inv = k_c * jnp.exp(-bcum)
            qd_s[rows, cols] = (qf_s[rows, cols] * jnp.exp(bcum)).astype(BF16)
            ki_s[rows, cols] = k_inv.astype(BF16)
            ke_s[rows, cols] = (k_inv * a_last).astype(BF16)
            a_s[ci, :, cols] = a_last
        return c_

    lax.fori_loop(0, n_chunks, decays, 0)

    nt_dims = (((1,), (1,)), ((), ()))
    pairs = [(ci, hd) for ci in range(n_chunks) for hd in heads]

    def blk(ci, hd):
        return slice(ci * CHUNK, (ci + 1) * CHUNK), head_cols(0, hd)

    scores = {}
    for p in pairs:
        rows, cols = blk(*p)
        sc = lax.dot_general(qd_s[rows, cols], ki_s[rows, cols], nt_dims, preferred_element_type=F32)
        scores[p] = jnp.where(causal, sc, 0.0).astype(BF16)
    for p in pairs:
        rows, cols = blk(*p)
        o_s[rows, cols] = jnp.dot(scores[p], v_s[rows, cols], preferred_element_type=F32)
    upd = {}
    for p in pairs:
        rows, cols = blk(*p)
        v_t = v_s[rows, cols].astype(F32).T.astype(BF16)
        upd[p] = jnp.dot(v_t, ke_s[rows, cols], preferred_element_type=F32)
    prev = {}
    for hd in heads:
        st = state[hd]
        for ci in range(n_chunks):
            prev[(ci, hd)] = st.astype(BF16)
            st = st * a_s[ci, :, head_cols(0, hd)] + upd[(ci, hd)]
        state[hd] = st
    for p in pairs:
        rows, cols = blk(*p)
        o_s[rows, cols] += lax.dot_general(qd_s[rows, cols], prev[p], nt_dims, preferred_element_type=F32)

    for rows in stat_blocks:
        for hd in heads:
            o = _rms(o_s[rows, head_cols(0, hd)], rng_ref[...]) * sg_s[rows, head_cols(0, hd)]
            yb[rows, head_cols(D_CONV, hd)] = o.astype(BF16)

    y = jnp.dot(yb[...], wout_ref[...], preferred_element_type=F32)
    x1 = x_ref[0] + y * lax.rsqrt(jnp.mean(y * y, axis=-1, keepdims=True) + EPS) * vec_s[2:3, :]
    x1_ref[0] = x1
    h2v = x1 * lax.rsqrt(jnp.mean(x1 * x1, axis=-1, keepdims=True) + EPS) * vec_s[3:4, :] + vec_s[4:5, :]
    h2p_ref[0] = _pack_pair(h2v[:, 0:D_PACK], h2v[:, D_PACK:D_MODEL])

    h_hi = h2v.astype(BF16)
    h_lo = (h2v - h_hi.astype(F32)).astype(BF16)
    both = jnp.dot(h_hi, wr_ref[...], preferred_element_type=F32)
    logits = (both[:, 0:LANES] + both[:, LANES:2 * LANES]
              + jnp.dot(h_lo, wr_ref[:, 0:LANES], preferred_element_type=F32)) + br_ref[...]
    lt = logits.T
    neg = jnp.float32(-jnp.inf)
    r8 = lax.broadcasted_iota(I32, (SUBLANES, t), 0)
    gl = jnp.where(r8 < N_GROUPS, lt[GROUP_ROW0:GROUP_ROW0 + SUBLANES], neg)
    gmax = jnp.max(gl, axis=0, keepdims=True)
    gidx = jnp.min(jnp.where(gl == gmax, r8, SUBLANES), axis=0, keepdims=True)
    gprob = 1.0 / jnp.sum(jnp.exp(gl - gmax), axis=0, keepdims=True)
    re = lax.broadcasted_iota(I32, (N_EXPERTS, t), 0)
    el = jnp.where((re // EXPERTS_PER_GROUP) == gidx, lt[0:N_EXPERTS], neg)
    m1 = jnp.max(el, axis=0, keepdims=True)
    i1 = jnp.min(jnp.where(el == m1, re, N_EXPERTS), axis=0, keepdims=True)
    el2 = jnp.where(re == i1, neg, el)
    m2 = jnp.max(el2, axis=0, keepdims=True)
    i2 = jnp.min(jnp.where(el2 == m2, re, N_EXPERTS), axis=0, keepdims=True)
    r = jnp.exp(m2 - m1)
    w1 = gprob / (1.0 + r)
    w2 = gprob * r / (1.0 + r)
    hot1 = re == i1
    hot2 = re == i2
    hot = jnp.where(hot1 | hot2, 1.0, 0.0)
    prefix = jnp.dot(hot.astype(BF16), upper_ref[...], preferred_element_type=F32) + carry[...]
    rank1 = jnp.sum(jnp.where(hot1, prefix, 0.0), axis=0, keepdims=True)
    rank2 = jnp.sum(jnp.where(hot2, prefix, 0.0), axis=0, keepdims=True)
    carry[...] = carry[...] + jnp.sum(hot, axis=1, keepdims=True)
    zi = jnp.zeros((SUBLANES - 4, t), I32)
    ri_ref[0] = jnp.concatenate([i1, i2, rank1.astype(I32), rank2.astype(I32), zi], axis=0)
    rw_ref[0] = jnp.concatenate([w1, w2, jnp.zeros((SUBLANES - 2, t), F32)], axis=0)
    cnt_ref[...] = jnp.broadcast_to(carry[...], cnt_ref.shape)


def _mix(x, mod3, g_pre, g_post, g_ffn, w_in, dwk, dwb, lng, lnb, lbl, rng, w_out, w_r, b_r):
    bsz, s, d = x.shape
    t = SEQ_TILE
    nt = s // t
    tile = lambda b, j: (b, j, 0)
    rtile = lambda b, j: (b * nt + j, 0, 0)
    const2 = lambda b, j: (0, 0)

    def const_spec(shape):
        return pl.BlockSpec(shape, const2, pipeline_mode=pl.Buffered(1))

    upper = jnp.triu(jnp.ones((t, t), BF16), k=1)
    return pl.pallas_call(
        _mix_kernel,
        grid=(bsz, nt),
        in_specs=[
            pl.BlockSpec((1, t, d), tile),
            pl.BlockSpec((1, 6, d), lambda b, j: (b, 0, 0)),
            const_spec((1, d)),
            const_spec((1, d)),
            const_spec((1, d)),
            const_spec((d, D_IN)),
            const_spec((CONV_WIDTH, D_CONV)),
            const_spec((1, D_CONV)),
            const_spec((1, D_CONV)),
            const_spec((1, D_CONV)),
            const_spec((2, D_REC)),
            const_spec((1, REC_DV)),
            const_spec((d, d)),
            const_spec((d, 2 * LANES)),
            const_spec((1, LANES)),
            const_spec((t, t)),
        ],
        out_specs=[
            pl.BlockSpec((1, t, d), tile),
            pl.BlockSpec((1, t, D_PACK), tile),
            pl.BlockSpec((1, SUBLANES, t), rtile),
            pl.BlockSpec((1, SUBLANES, t), rtile),
            pl.BlockSpec((N_EXPERTS, LANES), const2),
        ],
        out_shape=[
            jax.ShapeDtypeStruct((bsz, s, d), F32),
            jax.ShapeDtypeStruct((bsz, s, D_PACK), U32),
            jax.ShapeDtypeStruct((bsz * nt, SUBLANES, t), I32),
            jax.ShapeDtypeStruct((bsz * nt, SUBLANES, t), F32),
            jax.ShapeDtypeStruct((N_EXPERTS, LANES), F32),
        ],
        scratch_shapes=[
            pltpu.VMEM((t, LANES), F32),
            pltpu.VMEM((t, LANES), F32),
            pltpu.VMEM((SUBLANES, d), F32),
            pltpu.VMEM((t, d), BF16),
            pltpu.VMEM((t, D_REC), F32),
            pltpu.VMEM((t, D_REC), F32),
            pltpu.VMEM((CONV_HALO + t, D_CONV), F32),
            pltpu.VMEM((t, D_CONV), F32),
            pltpu.VMEM((3, t, D_REC), BF16),
            pltpu.VMEM((t, D_REC), F32),
            pltpu.VMEM((t, D_REC), F32),
            pltpu.VMEM((t, D_REC), BF16),
            pltpu.VMEM((t, D_REC), BF16),
            pltpu.VMEM((t, D_REC), BF16),
            pltpu.VMEM((t, D_REC), BF16),
            pltpu.VMEM((t // CHUNK, 1, D_REC), F32),
            pltpu.VMEM((t, D_REC), F32),
            pltpu.VMEM((t, d), BF16),
            pltpu.VMEM((REC_HEADS, REC_DV, REC_DK), F32),
            pltpu.VMEM((N_EXPERTS, 1), F32),
        ],
        compiler_params=pltpu.CompilerParams(
            dimension_semantics=("arbitrary", "arbitrary"),
            vmem_limit_bytes=VMEM_LIMIT),
        name="mixer",
    )(x, mod3, g_pre, g_post, g_ffn, w_in, dwk, dwb, lng, lnb, lbl, rng, w_out, w_r, b_r, upper)


def _meta_kernel(cnt_ref, start_ref, bexp_ref, bvalid_ref, nblk_ref):
    shift = EXPERT_ROWS.bit_length() - 1
    n_blocks = bexp_ref.shape[0]

    def fill(e, blk0, cnt):
        def body(jb, c_):
            bexp_ref[jb] = e
            bvalid_ref[jb] = jnp.clip(cnt - ((jb - blk0) << shift), 0, EXPERT_ROWS)
            return c_
        return body

    def per_expert(e, blk0):
        nb = (cnt_ref[e] + (EXPERT_ROWS - 1)) >> shift
        start_ref[e] = blk0 << shift
        lax.fori_loop(blk0, blk0 + nb, fill(e, blk0, cnt_ref[e]), 0)
        return blk0 + nb

    used = lax.fori_loop(0, N_EXPERTS, per_expert, jnp.int32(0))
    lax.fori_loop(used, n_blocks, fill(N_EXPERTS - 1, used, 0), 0)
    nblk_ref[0] = used


def _meta(counts, n_blocks):
    smem = pl.BlockSpec(memory_space=pltpu.SMEM)
    return pl.pallas_call(
        _meta_kernel,
        in_specs=[smem],
        out_specs=[smem, smem, smem, smem],
        out_shape=[
            jax.ShapeDtypeStruct((N_EXPERTS,), I32),
            jax.ShapeDtypeStruct((n_blocks,), I32),
            jax.ShapeDtypeStruct((n_blocks,), I32),
            jax.ShapeDtypeStruct((1,), I32),
        ],
        name="moe_layout",
    )(counts)


def _positions_kernel(start_ref, ri_ref, pos_ref):
    e = ri_ref[:, 0:TOP_K, :]
    seg = jnp.zeros(e.shape, I32)
    for k in range(N_EXPERTS):
        seg = jnp.where(e == k, start_ref[k], seg)
    pos = seg + ri_ref[:, TOP_K:2 * TOP_K, :]
    pad = jnp.zeros((e.shape[0], SUBLANES - TOP_K, e.shape[2]), I32)
    pos_ref[...] = jnp.concatenate([pos, pad], axis=1)


def _positions(starts, ri):
    n_tiles, _, t = ri.shape
    tb = SUBLANES
    return pl.pallas_call(
        _positions_kernel,
        grid_spec=pltpu.PrefetchScalarGridSpec(
            num_scalar_prefetch=1,
            grid=(n_tiles // tb,),
            in_specs=[pl.BlockSpec((tb, SUBLANES, t), lambda i, s_: (i, 0, 0))],
            out_specs=pl.BlockSpec((tb, SUBLANES, t), lambda i, s_: (i, 0, 0)),
        ),
        out_shape=jax.ShapeDtypeStruct(ri.shape, I32),
        name="moe_positions",
    )(starts, ri)


_SC_MESH = dict(core_axis_name="core", subcore_axis_name="subcore")
SC_WORKERS = 32
SC_INDEX_CHUNK = 128


def _sc_worker_base(rows_per_worker):
    wid = lax.axis_index("core") * (SC_WORKERS // 2) + lax.axis_index("subcore")
    return wid * rows_per_worker


def _sc_dispatch(rows_in, pos_a, pos_b, n_rows):
    n, d = rows_in.shape
    w = SC_WINDOW
    per_worker = n // SC_WORKERS
    chunks = per_worker // SC_INDEX_CHUNK
    windows = SC_INDEX_CHUNK // w
    dma = pltpu.SemaphoreType.DMA

    @pl.kernel(out_type=jax.ShapeDtypeStruct((n_rows, d), rows_in.dtype),
               mesh=plsc.VectorSubcoreMesh(**_SC_MESH),
               scratch_types=[pltpu.VMEM((SC_INDEX_CHUNK,), I32), pltpu.VMEM((SC_INDEX_CHUNK,), I32)]
               + [pltpu.VMEM((w, d), rows_in.dtype)] * windows + [dma] * (3 * windows),
               name="moe_dispatch_sc")
    def run(x_hbm, ia_hbm, ib_hbm, o_hbm, ia_v, ib_v, *rest):
        bufs, sems = rest[:windows], rest[windows:]
        base = _sc_worker_base(per_worker)

        @pl.loop(0, chunks)
        def _(c):
            row0 = base + c * SC_INDEX_CHUNK
            loads = [pltpu.make_async_copy(x_hbm.at[pl.ds(row0 + k * w, w)], bufs[k], sems[3 * k])
                     for k in range(windows)]
            for cp in loads:
                cp.start()
            pltpu.sync_copy(ia_hbm.at[pl.ds(row0, SC_INDEX_CHUNK)], ia_v)
            pltpu.sync_copy(ib_hbm.at[pl.ds(row0, SC_INDEX_CHUNK)], ib_v)
            stores = []
            for k in range(windows):
                loads[k].wait()
                for idx_v, sem in ((ia_v, sems[3 * k + 1]), (ib_v, sems[3 * k + 2])):
                    cp = pltpu.make_async_copy(bufs[k], o_hbm.at[idx_v.at[pl.ds(k * w, w)]], sem)
                    cp.start()
                    stores.append(cp)
            for cp in stores:
                cp.wait()

    return run(rows_in, pos_a, pos_b)


def _sc_gather(table, idx):
    m = idx.shape[0]
    d = table.shape[1]
    w = SC_WINDOW
    per_worker = m // SC_WORKERS
    chunks = per_worker // SC_INDEX_CHUNK
    windows = SC_INDEX_CHUNK // w
    dma = pltpu.SemaphoreType.DMA

    @pl.kernel(out_type=jax.ShapeDtypeStruct((m, d), table.dtype),
               mesh=plsc.VectorSubcoreMesh(**_SC_MESH),
               scratch_types=[pltpu.VMEM((SC_INDEX_CHUNK,), I32)]
               + [pltpu.VMEM((w, d), table.dtype)] * windows + [dma] * (2 * windows),
               name="moe_gather_sc")
    def run(x_hbm, i_hbm, o_hbm, i_v, *rest):
        bufs, sems = rest[:windows], rest[windows:]
        base = _sc_worker_base(per_worker)

        @pl.loop(0, chunks)
        def _(c):
            row0 = base + c * SC_INDEX_CHUNK
            pltpu.sync_copy(i_hbm.at[pl.ds(row0, SC_INDEX_CHUNK)], i_v)
            gathers = [pltpu.make_async_copy(x_hbm.at[i_v.at[pl.ds(k * w, w)]], bufs[k], sems[2 * k])
                       for k in range(windows)]
            for cp in gathers:
                cp.start()
            stores = []
            for k in range(windows):
                gathers[k].wait()
                cp = pltpu.make_async_copy(bufs[k], o_hbm.at[pl.ds(row0 + k * w, w)], sems[2 * k + 1])
                cp.start()
                stores.append(cp)
            for cp in stores:
                cp.wait()

    return run(table, idx)


def _expert_kernel(bexp_ref, bvalid_ref, nblk_ref, x_ref, wg_ref, wu_ref, wd_ref, y_ref, wg_s, wu_s, wd_s):
    jb = pl.program_id(0)
    valid = bvalid_ref[jb]

    @pl.when((jb == 0) | (bexp_ref[jb] != bexp_ref[jnp.maximum(jb - 1, 0)]))
    def _():
        wg_s[...] = wg_ref[0].astype(BF16)
        wu_s[...] = wu_ref[0].astype(BF16)
        wd_s[...] = wd_ref[0].astype(BF16)

    subs = [slice(r0, r0 + EXPERT_SUB) for r0 in range(0, EXPERT_ROWS, EXPERT_SUB)]
    passes = (valid + (EXPERT_SUB - 1)) // EXPERT_SUB

    def run(n_live):
        halves = []
        for rows in subs[:n_live]:
            live = lax.broadcasted_iota(I32, (EXPERT_SUB, 1), 0) + rows.start < valid
            lo, hi = _unpack_pair(jnp.where(live, x_ref[rows, :], jnp.uint32(0)))
            halves.append((lo.astype(BF16), hi.astype(BF16)))

        def project(w_s):
            return [jnp.dot(lo, w_s[0:D_PACK, :], preferred_element_type=F32)
                    + jnp.dot(hi, w_s[D_PACK:D_MODEL, :], preferred_element_type=F32) for lo, hi in halves]

        hidden = [(_silu(g) * u).astype(BF16) for g, u in zip(project(wg_s), project(wu_s))]
        outs = [jnp.dot(hb, wd_s[...], preferred_element_type=F32) for hb in hidden]
        for rows, y in zip(subs, outs):
            y_ref[rows, :] = _pack_pair(y[:, 0:D_PACK], y[:, D_PACK:D_MODEL])
        for rows in subs[n_live:]:
            y_ref[rows, :] = jnp.zeros((EXPERT_SUB, D_PACK), U32)

    for n_live in range(len(subs) + 1):
        pl.when(passes == n_live)(lambda n_live=n_live: run(n_live))


def _experts(bexp, bvalid, nblk, xs, w_gate, w_up, w_down):
    n_rows, dp = xs.shape
    d = 2 * dp
    n_blocks = n_rows // EXPERT_ROWS
    return pl.pallas_call(
        _expert_kernel,
        grid_spec=pltpu.PrefetchScalarGridSpec(
            num_scalar_prefetch=3,
            grid=(n_blocks,),
            in_specs=[
                pl.BlockSpec((EXPERT_ROWS, dp), lambda jb, be, bv, nb: (jnp.minimum(jb, nb[0] - 1), 0)),
                pl.BlockSpec((1, d, D_EXPERT), lambda jb, be, bv, nb: (be[jb], 0, 0)),
                pl.BlockSpec((1, d, D_EXPERT), lambda jb, be, bv, nb: (be[jb], 0, 0)),
                pl.BlockSpec((1, D_EXPERT, d), lambda jb, be, bv, nb: (be[jb], 0, 0)),
            ],
            out_specs=pl.BlockSpec((EXPERT_ROWS, dp), lambda jb, be, bv, nb: (jb, 0)),
            scratch_shapes=[
                pltpu.VMEM((d, D_EXPERT), BF16),
                pltpu.VMEM((d, D_EXPERT), BF16),
                pltpu.VMEM((D_EXPERT, d), BF16),
            ],
        ),
        out_shape=jax.ShapeDtypeStruct((n_rows, dp), U32),
        compiler_params=pltpu.CompilerParams(dimension_semantics=("arbitrary",)),
        name="moe_experts",
    )(bexp, bvalid, nblk, xs, w_gate, w_up, w_down)


def _combine_kernel(rw_ref, x1_ref, gt2_ref, gpost_ref, ya_ref, yb_ref, o_ref):
    wt = rw_ref[0].T
    a_lo, a_hi = _unpack_pair(ya_ref[...])
    b_lo, b_hi = _unpack_pair(yb_ref[...])
    y_lo = a_lo * wt[:, 0:1] + b_lo * wt[:, 1:2]
    y_hi = a_hi * wt[:, 0:1] + b_hi * wt[:, 1:2]
    ssq = jnp.sum(y_lo * y_lo, axis=-1, keepdims=True) + jnp.sum(y_hi * y_hi, axis=-1, keepdims=True)
    inv = lax.rsqrt(ssq * (1.0 / D_MODEL) + EPS)
    gate = gt2_ref[0] * gpost_ref[...]
    o_ref[:, 0:D_PACK] = x1_ref[:, 0:D_PACK] + y_lo * inv * gate[:, 0:D_PACK]
    o_ref[:, D_PACK:D_MODEL] = x1_ref[:, D_PACK:D_MODEL] + y_hi * inv * gate[:, D_PACK:D_MODEL]


def _combine_kernel_into(prev_ref, *refs):
    del prev_ref
    _combine_kernel(*refs)


def _combine(rw, x1, gt2, g_post, yg, seq, piece, prev):
    n, d = x1.shape
    t = SEQ_TILE
    per_batch = seq // t
    tiles = yg.shape[0] // (TOP_K * t)
    tile0 = piece * tiles
    in_specs = [
        pl.BlockSpec((1, SUBLANES, t), lambda i: (i + tile0, 0, 0)),
        pl.BlockSpec((t, d), lambda i: (i + tile0, 0)),
        pl.BlockSpec((1, 1, d), lambda i: ((i + tile0) // per_batch, 0, 0)),
        pl.BlockSpec((1, d), lambda i: (0, 0)),
        pl.BlockSpec((t, D_PACK), lambda i: (i, 0)),
        pl.BlockSpec((t, D_PACK), lambda i: (i + tiles, 0)),
    ]
    args = (rw, x1, gt2, g_post, yg, yg)
    body, aliases = _combine_kernel, {}
    if prev is not None:
        in_specs = [pl.BlockSpec(memory_space=pl.ANY)] + in_specs
        args = (prev,) + args
        body, aliases = _combine_kernel_into, {0: 0}
    return pl.pallas_call(
        body,
        grid=(tiles,),
        in_specs=in_specs,
        out_specs=pl.BlockSpec((t, d), lambda i: (i + tile0, 0)),
        out_shape=jax.ShapeDtypeStruct((n, d), F32),
        input_output_aliases=aliases,
        compiler_params=pltpu.CompilerParams(dimension_semantics=("arbitrary",)),
        name="moe_combine",
    )(*args)


def kernel(x, c, w_ada, b_ada, g_pre_mix, g_post_mix, w_in, dw_kernel, dw_bias, conv_ln_gain, conv_ln_bias, lb_logits, rec_norm_gain, w_out, g_pre_ffn, g_post_ffn, w_router_group, b_router_group, w_router_expert, b_router_expert, w_gate, w_up, w_down):
    bsz, s, d = x.shape
    depth = w_ada.shape[0]
    assert depth == 1 and lb_logits.shape[0] == 2
    n_tok = bsz * s
    n_rows = n_tok * TOP_K + N_EXPERTS * EXPERT_ROWS
    for l in range(depth):
        mod = _ada(c, w_ada[l], b_ada[l])
        mod3 = mod.reshape(bsz, 6, d)
        pad = LANES - N_EXPERTS - N_GROUPS
        w_r = jnp.concatenate([w_router_expert[l], w_router_group[l], jnp.zeros((d, pad), F32)], axis=1)
        w_r_hi = w_r.astype(BF16)
        w_r = jnp.concatenate([w_r_hi, (w_r - w_r_hi.astype(F32)).astype(BF16)], axis=1)
        b_r = jnp.concatenate([b_router_expert[l], b_router_group[l], jnp.zeros((pad,), F32)]).reshape(1, LANES)
        x1, h2p, ri, rw, cnt = _mix(
            x, mod3, g_pre_mix[l].reshape(1, d), g_post_mix[l].reshape(1, d), g_pre_ffn[l].reshape(1, d),
            w_in[l].astype(BF16), dw_kernel[l], dw_bias[l].reshape(1, D_CONV),
            conv_ln_gain[l].reshape(1, D_CONV), conv_ln_bias[l].reshape(1, D_CONV),
            lb_logits, rec_norm_gain[l].reshape(1, REC_DV), w_out[l].astype(BF16), w_r, b_r)
        counts = cnt[:, 0].astype(I32)
        starts, bexp, bvalid, nblk = _meta(counts, n_rows // EXPERT_ROWS)
        pos = _positions(starts, ri)
        pos_a = pos[:, 0, :].reshape(n_tok)
        pos_b = pos[:, 1, :].reshape(n_tok)
        xs = _sc_dispatch(h2p.reshape(n_tok, D_PACK), pos_a, pos_b, n_rows)
        ys = _experts(bexp, bvalid, nblk, xs, w_gate[l], w_up[l], w_down[l])
        piece_tok = n_tok // COMBINE_PIECES
        out = None
        for p in range(COMBINE_PIECES):
            tok = slice(p * piece_tok, (p + 1) * piece_tok)
            yg = _sc_gather(ys, jnp.concatenate([pos_a[tok], pos_b[tok]]))
            out = _combine(rw, x1.reshape(n_tok, d), mod3[:, 5:6, :], g_post_ffn[l].reshape(1, d), yg, s, p, out)
        x = out.reshape(bsz, s, d)
    return x
```

```python
import jax
import jax.numpy as jnp
from jax import lax
from jax.experimental import pallas as pl
from jax.experimental.pallas import tpu as pltpu
from jax.experimental.pallas import tpu_sc as plsc

D_MODEL = 1024
D_CONV = 512
D_REC = 512
CONV_WIDTH = 31
REC_HEADS = 4
REC_DK = 128
REC_DV = 128
CHUNK = 64
D_IN = 2 * D_CONV + 4 * D_REC
N_GROUPS = 4
EXPERTS_PER_GROUP = 8
N_EXPERTS = 32
TOP_K = 2
D_EXPERT = 256
EPS = 1e-6

LANES = 128
SUBLANES = 8
SEQ_TILE = 512
CONV_ROWS = 64
STAT_ROWS = 64
CONV_HALO = 32
GROUP_ROW0 = 32
EXPERT_ROWS = 512
COMBINE_PIECES = 4
EXPERT_SUB = 256
SC_WINDOW = 64
D_PACK = D_MODEL // 2
VMEM_LIMIT = 56 * 1024 * 1024

F32 = jnp.float32
BF16 = jnp.bfloat16
I32 = jnp.int32
U32 = jnp.uint32
HI = lax.Precision.HIGHEST


def _sigmoid(v):
    return 0.5 * jnp.tanh(0.5 * v) + 0.5


def _silu(v):
    return v * _sigmoid(v)


def _rms(v, gain):
    return v * lax.rsqrt(jnp.mean(v * v, axis=-1, keepdims=True) + EPS) * gain


def _pack_pair(lo, hi):
    lo_bits = lax.bitcast_convert_type(lo.astype(BF16).astype(F32), U32)
    hi_bits = lax.bitcast_convert_type(hi.astype(BF16).astype(F32), U32)
    return (lo_bits >> 16) | (hi_bits & jnp.uint32(0xFFFF0000))


def _unpack_pair(packed):
    lo = lax.bitcast_convert_type(packed << 16, F32)
    hi = lax.bitcast_convert_type(packed & jnp.uint32(0xFFFF0000), F32)
    return lo, hi


def _ada_kernel(c_ref, w_ref, b_ref, o_ref):
    cond = _silu(c_ref[...])
    o_ref[...] = jnp.dot(cond, w_ref[...], precision=HI, preferred_element_type=F32) + b_ref[...]


def _ada(c, w, b):
    bsz, d = c.shape
    n = w.shape[1]
    return pl.pallas_call(
        _ada_kernel,
        grid=(n // d,),
        in_specs=[
            pl.BlockSpec((bsz, d), lambda j: (0, 0)),
            pl.BlockSpec((d, d), lambda j: (0, j)),
            pl.BlockSpec((1, d), lambda j: (0, j)),
        ],
        out_specs=pl.BlockSpec((bsz, d), lambda j: (0, j)),
        out_shape=jax.ShapeDtypeStruct((bsz, n), F32),
        name="ada_mod",
    )(c, w, b.reshape(1, n))


def _mix_kernel(x_ref, mod_ref, gpre_ref, gpost_ref, gffn_ref, win_ref, dwk_ref, dwb_ref,
                lng_ref, lnb_ref, lbl_ref, rng_ref, wout_ref, wr_ref, br_ref, upper_ref,
                x1_ref, h2p_ref, ri_ref, rw_ref, cnt_ref,
                vec_s, qf_s, sg_s, ubuf, cv_s, lfp_s, k_s, qd_s, ki_s, ke_s, v_s, a_s, o_s, yb,
                state, carry):
    b = pl.program_id(0)
    j = pl.program_id(1)
    t = SEQ_TILE
    n_chunks = t // CHUNK
    heads = range(REC_HEADS)

    @pl.when(j == 0)
    def _():
        ubuf[0:CONV_HALO, :] = jnp.zeros((CONV_HALO, D_CONV), F32)
        state[...] = jnp.zeros(state.shape, F32)

    @pl.when((j == 0) & (b == 0))
    def _():
        carry[...] = jnp.zeros(carry.shape, F32)

    sh1 = mod_ref[0, 0:1, :]
    sc1 = mod_ref[0, 1:2, :]
    gt1 = mod_ref[0, 2:3, :]
    sh2 = mod_ref[0, 3:4, :]
    sc2 = mod_ref[0, 4:5, :]

    def row_loop(rows_per_step, body):
        def step(ci, c_):
            body(pl.ds(pl.multiple_of(ci * rows_per_step, rows_per_step), rows_per_step))
            return c_
        lax.fori_loop(0, t // rows_per_step, step, 0)

    stat_blocks = [slice(r0, r0 + STAT_ROWS) for r0 in range(0, t, STAT_ROWS)]

    def head_cols(base, hd):
        return slice(base + hd * REC_DK, base + (hd + 1) * REC_DK)

    vec_s[0:1, :] = gpre_ref[...] * (1.0 + sc1)
    vec_s[1:2, :] = sh1
    vec_s[2:3, :] = gt1 * gpost_ref[...]
    vec_s[3:4, :] = gffn_ref[...] * (1.0 + sc2)
    vec_s[4:5, :] = sh2

    xv = x_ref[0]
    hb = (xv * lax.rsqrt(jnp.mean(xv * xv, axis=-1, keepdims=True) + EPS) * vec_s[0:1, :] + vec_s[1:2, :]).astype(BF16)
    q0 = 2 * D_CONV
    f0 = q0 + D_REC
    i0 = f0 + D_REC
    g0 = i0 + D_REC

    def proj(c0, width):
        return jnp.dot(hb, win_ref[:, c0:c0 + width], preferred_element_type=F32)

    conv_in = proj(0, 2 * D_CONV)
    ubuf[CONV_HALO:CONV_HALO + t, :] = conv_in[:, 0:D_CONV] * _sigmoid(conv_in[:, D_CONV:2 * D_CONV])

    lead = CONV_HALO - (CONV_WIDTH - 1)
    win_rows = CONV_ROWS + CONV_HALO

    def conv(rows):
        for lt in range(D_CONV // LANES):
            lanes = slice(lt * LANES, (lt + 1) * LANES)
            win = ubuf[pl.ds(rows.start, win_rows), lanes]
            acc = jnp.broadcast_to(dwb_ref[:, lanes], (CONV_ROWS, LANES))
            for res in range(SUBLANES):
                shifted = win if res == 0 else pltpu.roll(win, win_rows - res, axis=0)
                for al in range(0, win_rows - CONV_ROWS + 1, SUBLANES):
                    kk = al + res - lead
                    if 0 <= kk < CONV_WIDTH and al + CONV_ROWS + res <= win_rows:
                        acc = acc + shifted[al:al + CONV_ROWS] * dwk_ref[kk:kk + 1, lanes]
            cv_s[rows, lanes] = acc

    row_loop(CONV_ROWS, conv)
    ubuf[0:CONV_HALO, :] = ubuf[t:t + CONV_HALO, :]

    conv_out = cv_s[...]
    centred = conv_out - jnp.mean(conv_out, axis=-1, keepdims=True)
    normed = centred * lax.rsqrt(jnp.mean(centred * centred, axis=-1, keepdims=True) + EPS)
    yb[:, 0:D_CONV] = _silu(normed * lng_ref[...] + lnb_ref[...]).astype(BF16)

    l0 = lbl_ref[0:1, :]
    lmax = jnp.max(lbl_ref[...], axis=0, keepdims=True)
    lb = jnp.exp(l0 - lmax) / jnp.sum(jnp.exp(lbl_ref[...] - lmax), axis=0, keepdims=True)

    qf_s[...] = _silu(proj(q0, D_REC))
    forget = lb + (1.0 - lb) * _sigmoid(proj(f0, D_REC))
    k_s[...] = 1.0 - forget
    lf = jnp.log(forget)
    hi = lf.astype(BF16)
    rem = lf - hi.astype(F32)
    mid = rem.astype(BF16)
    lfp_s[0] = hi
    lfp_s[1] = mid
    lfp_s[2] = (rem - mid.astype(F32)).astype(BF16)
    v_s[...] = proj(i0, D_REC).astype(BF16)
    sg_s[...] = _silu(proj(g0, D_REC))

    row = lax.broadcasted_iota(I32, (CHUNK, CHUNK), 0)
    col = lax.broadcasted_iota(I32, (CHUNK, CHUNK), 1)
    causal = row >= col
    tri = jnp.where(causal, 1.0, 0.0).astype(BF16)
    for ci in range(n_chunks):
        rows = slice(ci * CHUNK, (ci + 1) * CHUNK)
        bcum = (jnp.dot(tri, lfp_s[0, rows, :], preferred_element_type=F32)
                + jnp.dot(tri, lfp_s[1, rows, :], preferred_element_type=F32)
                + jnp.dot(tri, lfp_s[2, rows, :], preferred_element_type=F32))
        a_last = jnp.exp(bcum[CHUNK - 1:CHUNK, :])
        k_inv = k_s[rows, :] * jnp.exp(-bcum)
        qd_s[rows, :] = (qf_s[rows, :] * jnp.exp(bcum)).astype(BF16)
        ki_s[rows, :] = k_inv.astype(BF16)
        ke_s[rows, :] = (k_inv * a_last).astype(BF16)
        a_s[ci] = a_last

    nt_dims = (((1,), (1,)), ((), ()))
    pairs = [(ci, hd) for ci in range(n_chunks) for hd in heads]

    def blk(ci, hd):
        return slice(ci * CHUNK, (ci + 1) * CHUNK), head_cols(0, hd)

    scores = {}
    for p in pairs:
        rows, cols = blk(*p)
        sc = lax.dot_general(qd_s[rows, cols], ki_s[rows, cols], nt_dims, preferred_element_type=F32)
        scores[p] = jnp.where(causal, sc, 0.0).astype(BF16)
    for p in pairs:
        rows, cols = blk(*p)
        o_s[rows, cols] = jnp.dot(scores[p], v_s[rows, cols], preferred_element_type=F32)
    upd = {}
    for p in pairs:
        rows, cols = blk(*p)
        v_t = v_s[rows, cols].astype(F32).T.astype(BF16)
        upd[p] = jnp.dot(v_t, ke_s[rows, cols], preferred_element_type=F32)
    prev = {}
    for hd in heads:
        st = state[hd]
        for ci in range(n_chunks):
            prev[(ci, hd)] = st.astype(BF16)
            st = st * a_s[ci, :, head_cols(0, hd)] + upd[(ci, hd)]
        state[hd] = st
    for p in pairs:
        rows, cols = blk(*p)
        o_s[rows, cols] += lax.dot_general(qd_s[rows, cols], prev[p], nt_dims, preferred_element_type=F32)

    for rows in stat_blocks:
        for hd in heads:
            o = _rms(o_s[rows, head_cols(0, hd)], rng_ref[...]) * sg_s[rows, head_cols(0, hd)]
            yb[rows, head_cols(D_CONV, hd)] = o.astype(BF16)

    y = jnp.dot(yb[...], wout_ref[...], preferred_element_type=F32)
    x1 = x_ref[0] + y * lax.rsqrt(jnp.mean(y * y, axis=-1, keepdims=True) + EPS) * vec_s[2:3, :]
    x1_ref[0] = x1
    h2v = x1 * lax.rsqrt(jnp.mean(x1 * x1, axis=-1, keepdims=True) + EPS) * vec_s[3:4, :] + vec_s[4:5, :]
    h2p_ref[0] = _pack_pair(h2v[:, 0:D_PACK], h2v[:, D_PACK:D_MODEL])

    h_hi = h2v.astype(BF16)
    h_lo = (h2v - h_hi.astype(F32)).astype(BF16)
    both = jnp.dot(h_hi, wr_ref[...], preferred_element_type=F32)
    logits = (both[:, 0:LANES] + both[:, LANES:2 * LANES]
              + jnp.dot(h_lo, wr_ref[:, 0:LANES], preferred_element_type=F32)) + br_ref[...]
    lt = logits.T
    neg = jnp.float32(-jnp.inf)
    r8 = lax.broadcasted_iota(I32, (SUBLANES, t), 0)
    gl = jnp.where(r8 < N_GROUPS, lt[GROUP_ROW0:GROUP_ROW0 + SUBLANES], neg)
    gmax = jnp.max(gl, axis=0, keepdims=True)
    gidx = jnp.min(jnp.where(gl == gmax, r8, SUBLANES), axis=0, keepdims=True)
    gprob = 1.0 / jnp.sum(jnp.exp(gl - gmax), axis=0, keepdims=True)
    re = lax.broadcasted_iota(I32, (N_EXPERTS, t), 0)
    el = jnp.where((re // EXPERTS_PER_GROUP) == gidx, lt[0:N_EXPERTS], neg)
    m1 = jnp.max(el, axis=0, keepdims=True)
    i1 = jnp.min(jnp.where(el == m1, re, N_EXPERTS), axis=0, keepdims=True)
    el2 = jnp.where(re == i1, neg, el)
    m2 = jnp.max(el2, axis=0, keepdims=True)
    i2 = jnp.min(jnp.where(el2 == m2, re, N_EXPERTS), axis=0, keepdims=True)
    r = jnp.exp(m2 - m1)
    w1 = gprob / (1.0 + r)
    w2 = gprob * r / (1.0 + r)
    hot1 = re == i1
    hot2 = re == i2
    hot = jnp.where(hot1 | hot2, 1.0, 0.0)
    prefix = jnp.dot(hot.astype(BF16), upper_ref[...], preferred_element_type=F32) + carry[...]
    rank1 = jnp.sum(jnp.where(hot1, prefix, 0.0), axis=0, keepdims=True)
    rank2 = jnp.sum(jnp.where(hot2, prefix, 0.0), axis=0, keepdims=True)
    carry[...] = carry[...] + jnp.sum(hot, axis=1, keepdims=True)
    zi = jnp.zeros((SUBLANES - 4, t), I32)
    ri_ref[0] = jnp.concatenate([i1, i2, rank1.astype(I32), rank2.astype(I32), zi], axis=0)
    rw_ref[0] = jnp.concatenate([w1, w2, jnp.zeros((SUBLANES - 2, t), F32)], axis=0)
    cnt_ref[...] = jnp.broadcast_to(carry[...], cnt_ref.shape)


def _mix(x, mod3, g_pre, g_post, g_ffn, w_in, dwk, dwb, lng, lnb, lbl, rng, w_out, w_r, b_r):
    bsz, s, d = x.shape
    t = SEQ_TILE
    nt = s // t
    tile = lambda b, j: (b, j, 0)
    rtile = lambda b, j: (b * nt + j, 0, 0)
    const2 = lambda b, j: (0, 0)

    def const_spec(shape):
        return pl.BlockSpec(shape, const2, pipeline_mode=pl.Buffered(1))

    upper = jnp.triu(jnp.ones((t, t), BF16), k=1)
    return pl.pallas_call(
        _mix_kernel,
        grid=(bsz, nt),
        in_specs=[
            pl.BlockSpec((1, t, d), tile),
            pl.BlockSpec((1, 6, d), lambda b, j: (b, 0, 0)),
            const_spec((1, d)),
            const_spec((1, d)),
            const_spec((1, d)),
            const_spec((d, D_IN)),
            const_spec((CONV_WIDTH, D_CONV)),
            const_spec((1, D_CONV)),
            const_spec((1, D_CONV)),
            const_spec((1, D_CONV)),
            const_spec((2, D_REC)),
            const_spec((1, REC_DV)),
            const_spec((d, d)),
            const_spec((d, 2 * LANES)),
            const_spec((1, LANES)),
            const_spec((t, t)),
        ],
        out_specs=[
            pl.BlockSpec((1, t, d), tile),
            pl.BlockSpec((1, t, D_PACK), tile),
            pl.BlockSpec((1, SUBLANES, t), rtile),
            pl.BlockSpec((1, SUBLANES, t), rtile),
            pl.BlockSpec((N_EXPERTS, LANES), const2),
        ],
        out_shape=[
            jax.ShapeDtypeStruct((bsz, s, d), F32),
            jax.ShapeDtypeStruct((bsz, s, D_PACK), U32),
            jax.ShapeDtypeStruct((bsz * nt, SUBLANES, t), I32),
            jax.ShapeDtypeStruct((bsz * nt, SUBLANES, t), F32),
            jax.ShapeDtypeStruct((N_EXPERTS, LANES), F32),
        ],
        scratch_shapes=[
            pltpu.VMEM((SUBLANES, d), F32),
            pltpu.VMEM((t, D_REC), F32),
            pltpu.VMEM((t, D_REC), F32),
            pltpu.VMEM((CONV_HALO + t, D_CONV), F32),
            pltpu.VMEM((t, D_CONV), F32),
            pltpu.VMEM((3, t, D_REC), BF16),
            pltpu.VMEM((t, D_REC), F32),
            pltpu.VMEM((t, D_REC), BF16),
            pltpu.VMEM((t, D_REC), BF16),
            pltpu.VMEM((t, D_REC), BF16),
            pltpu.VMEM((t, D_REC), BF16),
            pltpu.VMEM((t // CHUNK, 1, D_REC), F32),
            pltpu.VMEM((t, D_REC), F32),
            pltpu.VMEM((t, d), BF16),
            pltpu.VMEM((REC_HEADS, REC_DV, REC_DK), F32),
            pltpu.VMEM((N_EXPERTS, 1), F32),
        ],
        compiler_params=pltpu.CompilerParams(
            dimension_semantics=("arbitrary", "arbitrary"),
            vmem_limit_bytes=VMEM_LIMIT),
        name="mixer",
    )(x, mod3, g_pre, g_post, g_ffn, w_in, dwk, dwb, lng, lnb, lbl, rng, w_out, w_r, b_r, upper)


def _meta_kernel(cnt_ref, start_ref, bexp_ref, bvalid_ref, nblk_ref):
    shift = EXPERT_ROWS.bit_length() - 1
    n_blocks = bexp_ref.shape[0]

    def fill(e, blk0, cnt):
        def body(jb, c_):
            bexp_ref[jb] = e
            bvalid_ref[jb] = jnp.clip(cnt - ((jb - blk0) << shift), 0, EXPERT_ROWS)
            return c_
        return body

    def per_expert(e, blk0):
        nb = (cnt_ref[e] + (EXPERT_ROWS - 1)) >> shift
        start_ref[e] = blk0 << shift
        lax.fori_loop(blk0, blk0 + nb, fill(e, blk0, cnt_ref[e]), 0)
        return blk0 + nb

    used = lax.fori_loop(0, N_EXPERTS, per_expert, jnp.int32(0))
    lax.fori_loop(used, n_blocks, fill(N_EXPERTS - 1, used, 0), 0)
    nblk_ref[0] = used


def _meta(counts, n_blocks):
    smem = pl.BlockSpec(memory_space=pltpu.SMEM)
    return pl.pallas_call(
        _meta_kernel,
        in_specs=[smem],
        out_specs=[smem, smem, smem, smem],
        out_shape=[
            jax.ShapeDtypeStruct((N_EXPERTS,), I32),
            jax.ShapeDtypeStruct((n_blocks,), I32),
            jax.ShapeDtypeStruct((n_blocks,), I32),
            jax.ShapeDtypeStruct((1,), I32),
        ],
        name="moe_layout",
    )(counts)


def _positions_kernel(start_ref, ri_ref, pos_ref):
    e = ri_ref[:, 0:TOP_K, :]
    seg = jnp.zeros(e.shape, I32)
    for k in range(N_EXPERTS):
        seg = jnp.where(e == k, start_ref[k], seg)
    pos = seg + ri_ref[:, TOP_K:2 * TOP_K, :]
    pad = jnp.zeros((e.shape[0], SUBLANES - TOP_K, e.shape[2]), I32)
    pos_ref[...] = jnp.concatenate([pos, pad], axis=1)


def _positions(starts, ri):
    n_tiles, _, t = ri.shape
    tb = SUBLANES
    return pl.pallas_call(
        _positions_kernel,
        grid_spec=pltpu.PrefetchScalarGridSpec(
            num_scalar_prefetch=1,
            grid=(n_tiles // tb,),
            in_specs=[pl.BlockSpec((tb, SUBLANES, t), lambda i, s_: (i, 0, 0))],
            out_specs=pl.BlockSpec((tb, SUBLANES, t), lambda i, s_: (i, 0, 0)),
        ),
        out_shape=jax.ShapeDtypeStruct(ri.shape, I32),
        name="moe_positions",
    )(starts, ri)


_SC_MESH = dict(core_axis_name="core", subcore_axis_name="subcore")
SC_WORKERS = 32
SC_INDEX_CHUNK = 128


def _sc_worker_base(rows_per_worker):
    wid = lax.axis_index("core") * (SC_WORKERS // 2) + lax.axis_index("subcore")
    return wid * rows_per_worker


def _sc_dispatch(rows_in, pos_a, pos_b, n_rows):
    n, d = rows_in.shape
    w = SC_WINDOW
    per_worker = n // SC_WORKERS
    chunks = per_worker // SC_INDEX_CHUNK
    windows = SC_INDEX_CHUNK // w
    dma = pltpu.SemaphoreType.DMA

    @pl.kernel(out_type=jax.ShapeDtypeStruct((n_rows, d), rows_in.dtype),
               mesh=plsc.VectorSubcoreMesh(**_SC_MESH),
               scratch_types=[pltpu.VMEM((SC_INDEX_CHUNK,), I32), pltpu.VMEM((SC_INDEX_CHUNK,), I32)]
               + [pltpu.VMEM((w, d), rows_in.dtype)] * windows + [dma] * (3 * windows),
               name="moe_dispatch_sc")
    def run(x_hbm, ia_hbm, ib_hbm, o_hbm, ia_v, ib_v, *rest):
        bufs, sems = rest[:windows], rest[windows:]
        base = _sc_worker_base(per_worker)

        @pl.loop(0, chunks)
        def _(c):
            row0 = base + c * SC_INDEX_CHUNK
            loads = [pltpu.make_async_copy(x_hbm.at[pl.ds(row0 + k * w, w)], bufs[k], sems[3 * k])
                     for k in range(windows)]
            for cp in loads:
                cp.start()
            pltpu.sync_copy(ia_hbm.at[pl.ds(row0, SC_INDEX_CHUNK)], ia_v)
            pltpu.sync_copy(ib_hbm.at[pl.ds(row0, SC_INDEX_CHUNK)], ib_v)
            stores = []
            for k in range(windows):
                loads[k].wait()
                for idx_v, sem in ((ia_v, sems[3 * k + 1]), (ib_v, sems[3 * k + 2])):
                    cp = pltpu.make_async_copy(bufs[k], o_hbm.at[idx_v.at[pl.ds(k * w, w)]], sem)
                    cp.start()
                    stores.append(cp)
            for cp in stores:
                cp.wait()

    return run(rows_in, pos_a, pos_b)


def _sc_gather(table, idx):
    m = idx.shape[0]
    d = table.shape[1]
    w = SC_WINDOW
    per_worker = m // SC_WORKERS
    chunks = per_worker // SC_INDEX_CHUNK
    windows = SC_INDEX_CHUNK // w
    dma = pltpu.SemaphoreType.DMA

    @pl.kernel(out_type=jax.ShapeDtypeStruct((m, d), table.dtype),
               mesh=plsc.VectorSubcoreMesh(**_SC_MESH),
               scratch_types=[pltpu.VMEM((SC_INDEX_CHUNK,), I32)]
               + [pltpu.VMEM((w, d), table.dtype)] * windows + [dma] * (2 * windows),
               name="moe_gather_sc")
    def run(x_hbm, i_hbm, o_hbm, i_v, *rest):
        bufs, sems = rest[:windows], rest[windows:]
        base = _sc_worker_base(per_worker)

        @pl.loop(0, chunks)
        def _(c):
            row0 = base + c * SC_INDEX_CHUNK
            pltpu.sync_copy(i_hbm.at[pl.ds(row0, SC_INDEX_CHUNK)], i_v)
            gathers = [pltpu.make_async_copy(x_hbm.at[i_v.at[pl.ds(k * w, w)]], bufs[k], sems[2 * k])
                       for k in range(windows)]
            for cp in gathers:
                cp.start()
            stores = []
            for k in range(windows):
                gathers[k].wait()
                cp = pltpu.make_async_copy(bufs[k], o_hbm.at[pl.ds(row0 + k * w, w)], sems[2 * k + 1])
                cp.start()
                stores.append(cp)
            for cp in stores:
                cp.wait()

    return run(table, idx)


def _expert_kernel(bexp_ref, bvalid_ref, nblk_ref, x_ref, wg_ref, wu_ref, wd_ref, y_ref, wg_s, wu_s, wd_s):
    jb = pl.program_id(0)
    valid = bvalid_ref[jb]

    @pl.when((jb == 0) | (bexp_ref[jb] != bexp_ref[jnp.maximum(jb - 1, 0)]))
    def _():
        wg_s[...] = wg_ref[0].astype(BF16)
        wu_s[...] = wu_ref[0].astype(BF16)
        wd_s[...] = wd_ref[0].astype(BF16)

    subs = [slice(r0, r0 + EXPERT_SUB) for r0 in range(0, EXPERT_ROWS, EXPERT_SUB)]
    passes = (valid + (EXPERT_SUB - 1)) // EXPERT_SUB

    def run(n_live):
        halves = []
        for rows in subs[:n_live]:
            live = lax.broadcasted_iota(I32, (EXPERT_SUB, 1), 0) + rows.start < valid
            lo, hi = _unpack_pair(jnp.where(live, x_ref[rows, :], jnp.uint32(0)))
            halves.append((lo.astype(BF16), hi.astype(BF16)))

        def project(w_s):
            return [jnp.dot(lo, w_s[0:D_PACK, :], preferred_element_type=F32)
                    + jnp.dot(hi, w_s[D_PACK:D_MODEL, :], preferred_element_type=F32) for lo, hi in halves]

        hidden = [(_silu(g) * u).astype(BF16) for g, u in zip(project(wg_s), project(wu_s))]
        outs = [jnp.dot(hb, wd_s[...], preferred_element_type=F32) for hb in hidden]
        for rows, y in zip(subs, outs):
            y_ref[rows, :] = _pack_pair(y[:, 0:D_PACK], y[:, D_PACK:D_MODEL])
        for rows in subs[n_live:]:
            y_ref[rows, :] = jnp.zeros((EXPERT_SUB, D_PACK), U32)

    for n_live in range(len(subs) + 1):
        pl.when(passes == n_live)(lambda n_live=n_live: run(n_live))


def _experts(bexp, bvalid, nblk, xs, w_gate, w_up, w_down):
    n_rows, dp = xs.shape
    d = 2 * dp
    n_blocks = n_rows // EXPERT_ROWS
    return pl.pallas_call(
        _expert_kernel,
        grid_spec=pltpu.PrefetchScalarGridSpec(
            num_scalar_prefetch=3,
            grid=(n_blocks,),
            in_specs=[
                pl.BlockSpec((EXPERT_ROWS, dp), lambda jb, be, bv, nb: (jnp.minimum(jb, nb[0] - 1), 0)),
                pl.BlockSpec((1, d, D_EXPERT), lambda jb, be, bv, nb: (be[jb], 0, 0)),
                pl.BlockSpec((1, d, D_EXPERT), lambda jb, be, bv, nb: (be[jb], 0, 0)),
                pl.BlockSpec((1, D_EXPERT, d), lambda jb, be, bv, nb: (be[jb], 0, 0)),
            ],
            out_specs=pl.BlockSpec((EXPERT_ROWS, dp), lambda jb, be, bv, nb: (jb, 0)),
            scratch_shapes=[
                pltpu.VMEM((d, D_EXPERT), BF16),
                pltpu.VMEM((d, D_EXPERT), BF16),
                pltpu.VMEM((D_EXPERT, d), BF16),
            ],
        ),
        out_shape=jax.ShapeDtypeStruct((n_rows, dp), U32),
        compiler_params=pltpu.CompilerParams(dimension_semantics=("arbitrary",)),
        name="moe_experts",
    )(bexp, bvalid, nblk, xs, w_gate, w_up, w_down)


def _combine_kernel(rw_ref, x1_ref, gt2_ref, gpost_ref, ya_ref, yb_ref, o_ref):
    wt = rw_ref[0].T
    a_lo, a_hi = _unpack_pair(ya_ref[...])
    b_lo, b_hi = _unpack_pair(yb_ref[...])
    y_lo = a_lo * wt[:, 0:1] + b_lo * wt[:, 1:2]
    y_hi = a_hi * wt[:, 0:1] + b_hi * wt[:, 1:2]
    ssq = jnp.sum(y_lo * y_lo, axis=-1, keepdims=True) + jnp.sum(y_hi * y_hi, axis=-1, keepdims=True)
    inv = lax.rsqrt(ssq * (1.0 / D_MODEL) + EPS)
    gate = gt2_ref[0] * gpost_ref[...]
    o_ref[:, 0:D_PACK] = x1_ref[:, 0:D_PACK] + y_lo * inv * gate[:, 0:D_PACK]
    o_ref[:, D_PACK:D_MODEL] = x1_ref[:, D_PACK:D_MODEL] + y_hi * inv * gate[:, D_PACK:D_MODEL]


def _combine_kernel_into(prev_ref, *refs):
    del prev_ref
    _combine_kernel(*refs)


def _combine(rw, x1, gt2, g_post, yg, seq, piece, prev):
    n, d = x1.shape
    t = SEQ_TILE
    per_batch = seq // t
    tiles = yg.shape[0] // (TOP_K * t)
    tile0 = piece * tiles
    in_specs = [
        pl.BlockSpec((1, SUBLANES, t), lambda i: (i + tile0, 0, 0)),
        pl.BlockSpec((t, d), lambda i: (i + tile0, 0)),
        pl.BlockSpec((1, 1, d), lambda i: ((i + tile0) // per_batch, 0, 0)),
        pl.BlockSpec((1, d), lambda i: (0, 0)),
        pl.BlockSpec((t, D_PACK), lambda i: (i, 0)),
        pl.BlockSpec((t, D_PACK), lambda i: (i + tiles, 0)),
    ]
    args = (rw, x1, gt2, g_post, yg, yg)
    body, aliases = _combine_kernel, {}
    if prev is not None:
        in_specs = [pl.BlockSpec(memory_space=pl.ANY)] + in_specs
        args = (prev,) + args
        body, aliases = _combine_kernel_into, {0: 0}
    return pl.pallas_call(
        body,
        grid=(tiles,),
        in_specs=in_specs,
        out_specs=pl.BlockSpec((t, d), lambda i: (i + tile0, 0)),
        out_shape=jax.ShapeDtypeStruct((n, d), F32),
        input_output_aliases=aliases,
        compiler_params=pltpu.CompilerParams(dimension_semantics=("arbitrary",)),
        name="moe_combine",
    )(*args)


def kernel(x, c, w_ada, b_ada, g_pre_mix, g_post_mix, w_in, dw_kernel, dw_bias, conv_ln_gain, conv_ln_bias, lb_logits, rec_norm_gain, w_out, g_pre_ffn, g_post_ffn, w_router_group, b_router_group, w_router_expert, b_router_expert, w_gate, w_up, w_down):
    bsz, s, d = x.shape
    depth = w_ada.shape[0]
    assert depth == 1 and lb_logits.shape[0] == 2
    n_tok = bsz * s
    n_rows = n_tok * TOP_K + N_EXPERTS * EXPERT_ROWS
    for l in range(depth):
        mod = _ada(c, w_ada[l], b_ada[l])
        mod3 = mod.reshape(bsz, 6, d)
        pad = LANES - N_EXPERTS - N_GROUPS
        w_r = jnp.concatenate([w_router_expert[l], w_router_group[l], jnp.zeros((d, pad), F32)], axis=1)
        w_r_hi = w_r.astype(BF16)
        w_r = jnp.concatenate([w_r_hi, (w_r - w_r_hi.astype(F32)).astype(BF16)], axis=1)
        b_r = jnp.concatenate([b_router_expert[l], b_router_group[l], jnp.zeros((pad,), F32)]).reshape(1, LANES)
        x1, h2p, ri, rw, cnt = _mix(
            x, mod3, g_pre_mix[l].reshape(1, d), g_post_mix[l].reshape(1, d), g_pre_ffn[l].reshape(1, d),
            w_in[l].astype(BF16), dw_kernel[l], dw_bias[l].reshape(1, D_CONV),
            conv_ln_gain[l].reshape(1, D_CONV), conv_ln_bias[l].reshape(1, D_CONV),
            lb_logits, rec_norm_gain[l].reshape(1, REC_DV), w_out[l].astype(BF16), w_r, b_r)
        counts = cnt[:, 0].astype(I32)
        starts, bexp, bvalid, nblk = _meta(counts, n_rows // EXPERT_ROWS)
        pos = _positions(starts, ri)
        pos_a = pos[:, 0, :].reshape(n_tok)
        pos_b = pos[:, 1, :].reshape(n_tok)
        xs = _sc_dispatch(h2p.reshape(n_tok, D_PACK), pos_a, pos_b, n_rows)
        ys = _experts(bexp, bvalid, nblk, xs, w_gate[l], w_up[l], w_down[l])
        piece_tok = n_tok // COMBINE_PIECES
        out = None
        for p in range(COMBINE_PIECES):
            tok = slice(p * piece_tok, (p + 1) * piece_tok)
            yg = _sc_gather(ys, jnp.concatenate([pos_a[tok], pos_b[tok]]))
            out = _combine(rw, x1.reshape(n_tok, d), mod3[:, 5:6, :], g_post_ffn[l].reshape(1, d), yg, s, p, out)
        x = out.reshape(bsz, s, d)
    return x
```

```python
import jax
import jax.numpy as jnp
from jax import lax
from jax.experimental import pallas as pl
from jax.experimental.pallas import tpu as pltpu
from jax.experimental.pallas import tpu_sc as plsc

D_MODEL = 1024
D_CONV = 512
D_REC = 512
CONV_WIDTH = 31
REC_HEADS = 4
REC_DK = 128
REC_DV = 128
CHUNK = 64
D_IN = 2 * D_CONV + 4 * D_REC
N_GROUPS = 4
EXPERTS_PER_GROUP = 8
N_EXPERTS = 32
TOP_K = 2
D_EXPERT = 256
EPS = 1e-6

LANES = 128
SUBLANES = 8
SEQ_TILE = 1024
CONV_ROWS = 64
STAT_ROWS = 64
CONV_HALO = 32
GROUP_ROW0 = 32
EXPERT_ROWS = 1024
COMBINE_PIECES = 4
EXPERT_SUB = 256
SC_WINDOW = 64
SC_WORKERS = 32
SC_INDEX_CHUNK = 128
D_PACK = D_MODEL // 2
VMEM_LIMIT = 56 * 1024 * 1024

F32 = jnp.float32
BF16 = jnp.bfloat16
I32 = jnp.int32
U32 = jnp.uint32
HI = lax.Precision.HIGHEST


def _sigmoid(v):
    return 0.5 * jnp.tanh(0.5 * v) + 0.5


def _silu(v):
    return v * _sigmoid(v)


def _rms(v, gain):
    return v * lax.rsqrt(jnp.mean(v * v, axis=-1, keepdims=True) + EPS) * gain


def _pack_pair(lo, hi):
    lo_bits = lax.bitcast_convert_type(lo.astype(BF16).astype(F32), U32)
    hi_bits = lax.bitcast_convert_type(hi.astype(BF16).astype(F32), U32)
    return (lo_bits >> 16) | (hi_bits & jnp.uint32(0xFFFF0000))


def _unpack_pair(packed):
    lo = lax.bitcast_convert_type(packed << 16, F32)
    hi = lax.bitcast_convert_type(packed & jnp.uint32(0xFFFF0000), F32)
    return lo, hi


def _ada_kernel(c_ref, w_ref, b_ref, o_ref):
    cond = _silu(c_ref[...])
    o_ref[...] = jnp.dot(cond, w_ref[...], precision=HI, preferred_element_type=F32) + b_ref[...]


def _ada(c, w, b):
    bsz, d = c.shape
    n = w.shape[1]
    return pl.pallas_call(
        _ada_kernel,
        grid=(n // d,),
        in_specs=[
            pl.BlockSpec((bsz, d), lambda j: (0, 0)),
            pl.BlockSpec((d, d), lambda j: (0, j)),
            pl.BlockSpec((1, d), lambda j: (0, j)),
        ],
        out_specs=pl.BlockSpec((bsz, d), lambda j: (0, j)),
        out_shape=jax.ShapeDtypeStruct((bsz, n), F32),
        name="ada_mod",
    )(c, w, b.reshape(1, n))


def _mix_kernel(x_ref, mod_ref, gpre_ref, gpost_ref, gffn_ref, win_ref, dwk_ref, dwb_ref,
                lng_ref, lnb_ref, lbl_ref, rng_ref, wout_ref, wr_ref, br_ref, upper_ref,
                x1_ref, h2p_ref, ri_ref, rw_ref, cnt_ref,
                vec_s, qf_s, sg_s, ubuf, cv_s, lfp_s, k_s, qd_s, ki_s, ke_s, v_s, a_s, o_s, yb,
                state, carry):
    b = pl.program_id(0)
    j = pl.program_id(1)
    t = SEQ_TILE
    n_chunks = t // CHUNK
    heads = range(REC_HEADS)

    @pl.when(j == 0)
    def _():
        ubuf[0:CONV_HALO, :] = jnp.zeros((CONV_HALO, D_CONV), F32)
        state[...] = jnp.zeros(state.shape, F32)

    @pl.when((j == 0) & (b == 0))
    def _():
        carry[...] = jnp.zeros(carry.shape, F32)

    sh1 = mod_ref[0, 0:1, :]
    sc1 = mod_ref[0, 1:2, :]
    gt1 = mod_ref[0, 2:3, :]
    sh2 = mod_ref[0, 3:4, :]
    sc2 = mod_ref[0, 4:5, :]

    def row_loop(rows_per_step, body):
        def step(ci, c_):
            body(pl.ds(pl.multiple_of(ci * rows_per_step, rows_per_step), rows_per_step))
            return c_
        lax.fori_loop(0, t // rows_per_step, step, 0)

    stat_blocks = [slice(r0, r0 + STAT_ROWS) for r0 in range(0, t, STAT_ROWS)]

    def head_cols(base, hd):
        return slice(base + hd * REC_DK, base + (hd + 1) * REC_DK)

    vec_s[0:1, :] = gpre_ref[...] * (1.0 + sc1)
    vec_s[1:2, :] = sh1
    vec_s[2:3, :] = gt1 * gpost_ref[...]
    vec_s[3:4, :] = gffn_ref[...] * (1.0 + sc2)
    vec_s[4:5, :] = sh2

    xv = x_ref[0]
    hb = (xv * lax.rsqrt(jnp.mean(xv * xv, axis=-1, keepdims=True) + EPS) * vec_s[0:1, :] + vec_s[1:2, :]).astype(BF16)
    q0 = 2 * D_CONV
    f0 = q0 + D_REC
    i0 = f0 + D_REC
    g0 = i0 + D_REC

    def proj(c0, width):
        return jnp.dot(hb, win_ref[:, c0:c0 + width], preferred_element_type=F32)

    conv_in = proj(0, 2 * D_CONV)
    ubuf[CONV_HALO:CONV_HALO + t, :] = conv_in[:, 0:D_CONV] * _sigmoid(conv_in[:, D_CONV:2 * D_CONV])

    lead = CONV_HALO - (CONV_WIDTH - 1)
    win_rows = CONV_ROWS + CONV_HALO

    def conv(rows):
        for lt in range(D_CONV // LANES):
            lanes = slice(lt * LANES, (lt + 1) * LANES)
            win = ubuf[pl.ds(rows.start, win_rows), lanes]
            acc = jnp.broadcast_to(dwb_ref[:, lanes], (CONV_ROWS, LANES))
            for res in range(SUBLANES):
                shifted = win if res == 0 else pltpu.roll(win, win_rows - res, axis=0)
                for al in range(0, win_rows - CONV_ROWS + 1, SUBLANES):
                    kk = al + res - lead
                    if 0 <= kk < CONV_WIDTH and al + CONV_ROWS + res <= win_rows:
                        acc = acc + shifted[al:al + CONV_ROWS] * dwk_ref[kk:kk + 1, lanes]
            cv_s[rows, lanes] = acc

    row_loop(CONV_ROWS, conv)
    ubuf[0:CONV_HALO, :] = ubuf[t:t + CONV_HALO, :]

    conv_out = cv_s[...]
    centred = conv_out - jnp.mean(conv_out, axis=-1, keepdims=True)
    normed = centred * lax.rsqrt(jnp.mean(centred * centred, axis=-1, keepdims=True) + EPS)
    y_conv = jnp.dot(_silu(normed * lng_ref[...] + lnb_ref[...]).astype(BF16), wout_ref[0:D_CONV, :],
                     preferred_element_type=F32)

    l0 = lbl_ref[0:1, :]
    lmax = jnp.max(lbl_ref[...], axis=0, keepdims=True)
    lb = jnp.exp(l0 - lmax) / jnp.sum(jnp.exp(lbl_ref[...] - lmax), axis=0, keepdims=True)

    qf_s[...] = _silu(proj(q0, D_REC))
    forget = lb + (1.0 - lb) * _sigmoid(proj(f0, D_REC))
    k_s[...] = 1.0 - forget
    lf = jnp.log(forget)
    hi = lf.astype(BF16)
    rem = lf - hi.astype(F32)
    mid = rem.astype(BF16)
    lfp_s[0] = hi
    lfp_s[1] = mid
    lfp_s[2] = (rem - mid.astype(F32)).astype(BF16)
    v_s[...] = proj(i0, D_REC).astype(BF16)
    sg_s[...] = _silu(proj(g0, D_REC))

    row = lax.broadcasted_iota(I32, (CHUNK, CHUNK), 0)
    col = lax.broadcasted_iota(I32, (CHUNK, CHUNK), 1)
    causal = row >= col
    tri = jnp.where(causal, 1.0, 0.0).astype(BF16)
    for ci in range(n_chunks):
        rows = slice(ci * CHUNK, (ci + 1) * CHUNK)
        bcum = (jnp.dot(tri, lfp_s[0, rows, :], preferred_element_type=F32)
                + jnp.dot(tri, lfp_s[1, rows, :], preferred_element_type=F32)
                + jnp.dot(tri, lfp_s[2, rows, :], preferred_element_type=F32))
        a_last = jnp.exp(bcum[CHUNK - 1:CHUNK, :])
        k_inv = k_s[rows, :] * jnp.exp(-bcum)
        qd_s[rows, :] = (qf_s[rows, :] * jnp.exp(bcum)).astype(BF16)
        ki_s[rows, :] = k_inv.astype(BF16)
        ke_s[rows, :] = (k_inv * a_last).astype(BF16)
        a_s[ci] = a_last

    nt_dims = (((1,), (1,)), ((), ()))
    pairs = [(ci, hd) for ci in range(n_chunks) for hd in heads]

    def blk(ci, hd):
        return slice(ci * CHUNK, (ci + 1) * CHUNK), head_cols(0, hd)

    scores = {}
    for p in pairs:
        rows, cols = blk(*p)
        sc = lax.dot_general(qd_s[rows, cols], ki_s[rows, cols], nt_dims, preferred_element_type=F32)
        scores[p] = jnp.where(causal, sc, 0.0).astype(BF16)
    for p in pairs:
        rows, cols = blk(*p)
        o_s[rows, cols] = jnp.dot(scores[p], v_s[rows, cols], preferred_element_type=F32)
    upd = {}
    for p in pairs:
        rows, cols = blk(*p)
        v_t = v_s[rows, cols].astype(F32).T.astype(BF16)
        upd[p] = jnp.dot(v_t, ke_s[rows, cols], preferred_element_type=F32)
    prev = {}
    for hd in heads:
        st = state[hd]
        for ci in range(n_chunks):
            prev[(ci, hd)] = st.astype(BF16)
            st = st * a_s[ci, :, head_cols(0, hd)] + upd[(ci, hd)]
        state[hd] = st
    for p in pairs:
        rows, cols = blk(*p)
        o_s[rows, cols] += lax.dot_general(qd_s[rows, cols], prev[p], nt_dims, preferred_element_type=F32)

    for rows in stat_blocks:
        for hd in heads:
            o = _rms(o_s[rows, head_cols(0, hd)], rng_ref[...]) * sg_s[rows, head_cols(0, hd)]
            yb[rows, head_cols(0, hd)] = o.astype(BF16)

    y = y_conv + jnp.dot(yb[...], wout_ref[D_CONV:D_MODEL, :], preferred_element_type=F32)
    x1 = x_ref[0] + y * lax.rsqrt(jnp.mean(y * y, axis=-1, keepdims=True) + EPS) * vec_s[2:3, :]
    x1_ref[0] = x1
    h2v = x1 * lax.rsqrt(jnp.mean(x1 * x1, axis=-1, keepdims=True) + EPS) * vec_s[3:4, :] + vec_s[4:5, :]
    h2p_ref[0] = _pack_pair(h2v[:, 0:D_PACK], h2v[:, D_PACK:D_MODEL])

    h_hi = h2v.astype(BF16)
    h_lo = (h2v - h_hi.astype(F32)).astype(BF16)
    both = jnp.dot(h_hi, wr_ref[...], preferred_element_type=F32)
    logits = (both[:, 0:LANES] + both[:, LANES:2 * LANES]
              + jnp.dot(h_lo, wr_ref[:, 0:LANES], preferred_element_type=F32)) + br_ref[...]
    lt = logits.T
    neg = jnp.float32(-jnp.inf)
    r8 = lax.broadcasted_iota(I32, (SUBLANES, t), 0)
    gl = jnp.where(r8 < N_GROUPS, lt[GROUP_ROW0:GROUP_ROW0 + SUBLANES], neg)
    gmax = jnp.max(gl, axis=0, keepdims=True)
    gidx = jnp.min(jnp.where(gl == gmax, r8, SUBLANES), axis=0, keepdims=True)
    gprob = 1.0 / jnp.sum(jnp.exp(gl - gmax), axis=0, keepdims=True)
    re = lax.broadcasted_iota(I32, (N_EXPERTS, t), 0)
    el = jnp.where((re // EXPERTS_PER_GROUP) == gidx, lt[0:N_EXPERTS], neg)
    m1 = jnp.max(el, axis=0, keepdims=True)
    i1 = jnp.min(jnp.where(el == m1, re, N_EXPERTS), axis=0, keepdims=True)
    el2 = jnp.where(re == i1, neg, el)
    m2 = jnp.max(el2, axis=0, keepdims=True)
    i2 = jnp.min(jnp.where(el2 == m2, re, N_EXPERTS), axis=0, keepdims=True)
    r = jnp.exp(m2 - m1)
    w1 = gprob / (1.0 + r)
    w2 = gprob * r / (1.0 + r)
    hot1 = re == i1
    hot2 = re == i2
    hot = jnp.where(hot1 | hot2, 1.0, 0.0)
    prefix = jnp.dot(hot.astype(BF16), upper_ref[...], preferred_element_type=F32) + carry[...]
    rank1 = jnp.sum(jnp.where(hot1, prefix, 0.0), axis=0, keepdims=True)
    rank2 = jnp.sum(jnp.where(hot2, prefix, 0.0), axis=0, keepdims=True)
    carry[...] = carry[...] + jnp.sum(hot, axis=1, keepdims=True)
    zi = jnp.zeros((SUBLANES - 4, t), I32)
    ri_ref[0] = jnp.concatenate([i1, i2, rank1.astype(I32), rank2.astype(I32), zi], axis=0)
    rw_ref[0] = jnp.concatenate([w1, w2, jnp.zeros((SUBLANES - 2, t), F32)], axis=0)
    cnt_ref[...] = jnp.broadcast_to(carry[...], cnt_ref.shape)


def _mix(x, mod3, g_pre, g_post, g_ffn, w_in, dwk, dwb, lng, lnb, lbl, rng, w_out, w_r, b_r):
    bsz, s, d = x.shape
    t = SEQ_TILE
    nt = s // t
    tile = lambda b, j: (b, j, 0)
    rtile = lambda b, j: (b * nt + j, 0, 0)
    const2 = lambda b, j: (0, 0)

    def const_spec(shape):
        return pl.BlockSpec(shape, const2, pipeline_mode=pl.Buffered(1))

    upper = jnp.triu(jnp.ones((t, t), BF16), k=1)
    return pl.pallas_call(
        _mix_kernel,
        grid=(bsz, nt),
        in_specs=[
            pl.BlockSpec((1, t, d), tile),
            pl.BlockSpec((1, 6, d), lambda b, j: (b, 0, 0)),
            const_spec((1, d)),
            const_spec((1, d)),
            const_spec((1, d)),
            const_spec((d, D_IN)),
            const_spec((CONV_WIDTH, D_CONV)),
            const_spec((1, D_CONV)),
            const_spec((1, D_CONV)),
            const_spec((1, D_CONV)),
            const_spec((2, D_REC)),
            const_spec((1, REC_DV)),
            const_spec((d, d)),
            const_spec((d, 2 * LANES)),
            const_spec((1, LANES)),
            const_spec((t, t)),
        ],
        out_specs=[
            pl.BlockSpec((1, t, d), tile),
            pl.BlockSpec((1, t, D_PACK), tile),
            pl.BlockSpec((1, SUBLANES, t), rtile),
            pl.BlockSpec((1, SUBLANES, t), rtile),
            pl.BlockSpec((N_EXPERTS, LANES), const2),
        ],
        out_shape=[
            jax.ShapeDtypeStruct((bsz, s, d), F32),
            jax.ShapeDtypeStruct((bsz, s, D_PACK), U32),
            jax.ShapeDtypeStruct((bsz * nt, SUBLANES, t), I32),
            jax.ShapeDtypeStruct((bsz * nt, SUBLANES, t), F32),
            jax.ShapeDtypeStruct((N_EXPERTS, LANES), F32),
        ],
        scratch_shapes=[
            pltpu.VMEM((SUBLANES, d), F32),
            pltpu.VMEM((t, D_REC), F32),
            pltpu.VMEM((t, D_REC), F32),
            pltpu.VMEM((CONV_HALO + t, D_CONV), F32),
            pltpu.VMEM((t, D_CONV), F32),
            pltpu.VMEM((3, t, D_REC), BF16),
            pltpu.VMEM((t, D_REC), F32),
            pltpu.VMEM((t, D_REC), BF16),
            pltpu.VMEM((t, D_REC), BF16),
            pltpu.VMEM((t, D_REC), BF16),
            pltpu.VMEM((t, D_REC), BF16),
            pltpu.VMEM((t // CHUNK, 1, D_REC), F32),
            pltpu.VMEM((t, D_REC), F32),
            pltpu.VMEM((t, D_REC), BF16),
            pltpu.VMEM((REC_HEADS, REC_DV, REC_DK), F32),
            pltpu.VMEM((N_EXPERTS, 1), F32),
        ],
        compiler_params=pltpu.CompilerParams(
            dimension_semantics=("arbitrary", "arbitrary"),
            vmem_limit_bytes=VMEM_LIMIT),
        name="mixer",
    )(x, mod3, g_pre, g_post, g_ffn, w_in, dwk, dwb, lng, lnb, lbl, rng, w_out, w_r, b_r, upper)


def _meta_kernel(cnt_ref, start_ref, bexp_ref, bvalid_ref, nblk_ref):
    shift = EXPERT_ROWS.bit_length() - 1
    n_blocks = bexp_ref.shape[0]

    def fill(e, blk0, cnt):
        def body(jb, c_):
            bexp_ref[jb] = e
            bvalid_ref[jb] = jnp.clip(cnt - ((jb - blk0) << shift), 0, EXPERT_ROWS)
            return c_
        return body

    def per_expert(e, blk0):
        nb = (cnt_ref[e] + (EXPERT_ROWS - 1)) >> shift
        start_ref[e] = blk0 << shift
        lax.fori_loop(blk0, blk0 + nb, fill(e, blk0, cnt_ref[e]), 0)
        return blk0 + nb

    used = lax.fori_loop(0, N_EXPERTS, per_expert, jnp.int32(0))
    lax.fori_loop(used, n_blocks, fill(N_EXPERTS - 1, used, 0), 0)
    nblk_ref[0] = used


def _meta(counts, n_blocks):
    smem = pl.BlockSpec(memory_space=pltpu.SMEM)
    return pl.pallas_call(
        _meta_kernel,
        in_specs=[smem],
        out_specs=[smem, smem, smem, smem],
        out_shape=[
            jax.ShapeDtypeStruct((N_EXPERTS,), I32),
            jax.ShapeDtypeStruct((n_blocks,), I32),
            jax.ShapeDtypeStruct((n_blocks,), I32),
            jax.ShapeDtypeStruct((1,), I32),
        ],
        name="moe_layout",
    )(counts)


def _positions_kernel(start_ref, ri_ref, pos_ref):
    e = ri_ref[:, 0:TOP_K, :]
    seg = jnp.zeros(e.shape, I32)
    for k in range(N_EXPERTS):
        seg = jnp.where(e == k, start_ref[k], seg)
    pos = seg + ri_ref[:, TOP_K:2 * TOP_K, :]
    pad = jnp.zeros((e.shape[0], SUBLANES - TOP_K, e.shape[2]), I32)
    pos_ref[...] = jnp.concatenate([pos, pad], axis=1)


def _positions(starts, ri):
    n_tiles, _, t = ri.shape
    tb = SUBLANES
    return pl.pallas_call(
        _positions_kernel,
        grid_spec=pltpu.PrefetchScalarGridSpec(
            num_scalar_prefetch=1,
            grid=(n_tiles // tb,),
            in_specs=[pl.BlockSpec((tb, SUBLANES, t), lambda i, s_: (i, 0, 0))],
            out_specs=pl.BlockSpec((tb, SUBLANES, t), lambda i, s_: (i, 0, 0)),
        ),
        out_shape=jax.ShapeDtypeStruct(ri.shape, I32),
        name="moe_positions",
    )(starts, ri)


_SC_MESH = dict(core_axis_name="core", subcore_axis_name="subcore")


def _sc_worker_base(rows_per_worker):
    wid = lax.axis_index("core") * (SC_WORKERS // 2) + lax.axis_index("subcore")
    return wid * rows_per_worker


def _sc_dispatch(rows_in, pos_a, pos_b, n_rows):
    n, d = rows_in.shape
    w = SC_WINDOW
    per_worker = n // SC_WORKERS
    chunks = per_worker // SC_INDEX_CHUNK
    windows = SC_INDEX_CHUNK // w
    dma = pltpu.SemaphoreType.DMA

    @pl.kernel(out_type=jax.ShapeDtypeStruct((n_rows, d), rows_in.dtype),
               mesh=plsc.VectorSubcoreMesh(**_SC_MESH),
               scratch_types=[pltpu.VMEM((SC_INDEX_CHUNK,), I32), pltpu.VMEM((SC_INDEX_CHUNK,), I32)]
               + [pltpu.VMEM((w, d), rows_in.dtype)] * windows + [dma] * (3 * windows),
               name="moe_dispatch_sc")
    def run(x_hbm, ia_hbm, ib_hbm, o_hbm, ia_v, ib_v, *rest):
        bufs, sems = rest[:windows], rest[windows:]
        base = _sc_worker_base(per_worker)

        @pl.loop(0, chunks)
        def _(c):
            row0 = base + c * SC_INDEX_CHUNK
            loads = [pltpu.make_async_copy(x_hbm.at[pl.ds(row0 + k * w, w)], bufs[k], sems[3 * k])
                     for k in range(windows)]
            for cp in loads:
                cp.start()
            pltpu.sync_copy(ia_hbm.at[pl.ds(row0, SC_INDEX_CHUNK)], ia_v)
            pltpu.sync_copy(ib_hbm.at[pl.ds(row0, SC_INDEX_CHUNK)], ib_v)
            stores = []
            for k in range(windows):
                loads[k].wait()
                for idx_v, sem in ((ia_v, sems[3 * k + 1]), (ib_v, sems[3 * k + 2])):
                    cp = pltpu.make_async_copy(bufs[k], o_hbm.at[idx_v.at[pl.ds(k * w, w)]], sem)
                    cp.start()
                    stores.append(cp)
            for cp in stores:
                cp.wait()

    return run(rows_in, pos_a, pos_b)


def _sc_gather(table, idx):
    m = idx.shape[0]
    d = table.shape[1]
    w = SC_WINDOW
    per_worker = m // SC_WORKERS
    chunks = per_worker // SC_INDEX_CHUNK
    windows = SC_INDEX_CHUNK // w
    dma = pltpu.SemaphoreType.DMA

    @pl.kernel(out_type=jax.ShapeDtypeStruct((m, d), table.dtype),
               mesh=plsc.VectorSubcoreMesh(**_SC_MESH),
               scratch_types=[pltpu.VMEM((SC_INDEX_CHUNK,), I32)]
               + [pltpu.VMEM((w, d), table.dtype)] * windows + [dma] * (2 * windows),
               name="moe_gather_sc")
    def run(x_hbm, i_hbm, o_hbm, i_v, *rest):
        bufs, sems = rest[:windows], rest[windows:]
        base = _sc_worker_base(per_worker)

        @pl.loop(0, chunks)
        def _(c):
            row0 = base + c * SC_INDEX_CHUNK
            pltpu.sync_copy(i_hbm.at[pl.ds(row0, SC_INDEX_CHUNK)], i_v)
            gathers = [pltpu.make_async_copy(x_hbm.at[i_v.at[pl.ds(k * w, w)]], bufs[k], sems[2 * k])
                       for k in range(windows)]
            for cp in gathers:
                cp.start()
            stores = []
            for k in range(windows):
                gathers[k].wait()
                cp = pltpu.make_async_copy(bufs[k], o_hbm.at[pl.ds(row0 + k * w, w)], sems[2 * k + 1])
                cp.start()
                stores.append(cp)
            for cp in stores:
                cp.wait()

    return run(table, idx)


def _expert_kernel(bexp_ref, bvalid_ref, nblk_ref, x_ref, wg_ref, wu_ref, wd_ref, y_ref, wg_s, wu_s, wd_s):
    jb = pl.program_id(0)
    valid = bvalid_ref[jb]

    @pl.when((jb == 0) | (bexp_ref[jb] != bexp_ref[jnp.maximum(jb - 1, 0)]))
    def _():
        wg_s[...] = wg_ref[0].astype(BF16)
        wu_s[...] = wu_ref[0].astype(BF16)
        wd_s[...] = wd_ref[0].astype(BF16)

    subs = [slice(r0, r0 + EXPERT_SUB) for r0 in range(0, EXPERT_ROWS, EXPERT_SUB)]
    passes = (valid + (EXPERT_SUB - 1)) // EXPERT_SUB

    def run(n_live):
        halves = []
        for rows in subs[:n_live]:
            live = lax.broadcasted_iota(I32, (EXPERT_SUB, 1), 0) + rows.start < valid
            lo, hi = _unpack_pair(jnp.where(live, x_ref[rows, :], jnp.uint32(0)))
            halves.append((lo.astype(BF16), hi.astype(BF16)))

        def project(w_s):
            return [jnp.dot(lo, w_s[0:D_PACK, :], preferred_element_type=F32)
                    + jnp.dot(hi, w_s[D_PACK:D_MODEL, :], preferred_element_type=F32) for lo, hi in halves]

        hidden = [(_silu(g) * u).astype(BF16) for g, u in zip(project(wg_s), project(wu_s))]
        outs = [jnp.dot(hb, wd_s[...], preferred_element_type=F32) for hb in hidden]
        for rows, y in zip(subs, outs):
            y_ref[rows, :] = _pack_pair(y[:, 0:D_PACK], y[:, D_PACK:D_MODEL])
        for rows in subs[n_live:]:
            y_ref[rows, :] = jnp.zeros((EXPERT_SUB, D_PACK), U32)

    for n_live in range(len(subs) + 1):
        pl.when(passes == n_live)(lambda n_live=n_live: run(n_live))


def _experts(bexp, bvalid, nblk, xs, w_gate, w_up, w_down):
    n_rows, dp = xs.shape
    d = 2 * dp
    n_blocks = n_rows // EXPERT_ROWS
    return pl.pallas_call(
        _expert_kernel,
        grid_spec=pltpu.PrefetchScalarGridSpec(
            num_scalar_prefetch=3,
            grid=(n_blocks,),
            in_specs=[
                pl.BlockSpec((EXPERT_ROWS, dp), lambda jb, be, bv, nb: (jnp.minimum(jb, nb[0] - 1), 0)),
                pl.BlockSpec((1, d, D_EXPERT), lambda jb, be, bv, nb: (be[jb], 0, 0)),
                pl.BlockSpec((1, d, D_EXPERT), lambda jb, be, bv, nb: (be[jb], 0, 0)),
                pl.BlockSpec((1, D_EXPERT, d), lambda jb, be, bv, nb: (be[jb], 0, 0)),
            ],
            out_specs=pl.BlockSpec((EXPERT_ROWS, dp), lambda jb, be, bv, nb: (jb, 0)),
            scratch_shapes=[
                pltpu.VMEM((d, D_EXPERT), BF16),
                pltpu.VMEM((d, D_EXPERT), BF16),
                pltpu.VMEM((D_EXPERT, d), BF16),
            ],
        ),
        out_shape=jax.ShapeDtypeStruct((n_rows, dp), U32),
        compiler_params=pltpu.CompilerParams(dimension_semantics=("arbitrary",)),
        name="moe_experts",
    )(bexp, bvalid, nblk, xs, w_gate, w_up, w_down)


def _combine_kernel(rw_ref, x1_ref, gt2_ref, gpost_ref, ya_ref, yb_ref, o_ref):
    wt = rw_ref[0].T
    a_lo, a_hi = _unpack_pair(ya_ref[...])
    b_lo, b_hi = _unpack_pair(yb_ref[...])
    y_lo = a_lo * wt[:, 0:1] + b_lo * wt[:, 1:2]
    y_hi = a_hi * wt[:, 0:1] + b_hi * wt[:, 1:2]
    ssq = jnp.sum(y_lo * y_lo, axis=-1, keepdims=True) + jnp.sum(y_hi * y_hi, axis=-1, keepdims=True)
    inv = lax.rsqrt(ssq * (1.0 / D_MODEL) + EPS)
    gate = gt2_ref[0] * gpost_ref[...]
    o_ref[:, 0:D_PACK] = x1_ref[:, 0:D_PACK] + y_lo * inv * gate[:, 0:D_PACK]
    o_ref[:, D_PACK:D_MODEL] = x1_ref[:, D_PACK:D_MODEL] + y_hi * inv * gate[:, D_PACK:D_MODEL]


def _combine_kernel_into(prev_ref, *refs):
    del prev_ref
    _combine_kernel(*refs)


def _combine(rw, x1, gt2, g_post, yg, seq, piece, prev):
    n, d = x1.shape
    t = SEQ_TILE
    per_batch = seq // t
    tiles = yg.shape[0] // (TOP_K * t)
    tile0 = piece * tiles
    in_specs = [
        pl.BlockSpec((1, SUBLANES, t), lambda i: (i + tile0, 0, 0)),
        pl.BlockSpec((t, d), lambda i: (i + tile0, 0)),
        pl.BlockSpec((1, 1, d), lambda i: ((i + tile0) // per_batch, 0, 0)),
        pl.BlockSpec((1, d), lambda i: (0, 0)),
        pl.BlockSpec((t, D_PACK), lambda i: (i, 0)),
        pl.BlockSpec((t, D_PACK), lambda i: (i + tiles, 0)),
    ]
    args = (rw, x1, gt2, g_post, yg, yg)
    body, aliases = _combine_kernel, {}
    if prev is not None:
        in_specs = [pl.BlockSpec(memory_space=pl.ANY)] + in_specs
        args = (prev,) + args
        body, aliases = _combine_kernel_into, {0: 0}
    return pl.pallas_call(
        body,
        grid=(tiles,),
        in_specs=in_specs,
        out_specs=pl.BlockSpec((t, d), lambda i: (i + tile0, 0)),
        out_shape=jax.ShapeDtypeStruct((n, d), F32),
        input_output_aliases=aliases,
        compiler_params=pltpu.CompilerParams(dimension_semantics=("arbitrary",)),
        name="moe_combine",
    )(*args)


def kernel(x, c, w_ada, b_ada, g_pre_mix, g_post_mix, w_in, dw_kernel, dw_bias, conv_ln_gain, conv_ln_bias, lb_logits, rec_norm_gain, w_out, g_pre_ffn, g_post_ffn, w_router_group, b_router_group, w_router_expert, b_router_expert, w_gate, w_up, w_down):
    bsz, s, d = x.shape
    depth = w_ada.shape[0]
    assert depth == 1 and lb_logits.shape[0] == 2
    n_tok = bsz * s
    n_rows = n_tok * TOP_K + N_EXPERTS * EXPERT_ROWS
    for l in range(depth):
        mod = _ada(c, w_ada[l], b_ada[l])
        mod3 = mod.reshape(bsz, 6, d)
        pad = LANES - N_EXPERTS - N_GROUPS
        w_r = jnp.concatenate([w_router_expert[l], w_router_group[l], jnp.zeros((d, pad), F32)], axis=1)
        w_r_hi = w_r.astype(BF16)
        w_r = jnp.concatenate([w_r_hi, (w_r - w_r_hi.astype(F32)).astype(BF16)], axis=1)
        b_r = jnp.concatenate([b_router_expert[l], b_router_group[l], jnp.zeros((pad,), F32)]).reshape(1, LANES)
        x1, h2p, ri, rw, cnt = _mix(
            x, mod3, g_pre_mix[l].reshape(1, d), g_post_mix[l].reshape(1, d), g_pre_ffn[l].reshape(1, d),
            w_in[l].astype(BF16), dw_kernel[l], dw_bias[l].reshape(1, D_CONV),
            conv_ln_gain[l].reshape(1, D_CONV), conv_ln_bias[l].reshape(1, D_CONV),
            lb_logits, rec_norm_gain[l].reshape(1, REC_DV), w_out[l].astype(BF16), w_r, b_r)
        counts = cnt[:, 0].astype(I32)
        starts, bexp, bvalid, nblk = _meta(counts, n_rows // EXPERT_ROWS)
        pos = _positions(starts, ri)
        pos_a = pos[:, 0, :].reshape(n_tok)
        pos_b = pos[:, 1, :].reshape(n_tok)
        xs = _sc_dispatch(h2p.reshape(n_tok, D_PACK), pos_a, pos_b, n_rows)
        ys = _experts(bexp, bvalid, nblk, xs, w_gate[l], w_up[l], w_down[l])
        piece_tok = n_tok // COMBINE_PIECES
        out = None
        for p in range(COMBINE_PIECES):
            tok = slice(p * piece_tok, (p + 1) * piece_tok)
            yg = _sc_gather(ys, jnp.concatenate([pos_a[tok], pos_b[tok]]))
            out = _combine(rw, x1.reshape(n_tok, d), mod3[:, 5:6, :], g_post_ffn[l].reshape(1, d), yg, s, p, out)
        x = out.reshape(bsz, s, d)
    return x
```

```python
import jax
import jax.numpy as jnp
from jax import lax
from jax.experimental import pallas as pl
from jax.experimental.pallas import tpu as pltpu
from jax.experimental.pallas import tpu_sc as plsc

D_MODEL = 1024
D_CONV = 512
D_REC = 512
CONV_WIDTH = 31
REC_HEADS = 4
REC_DK = 128
REC_DV = 128
CHUNK = 64
D_IN = 2 * D_CONV + 4 * D_REC
N_GROUPS = 4
EXPERTS_PER_GROUP = 8
N_EXPERTS = 32
TOP_K = 2
D_EXPERT = 256
EPS = 1e-6

LANES = 128
SUBLANES = 8
SEQ_TILE = 1024
CONV_ROWS = 64
STAT_ROWS = 64
CONV_HALO = 32
GROUP_ROW0 = 32
EXPERT_ROWS = 1024
COMBINE_PIECES = 8
EXPERT_SUB = 256
SC_WINDOW = 64
SC_WORKERS = 32
SC_INDEX_CHUNK = 128
D_PACK = D_MODEL // 2
VMEM_LIMIT = 56 * 1024 * 1024

F32 = jnp.float32
BF16 = jnp.bfloat16
I32 = jnp.int32
U32 = jnp.uint32
HI = lax.Precision.HIGHEST


def _sigmoid(v):
    return 0.5 * jnp.tanh(0.5 * v) + 0.5


def _silu(v):
    return v * _sigmoid(v)


def _rms(v, gain):
    return v * lax.rsqrt(jnp.mean(v * v, axis=-1, keepdims=True) + EPS) * gain


def _pack_pair(lo, hi):
    lo_bits = lax.bitcast_convert_type(lo.astype(BF16).astype(F32), U32)
    hi_bits = lax.bitcast_convert_type(hi.astype(BF16).astype(F32), U32)
    return (lo_bits >> 16) | (hi_bits & jnp.uint32(0xFFFF0000))


def _unpack_pair(packed):
    lo = lax.bitcast_convert_type(packed << 16, F32)
    hi = lax.bitcast_convert_type(packed & jnp.uint32(0xFFFF0000), F32)
    return lo, hi


def _ada_kernel(c_ref, w_ref, b_ref, o_ref):
    cond = _silu(c_ref[...])
    o_ref[...] = jnp.dot(cond, w_ref[...], precision=HI, preferred_element_type=F32) + b_ref[...]


def _ada(c, w, b):
    bsz, d = c.shape
    n = w.shape[1]
    return pl.pallas_call(
        _ada_kernel,
        grid=(n // d,),
        in_specs=[
            pl.BlockSpec((bsz, d), lambda j: (0, 0)),
            pl.BlockSpec((d, d), lambda j: (0, j)),
            pl.BlockSpec((1, d), lambda j: (0, j)),
        ],
        out_specs=pl.BlockSpec((bsz, d), lambda j: (0, j)),
        out_shape=jax.ShapeDtypeStruct((bsz, n), F32),
        name="ada_mod",
    )(c, w, b.reshape(1, n))


def _mix_kernel(x_ref, mod_ref, gpre_ref, gpost_ref, gffn_ref, win_ref, dwk_ref, dwb_ref,
                lng_ref, lnb_ref, lbl_ref, rng_ref, wout_ref, wr_ref, br_ref, upper_ref,
                x1_ref, h2p_ref, ri_ref, rw_ref, cnt_ref,
                vec_s, qf_s, sg_s, ubuf, cv_s, lfp_s, k_s, qd_s, ki_s, ke_s, v_s, a_s, o_s, yb,
                state, carry):
    b = pl.program_id(0)
    j = pl.program_id(1)
    t = SEQ_TILE
    n_chunks = t // CHUNK
    heads = range(REC_HEADS)

    @pl.when(j == 0)
    def _():
        ubuf[0:CONV_HALO, :] = jnp.zeros((CONV_HALO, D_CONV), F32)
        state[...] = jnp.zeros(state.shape, F32)

    @pl.when((j == 0) & (b == 0))
    def _():
        carry[...] = jnp.zeros(carry.shape, F32)

    sh1 = mod_ref[0, 0:1, :]
    sc1 = mod_ref[0, 1:2, :]
    gt1 = mod_ref[0, 2:3, :]
    sh2 = mod_ref[0, 3:4, :]
    sc2 = mod_ref[0, 4:5, :]

    def row_loop(rows_per_step, body):
        def step(ci, c_):
            body(pl.ds(pl.multiple_of(ci * rows_per_step, rows_per_step), rows_per_step))
            return c_
        lax.fori_loop(0, t // rows_per_step, step, 0)

    stat_blocks = [slice(r0, r0 + STAT_ROWS) for r0 in range(0, t, STAT_ROWS)]

    def head_cols(base, hd):
        return slice(base + hd * REC_DK, base + (hd + 1) * REC_DK)

    vec_s[0:1, :] = gpre_ref[...] * (1.0 + sc1)
    vec_s[1:2, :] = sh1
    vec_s[2:3, :] = gt1 * gpost_ref[...]
    vec_s[3:4, :] = gffn_ref[...] * (1.0 + sc2)
    vec_s[4:5, :] = sh2

    xv = x_ref[0]
    hb = (xv * lax.rsqrt(jnp.mean(xv * xv, axis=-1, keepdims=True) + EPS) * vec_s[0:1, :] + vec_s[1:2, :]).astype(BF16)
    q0 = 2 * D_CONV
    f0 = q0 + D_REC
    i0 = f0 + D_REC
    g0 = i0 + D_REC

    def proj(c0, width):
        return jnp.dot(hb, win_ref[:, c0:c0 + width], preferred_element_type=F32)

    conv_in = proj(0, 2 * D_CONV)
    ubuf[CONV_HALO:CONV_HALO + t, :] = conv_in[:, 0:D_CONV] * _sigmoid(conv_in[:, D_CONV:2 * D_CONV])

    lead = CONV_HALO - (CONV_WIDTH - 1)
    win_rows = CONV_ROWS + CONV_HALO

    def conv(rows):
        for lt in range(D_CONV // LANES):
            lanes = slice(lt * LANES, (lt + 1) * LANES)
            win = ubuf[pl.ds(rows.start, win_rows), lanes]
            acc = jnp.broadcast_to(dwb_ref[:, lanes], (CONV_ROWS, LANES))
            for res in range(SUBLANES):
                shifted = win if res == 0 else pltpu.roll(win, win_rows - res, axis=0)
                for al in range(0, win_rows - CONV_ROWS + 1, SUBLANES):
                    kk = al + res - lead
                    if 0 <= kk < CONV_WIDTH and al + CONV_ROWS + res <= win_rows:
                        acc = acc + shifted[al:al + CONV_ROWS] * dwk_ref[kk:kk + 1, lanes]
            cv_s[rows, lanes] = acc

    row_loop(CONV_ROWS, conv)
    ubuf[0:CONV_HALO, :] = ubuf[t:t + CONV_HALO, :]

    conv_out = cv_s[...]
    centred = conv_out - jnp.mean(conv_out, axis=-1, keepdims=True)
    normed = centred * lax.rsqrt(jnp.mean(centred * centred, axis=-1, keepdims=True) + EPS)
    y_conv = jnp.dot(_silu(normed * lng_ref[...] + lnb_ref[...]).astype(BF16), wout_ref[0:D_CONV, :],
                     preferred_element_type=F32)

    l0 = lbl_ref[0:1, :]
    lmax = jnp.max(lbl_ref[...], axis=0, keepdims=True)
    lb = jnp.exp(l0 - lmax) / jnp.sum(jnp.exp(lbl_ref[...] - lmax), axis=0, keepdims=True)

    qf_s[...] = _silu(proj(q0, D_REC))
    forget = lb + (1.0 - lb) * _sigmoid(proj(f0, D_REC))
    k_s[...] = 1.0 - forget
    lf = jnp.log(forget)
    hi = lf.astype(BF16)
    rem = lf - hi.astype(F32)
    mid = rem.astype(BF16)
    lfp_s[0] = hi
    lfp_s[1] = mid
    lfp_s[2] = (rem - mid.astype(F32)).astype(BF16)
    v_s[...] = proj(i0, D_REC).astype(BF16)
    sg_s[...] = _silu(proj(g0, D_REC))

    row = lax.broadcasted_iota(I32, (CHUNK, CHUNK), 0)
    col = lax.broadcasted_iota(I32, (CHUNK, CHUNK), 1)
    causal = row >= col
    tri = jnp.where(causal, 1.0, 0.0).astype(BF16)
    for ci in range(n_chunks):
        rows = slice(ci * CHUNK, (ci + 1) * CHUNK)
        bcum = (jnp.dot(tri, lfp_s[0, rows, :], preferred_element_type=F32)
                + jnp.dot(tri, lfp_s[1, rows, :], preferred_element_type=F32)
                + jnp.dot(tri, lfp_s[2, rows, :], preferred_element_type=F32))
        a_last = jnp.exp(bcum[CHUNK - 1:CHUNK, :])
        k_inv = k_s[rows, :] * jnp.exp(-bcum)
        qd_s[rows, :] = (qf_s[rows, :] * jnp.exp(bcum)).astype(BF16)
        ki_s[rows, :] = k_inv.astype(BF16)
        ke_s[rows, :] = (k_inv * a_last).astype(BF16)
        a_s[ci] = a_last

    nt_dims = (((1,), (1,)), ((), ()))
    pairs = [(ci, hd) for ci in range(n_chunks) for hd in heads]

    def blk(ci, hd):
        return slice(ci * CHUNK, (ci + 1) * CHUNK), head_cols(0, hd)

    scores = {}
    for p in pairs:
        rows, cols = blk(*p)
        sc = lax.dot_general(qd_s[rows, cols], ki_s[rows, cols], nt_dims, preferred_element_type=F32)
        scores[p] = jnp.where(causal, sc, 0.0).astype(BF16)
    for p in pairs:
        rows, cols = blk(*p)
        o_s[rows, cols] = jnp.dot(scores[p], v_s[rows, cols], preferred_element_type=F32)
    upd = {}
    for p in pairs:
        rows, cols = blk(*p)
        v_t = v_s[rows, cols].astype(F32).T.astype(BF16)
        upd[p] = jnp.dot(v_t, ke_s[rows, cols], preferred_element_type=F32)
    prev = {}
    for hd in heads:
        st = state[hd]
        for ci in range(n_chunks):
            prev[(ci, hd)] = st.astype(BF16)
            st = st * a_s[ci, :, head_cols(0, hd)] + upd[(ci, hd)]
        state[hd] = st
    for p in pairs:
        rows, cols = blk(*p)
        o_s[rows, cols] += lax.dot_general(qd_s[rows, cols], prev[p], nt_dims, preferred_element_type=F32)

    for rows in stat_blocks:
        for hd in heads:
            o = _rms(o_s[rows, head_cols(0, hd)], rng_ref[...]) * sg_s[rows, head_cols(0, hd)]
            yb[rows, head_cols(0, hd)] = o.astype(BF16)

    y = y_conv + jnp.dot(yb[...], wout_ref[D_CONV:D_MODEL, :], preferred_element_type=F32)
    x1 = x_ref[0] + y * lax.rsqrt(jnp.mean(y * y, axis=-1, keepdims=True) + EPS) * vec_s[2:3, :]
    x1_ref[0] = x1
    h2v = x1 * lax.rsqrt(jnp.mean(x1 * x1, axis=-1, keepdims=True) + EPS) * vec_s[3:4, :] + vec_s[4:5, :]
    h2p_ref[0] = _pack_pair(h2v[:, 0:D_PACK], h2v[:, D_PACK:D_MODEL])

    h_hi = h2v.astype(BF16)
    h_lo = (h2v - h_hi.astype(F32)).astype(BF16)
    both = jnp.dot(h_hi, wr_ref[...], preferred_element_type=F32)
    logits = (both[:, 0:LANES] + both[:, LANES:2 * LANES]
              + jnp.dot(h_lo, wr_ref[:, 0:LANES], preferred_element_type=F32)) + br_ref[...]
    lt = logits.T
    neg = jnp.float32(-jnp.inf)
    r8 = lax.broadcasted_iota(I32, (SUBLANES, t), 0)
    gl = jnp.where(r8 < N_GROUPS, lt[GROUP_ROW0:GROUP_ROW0 + SUBLANES], neg)
    gmax = jnp.max(gl, axis=0, keepdims=True)
    gidx = jnp.min(jnp.where(gl == gmax, r8, SUBLANES), axis=0, keepdims=True)
    gprob = 1.0 / jnp.sum(jnp.exp(gl - gmax), axis=0, keepdims=True)
    re = lax.broadcasted_iota(I32, (N_EXPERTS, t), 0)
    el = jnp.where((re // EXPERTS_PER_GROUP) == gidx, lt[0:N_EXPERTS], neg)
    m1 = jnp.max(el, axis=0, keepdims=True)
    i1 = jnp.min(jnp.where(el == m1, re, N_EXPERTS), axis=0, keepdims=True)
    el2 = jnp.where(re == i1, neg, el)
    m2 = jnp.max(el2, axis=0, keepdims=True)
    i2 = jnp.min(jnp.where(el2 == m2, re, N_EXPERTS), axis=0, keepdims=True)
    r = jnp.exp(m2 - m1)
    w1 = gprob / (1.0 + r)
    w2 = gprob * r / (1.0 + r)
    hot1 = re == i1
    hot2 = re == i2
    hot = jnp.where(hot1 | hot2, 1.0, 0.0)
    prefix = jnp.dot(hot.astype(BF16), upper_ref[...], preferred_element_type=F32) + carry[...]
    rank1 = jnp.sum(jnp.where(hot1, prefix, 0.0), axis=0, keepdims=True)
    rank2 = jnp.sum(jnp.where(hot2, prefix, 0.0), axis=0, keepdims=True)
    carry[...] = carry[...] + jnp.sum(hot, axis=1, keepdims=True)
    zi = jnp.zeros((SUBLANES - 4, t), I32)
    ri_ref[0] = jnp.concatenate([i1, i2, rank1.astype(I32), rank2.astype(I32), zi], axis=0)
    rw_ref[0] = jnp.concatenate([w1, w2, jnp.zeros((SUBLANES - 2, t), F32)], axis=0)
    cnt_ref[...] = jnp.broadcast_to(carry[...], cnt_ref.shape)


def _mix(x, mod3, g_pre, g_post, g_ffn, w_in, dwk, dwb, lng, lnb, lbl, rng, w_out, w_r, b_r):
    bsz, s, d = x.shape
    t = SEQ_TILE
    nt = s // t
    tile = lambda b, j: (b, j, 0)
    rtile = lambda b, j: (b * nt + j, 0, 0)
    const2 = lambda b, j: (0, 0)

    def const_spec(shape):
        return pl.BlockSpec(shape, const2, pipeline_mode=pl.Buffered(1))

    upper = jnp.triu(jnp.ones((t, t), BF16), k=1)
    return pl.pallas_call(
        _mix_kernel,
        grid=(bsz, nt),
        in_specs=[
            pl.BlockSpec((1, t, d), tile),
            pl.BlockSpec((1, 6, d), lambda b, j: (b, 0, 0)),
            const_spec((1, d)),
            const_spec((1, d)),
            const_spec((1, d)),
            const_spec((d, D_IN)),
            const_spec((CONV_WIDTH, D_CONV)),
            const_spec((1, D_CONV)),
            const_spec((1, D_CONV)),
            const_spec((1, D_CONV)),
            const_spec((2, D_REC)),
            const_spec((1, REC_DV)),
            const_spec((d, d)),
            const_spec((d, 2 * LANES)),
            const_spec((1, LANES)),
            const_spec((t, t)),
        ],
        out_specs=[
            pl.BlockSpec((1, t, d), tile),
            pl.BlockSpec((1, t, D_PACK), tile),
            pl.BlockSpec((1, SUBLANES, t), rtile),
            pl.BlockSpec((1, SUBLANES, t), rtile),
            pl.BlockSpec((N_EXPERTS, LANES), const2),
        ],
        out_shape=[
            jax.ShapeDtypeStruct((bsz, s, d), F32),
            jax.ShapeDtypeStruct((bsz, s, D_PACK), U32),
            jax.ShapeDtypeStruct((bsz * nt, SUBLANES, t), I32),
            jax.ShapeDtypeStruct((bsz * nt, SUBLANES, t), F32),
            jax.ShapeDtypeStruct((N_EXPERTS, LANES), F32),
        ],
        scratch_shapes=[
            pltpu.VMEM((SUBLANES, d), F32),
            pltpu.VMEM((t, D_REC), F32),
            pltpu.VMEM((t, D_REC), F32),
            pltpu.VMEM((CONV_HALO + t, D_CONV), F32),
            pltpu.VMEM((t, D_CONV), F32),
            pltpu.VMEM((3, t, D_REC), BF16),
            pltpu.VMEM((t, D_REC), F32),
            pltpu.VMEM((t, D_REC), BF16),
            pltpu.VMEM((t, D_REC), BF16),
            pltpu.VMEM((t, D_REC), BF16),
            pltpu.VMEM((t, D_REC), BF16),
            pltpu.VMEM((t // CHUNK, 1, D_REC), F32),
            pltpu.VMEM((t, D_REC), F32),
            pltpu.VMEM((t, D_REC), BF16),
            pltpu.VMEM((REC_HEADS, REC_DV, REC_DK), F32),
            pltpu.VMEM((N_EXPERTS, 1), F32),
        ],
        compiler_params=pltpu.CompilerParams(
            dimension_semantics=("arbitrary", "arbitrary"),
            vmem_limit_bytes=VMEM_LIMIT),
        name="mixer",
    )(x, mod3, g_pre, g_post, g_ffn, w_in, dwk, dwb, lng, lnb, lbl, rng, w_out, w_r, b_r, upper)


def _meta_kernel(cnt_ref, start_ref, bexp_ref, bvalid_ref, nblk_ref):
    shift = EXPERT_ROWS.bit_length() - 1
    n_blocks = bexp_ref.shape[0]

    def fill(e, blk0, cnt):
        def body(jb, c_):
            bexp_ref[jb] = e
            bvalid_ref[jb] = jnp.clip(cnt - ((jb - blk0) << shift), 0, EXPERT_ROWS)
            return c_
        return body

    def per_expert(e, blk0):
        nb = (cnt_ref[e] + (EXPERT_ROWS - 1)) >> shift
        start_ref[e] = blk0 << shift
        lax.fori_loop(blk0, blk0 + nb, fill(e, blk0, cnt_ref[e]), 0)
        return blk0 + nb

    used = lax.fori_loop(0, N_EXPERTS, per_expert, jnp.int32(0))
    lax.fori_loop(used, n_blocks, fill(N_EXPERTS - 1, used, 0), 0)
    nblk_ref[0] = used


def _meta(counts, n_blocks):
    smem = pl.BlockSpec(memory_space=pltpu.SMEM)
    return pl.pallas_call(
        _meta_kernel,
        in_specs=[smem],
        out_specs=[smem, smem, smem, smem],
        out_shape=[
            jax.ShapeDtypeStruct((N_EXPERTS,), I32),
            jax.ShapeDtypeStruct((n_blocks,), I32),
            jax.ShapeDtypeStruct((n_blocks,), I32),
            jax.ShapeDtypeStruct((1,), I32),
        ],
        name="moe_layout",
    )(counts)


def _positions_kernel(start_ref, ri_ref, pos_ref):
    e = ri_ref[:, 0:TOP_K, :]
    seg = jnp.zeros(e.shape, I32)
    for k in range(N_EXPERTS):
        seg = jnp.where(e == k, start_ref[k], seg)
    pos = seg + ri_ref[:, TOP_K:2 * TOP_K, :]
    for k in range(TOP_K):
        pos_ref[k] = pos[:, k, :]


def _positions(starts, ri):
    n_tiles, _, t = ri.shape
    tb = SUBLANES
    return pl.pallas_call(
        _positions_kernel,
        grid_spec=pltpu.PrefetchScalarGridSpec(
            num_scalar_prefetch=1,
            grid=(n_tiles // tb,),
            in_specs=[pl.BlockSpec((tb, SUBLANES, t), lambda i, s_: (i, 0, 0))],
            out_specs=pl.BlockSpec((TOP_K, tb, t), lambda i, s_: (0, i, 0)),
        ),
        out_shape=jax.ShapeDtypeStruct((TOP_K, n_tiles, t), I32),
        name="moe_positions",
    )(starts, ri).reshape(TOP_K, n_tiles * t)


_SC_MESH = dict(core_axis_name="core", subcore_axis_name="subcore")


def _sc_worker_base(rows_per_worker):
    wid = lax.axis_index("core") * (SC_WORKERS // 2) + lax.axis_index("subcore")
    return wid * rows_per_worker


def _sc_dispatch(rows_in, pos, n_rows):
    n, d = rows_in.shape
    w = SC_WINDOW
    per_worker = n // SC_WORKERS
    chunks = per_worker // SC_INDEX_CHUNK
    windows = SC_INDEX_CHUNK // w
    dma = pltpu.SemaphoreType.DMA

    @pl.kernel(out_type=jax.ShapeDtypeStruct((n_rows, d), rows_in.dtype),
               mesh=plsc.VectorSubcoreMesh(**_SC_MESH),
               scratch_types=[pltpu.VMEM((SC_INDEX_CHUNK,), I32), pltpu.VMEM((SC_INDEX_CHUNK,), I32)]
               + [pltpu.VMEM((w, d), rows_in.dtype)] * windows + [dma] * (3 * windows),
               name="moe_dispatch_sc")
    def run(x_hbm, i_hbm, o_hbm, ia_v, ib_v, *rest):
        bufs, sems = rest[:windows], rest[windows:]
        base = _sc_worker_base(per_worker)

        @pl.loop(0, chunks)
        def _(c):
            row0 = base + c * SC_INDEX_CHUNK
            loads = [pltpu.make_async_copy(x_hbm.at[pl.ds(row0 + k * w, w)], bufs[k], sems[3 * k])
                     for k in range(windows)]
            for cp in loads:
                cp.start()
            pltpu.sync_copy(i_hbm.at[0, pl.ds(row0, SC_INDEX_CHUNK)], ia_v)
            pltpu.sync_copy(i_hbm.at[1, pl.ds(row0, SC_INDEX_CHUNK)], ib_v)
            stores = []
            for k in range(windows):
                loads[k].wait()
                for idx_v, sem in ((ia_v, sems[3 * k + 1]), (ib_v, sems[3 * k + 2])):
                    cp = pltpu.make_async_copy(bufs[k], o_hbm.at[idx_v.at[pl.ds(k * w, w)]], sem)
                    cp.start()
                    stores.append(cp)
            for cp in stores:
                cp.wait()

    return run(rows_in, pos)


def _sc_gather(table, pos, tok0, n_piece):
    m = TOP_K * n_piece
    d = table.shape[1]
    w = SC_WINDOW
    per_worker = m // SC_WORKERS
    chunks = per_worker // SC_INDEX_CHUNK
    windows = SC_INDEX_CHUNK // w
    workers_per_slot = SC_WORKERS // TOP_K
    dma = pltpu.SemaphoreType.DMA

    @pl.kernel(out_type=jax.ShapeDtypeStruct((m, d), table.dtype),
               mesh=plsc.VectorSubcoreMesh(**_SC_MESH),
               scratch_types=[pltpu.VMEM((SC_INDEX_CHUNK,), I32)]
               + [pltpu.VMEM((w, d), table.dtype)] * windows + [dma] * (2 * windows),
               name="moe_gather_sc")
    def run(x_hbm, i_hbm, o_hbm, i_v, *rest):
        bufs, sems = rest[:windows], rest[windows:]
        wid = lax.axis_index("core") * (SC_WORKERS // 2) + lax.axis_index("subcore")
        slot = wid // workers_per_slot
        src0 = tok0 + (wid % workers_per_slot) * per_worker

        @pl.loop(0, chunks)
        def _(c):
            pltpu.sync_copy(i_hbm.at[slot, pl.ds(src0 + c * SC_INDEX_CHUNK, SC_INDEX_CHUNK)], i_v)
            row0 = wid * per_worker + c * SC_INDEX_CHUNK
            gathers = [pltpu.make_async_copy(x_hbm.at[i_v.at[pl.ds(k * w, w)]], bufs[k], sems[2 * k])
                       for k in range(windows)]
            for cp in gathers:
                cp.start()
            stores = []
            for k in range(windows):
                gathers[k].wait()
                cp = pltpu.make_async_copy(bufs[k], o_hbm.at[pl.ds(row0 + k * w, w)], sems[2 * k + 1])
                cp.start()
                stores.append(cp)
            for cp in stores:
                cp.wait()

    return run(table, pos)


def _expert_kernel(bexp_ref, bvalid_ref, nblk_ref, x_ref, wg_ref, wu_ref, wd_ref, y_ref, wg_s, wu_s, wd_s):
    jb = pl.program_id(0)
    valid = bvalid_ref[jb]

    @pl.when((jb == 0) | (bexp_ref[jb] != bexp_ref[jnp.maximum(jb - 1, 0)]))
    def _():
        wg_s[...] = wg_ref[0].astype(BF16)
        wu_s[...] = wu_ref[0].astype(BF16)
        wd_s[...] = wd_ref[0].astype(BF16)

    subs = [slice(r0, r0 + EXPERT_SUB) for r0 in range(0, EXPERT_ROWS, EXPERT_SUB)]
    passes = (valid + (EXPERT_SUB - 1)) // EXPERT_SUB

    def run(n_live):
        halves = []
        for rows in subs[:n_live]:
            live = lax.broadcasted_iota(I32, (EXPERT_SUB, 1), 0) + rows.start < valid
            lo, hi = _unpack_pair(jnp.where(live, x_ref[rows, :], jnp.uint32(0)))
            halves.append((lo.astype(BF16), hi.astype(BF16)))

        def project(w_s):
            return [jnp.dot(lo, w_s[0:D_PACK, :], preferred_element_type=F32)
                    + jnp.dot(hi, w_s[D_PACK:D_MODEL, :], preferred_element_type=F32) for lo, hi in halves]

        hidden = [(_silu(g) * u).astype(BF16) for g, u in zip(project(wg_s), project(wu_s))]
        outs = [jnp.dot(hb, wd_s[...], preferred_element_type=F32) for hb in hidden]
        for rows, y in zip(subs, outs):
            y_ref[rows, :] = _pack_pair(y[:, 0:D_PACK], y[:, D_PACK:D_MODEL])
        for rows in subs[n_live:]:
            y_ref[rows, :] = jnp.zeros((EXPERT_SUB, D_PACK), U32)

    for n_live in range(len(subs) + 1):
        pl.when(passes == n_live)(lambda n_live=n_live: run(n_live))


def _experts(bexp, bvalid, nblk, xs, w_gate, w_up, w_down):
    n_rows, dp = xs.shape
    d = 2 * dp
    n_blocks = n_rows // EXPERT_ROWS
    return pl.pallas_call(
        _expert_kernel,
        grid_spec=pltpu.PrefetchScalarGridSpec(
            num_scalar_prefetch=3,
            grid=(n_blocks,),
            in_specs=[
                pl.BlockSpec((EXPERT_ROWS, dp), lambda jb, be, bv, nb: (jnp.minimum(jb, nb[0] - 1), 0)),
                pl.BlockSpec((1, d, D_EXPERT), lambda jb, be, bv, nb: (be[jb], 0, 0)),
                pl.BlockSpec((1, d, D_EXPERT), lambda jb, be, bv, nb: (be[jb], 0, 0)),
                pl.BlockSpec((1, D_EXPERT, d), lambda jb, be, bv, nb: (be[jb], 0, 0)),
            ],
            out_specs=pl.BlockSpec((EXPERT_ROWS, dp), lambda jb, be, bv, nb: (jb, 0)),
            scratch_shapes=[
                pltpu.VMEM((d, D_EXPERT), BF16),
                pltpu.VMEM((d, D_EXPERT), BF16),
                pltpu.VMEM((D_EXPERT, d), BF16),
            ],
        ),
        out_shape=jax.ShapeDtypeStruct((n_rows, dp), U32),
        compiler_params=pltpu.CompilerParams(dimension_semantics=("arbitrary",)),
        name="moe_experts",
    )(bexp, bvalid, nblk, xs, w_gate, w_up, w_down)


def _combine_kernel(rw_ref, x1_ref, gt2_ref, gpost_ref, ya_ref, yb_ref, o_ref):
    wt = rw_ref[0].T
    a_lo, a_hi = _unpack_pair(ya_ref[...])
    b_lo, b_hi = _unpack_pair(yb_ref[...])
    y_lo = a_lo * wt[:, 0:1] + b_lo * wt[:, 1:2]
    y_hi = a_hi * wt[:, 0:1] + b_hi * wt[:, 1:2]
    ssq = jnp.sum(y_lo * y_lo, axis=-1, keepdims=True) + jnp.sum(y_hi * y_hi, axis=-1, keepdims=True)
    inv = lax.rsqrt(ssq * (1.0 / D_MODEL) + EPS)
    gate = gt2_ref[0] * gpost_ref[...]
    o_ref[:, 0:D_PACK] = x1_ref[:, 0:D_PACK] + y_lo * inv * gate[:, 0:D_PACK]
    o_ref[:, D_PACK:D_MODEL] = x1_ref[:, D_PACK:D_MODEL] + y_hi * inv * gate[:, D_PACK:D_MODEL]


def _combine_kernel_into(prev_ref, *refs):
    del prev_ref
    _combine_kernel(*refs)


def _combine(rw, x1, gt2, g_post, yg, seq, piece, prev):
    n, d = x1.shape
    t = SEQ_TILE
    per_batch = seq // t
    tiles = yg.shape[0] // (TOP_K * t)
    tile0 = piece * tiles
    in_specs = [
        pl.BlockSpec((1, SUBLANES, t), lambda i: (i + tile0, 0, 0)),
        pl.BlockSpec((t, d), lambda i: (i + tile0, 0)),
        pl.BlockSpec((1, 1, d), lambda i: ((i + tile0) // per_batch, 0, 0)),
        pl.BlockSpec((1, d), lambda i: (0, 0)),
        pl.BlockSpec((t, D_PACK), lambda i: (i, 0)),
        pl.BlockSpec((t, D_PACK), lambda i: (i + tiles, 0)),
    ]
    args = (rw, x1, gt2, g_post, yg, yg)
    body, aliases = _combine_kernel, {}
    if prev is not None:
        in_specs = [pl.BlockSpec(memory_space=pl.ANY)] + in_specs
        args = (prev,) + args
        body, aliases = _combine_kernel_into, {0: 0}
    return pl.pallas_call(
        body,
        grid=(tiles,),
        in_specs=in_specs,
        out_specs=pl.BlockSpec((t, d), lambda i: (i + tile0, 0)),
        out_shape=jax.ShapeDtypeStruct((n, d), F32),
        input_output_aliases=aliases,
        compiler_params=pltpu.CompilerParams(dimension_semantics=("arbitrary",)),
        name="moe_combine",
    )(*args)


def kernel(x, c, w_ada, b_ada, g_pre_mix, g_post_mix, w_in, dw_kernel, dw_bias, conv_ln_gain, conv_ln_bias, lb_logits, rec_norm_gain, w_out, g_pre_ffn, g_post_ffn, w_router_group, b_router_group, w_router_expert, b_router_expert, w_gate, w_up, w_down):
    bsz, s, d = x.shape
    depth = w_ada.shape[0]
    assert depth == 1 and lb_logits.shape[0] == 2
    n_tok = bsz * s
    n_rows = n_tok * TOP_K + N_EXPERTS * EXPERT_ROWS
    for l in range(depth):
        mod = _ada(c, w_ada[l], b_ada[l])
        mod3 = mod.reshape(bsz, 6, d)
        pad = LANES - N_EXPERTS - N_GROUPS
        w_r = jnp.concatenate([w_router_expert[l], w_router_group[l], jnp.zeros((d, pad), F32)], axis=1)
        w_r_hi = w_r.astype(BF16)
        w_r = jnp.concatenate([w_r_hi, (w_r - w_r_hi.astype(F32)).astype(BF16)], axis=1)
        b_r = jnp.concatenate([b_router_expert[l], b_router_group[l], jnp.zeros((pad,), F32)]).reshape(1, LANES)
        x1, h2p, ri, rw, cnt = _mix(
            x, mod3, g_pre_mix[l].reshape(1, d), g_post_mix[l].reshape(1, d), g_pre_ffn[l].reshape(1, d),
            w_in[l].astype(BF16), dw_kernel[l], dw_bias[l].reshape(1, D_CONV),
            conv_ln_gain[l].reshape(1, D_CONV), conv_ln_bias[l].reshape(1, D_CONV),
            lb_logits, rec_norm_gain[l].reshape(1, REC_DV), w_out[l].astype(BF16), w_r, b_r)
        counts = cnt[:, 0].astype(I32)
        starts, bexp, bvalid, nblk = _meta(counts, n_rows // EXPERT_ROWS)
        pos = _positions(starts, ri)
        xs = _sc_dispatch(h2p.reshape(n_tok, D_PACK), pos, n_rows)
        ys = _experts(bexp, bvalid, nblk, xs, w_gate[l], w_up[l], w_down[l])
        piece_tok = n_tok // COMBINE_PIECES
        out = None
        for p in range(COMBINE_PIECES):
            yg = _sc_gather(ys, pos, p * piece_tok, piece_tok)
            out = _combine(rw, x1.reshape(n_tok, d), mod3[:, 5:6, :], g_post_ffn[l].reshape(1, d), yg, s, p, out)
        x = out.reshape(bsz, s, d)
    return x
```

```python
import jax
import jax.numpy as jnp
from jax import lax
from jax.experimental import pallas as pl
from jax.experimental.pallas import tpu as pltpu
from jax.experimental.pallas import tpu_sc as plsc

D_MODEL = 1024
D_CONV = 512
D_REC = 512
CONV_WIDTH = 31
REC_HEADS = 4
REC_DK = 128
REC_DV = 128
CHUNK = 64
D_IN = 2 * D_CONV + 4 * D_REC
N_GROUPS = 4
EXPERTS_PER_GROUP = 8
N_EXPERTS = 32
TOP_K = 2
D_EXPERT = 256
EPS = 1e-6

LANES = 128
SUBLANES = 8
SEQ_TILE = 1024
CONV_ROWS = 64
STAT_ROWS = 64
CONV_HALO = 32
GROUP_ROW0 = 32
EXPERT_ROWS = 2048
COMBINE_PIECES = 8
EXPERT_SUB = 256
SC_WINDOW = 64
SC_WORKERS = 32
SC_INDEX_CHUNK = 128
D_PACK = D_MODEL // 2
VMEM_LIMIT = 56 * 1024 * 1024

F32 = jnp.float32
BF16 = jnp.bfloat16
I32 = jnp.int32
U32 = jnp.uint32
HI = lax.Precision.HIGHEST


def _sigmoid(v):
    return 0.5 * jnp.tanh(0.5 * v) + 0.5


def _silu(v):
    return v * _sigmoid(v)


def _rms(v, gain):
    return v * lax.rsqrt(jnp.mean(v * v, axis=-1, keepdims=True) + EPS) * gain


def _pack_pair(lo, hi):
    lo_bits = lax.bitcast_convert_type(lo.astype(BF16).astype(F32), U32)
    hi_bits = lax.bitcast_convert_type(hi.astype(BF16).astype(F32), U32)
    return (lo_bits >> 16) | (hi_bits & jnp.uint32(0xFFFF0000))


def _unpack_pair(packed):
    lo = lax.bitcast_convert_type(packed << 16, F32)
    hi = lax.bitcast_convert_type(packed & jnp.uint32(0xFFFF0000), F32)
    return lo, hi


def _ada_kernel(c_ref, w_ref, b_ref, o_ref):
    cond = _silu(c_ref[...])
    o_ref[...] = jnp.dot(cond, w_ref[...], precision=HI, preferred_element_type=F32) + b_ref[...]


def _ada(c, w, b):
    bsz, d = c.shape
    n = w.shape[1]
    return pl.pallas_call(
        _ada_kernel,
        grid=(n // d,),
        in_specs=[
            pl.BlockSpec((bsz, d), lambda j: (0, 0)),
            pl.BlockSpec((d, d), lambda j: (0, j)),
            pl.BlockSpec((1, d), lambda j: (0, j)),
        ],
        out_specs=pl.BlockSpec((bsz, d), lambda j: (0, j)),
        out_shape=jax.ShapeDtypeStruct((bsz, n), F32),
        name="ada_mod",
    )(c, w, b.reshape(1, n))


def _mix_kernel(x_ref, mod_ref, gpre_ref, gpost_ref, gffn_ref, win_ref, dwk_ref, dwb_ref,
                lng_ref, lnb_ref, lbl_ref, rng_ref, wout_ref, wr_ref, br_ref, upper_ref,
                x1_ref, h2p_ref, ri_ref, rw_ref, cnt_ref,
                vec_s, qf_s, sg_s, ubuf, cv_s, lfp_s, k_s, qd_s, ki_s, ke_s, v_s, a_s, o_s, yb,
                state, carry):
    b = pl.program_id(0)
    j = pl.program_id(1)
    t = SEQ_TILE
    n_chunks = t // CHUNK
    heads = range(REC_HEADS)

    @pl.when(j == 0)
    def _():
        ubuf[0:CONV_HALO, :] = jnp.zeros((CONV_HALO, D_CONV), F32)
        state[...] = jnp.zeros(state.shape, F32)

    @pl.when((j == 0) & (b == 0))
    def _():
        carry[...] = jnp.zeros(carry.shape, F32)

    sh1 = mod_ref[0, 0:1, :]
    sc1 = mod_ref[0, 1:2, :]
    gt1 = mod_ref[0, 2:3, :]
    sh2 = mod_ref[0, 3:4, :]
    sc2 = mod_ref[0, 4:5, :]

    def row_loop(rows_per_step, body):
        def step(ci, c_):
            body(pl.ds(pl.multiple_of(ci * rows_per_step, rows_per_step), rows_per_step))
            return c_
        lax.fori_loop(0, t // rows_per_step, step, 0)

    stat_blocks = [slice(r0, r0 + STAT_ROWS) for r0 in range(0, t, STAT_ROWS)]

    def head_cols(base, hd):
        return slice(base + hd * REC_DK, base + (hd + 1) * REC_DK)

    vec_s[0:1, :] = gpre_ref[...] * (1.0 + sc1)
    vec_s[1:2, :] = sh1
    vec_s[2:3, :] = gt1 * gpost_ref[...]
    vec_s[3:4, :] = gffn_ref[...] * (1.0 + sc2)
    vec_s[4:5, :] = sh2

    xv = x_ref[0]
    hb = (xv * lax.rsqrt(jnp.mean(xv * xv, axis=-1, keepdims=True) + EPS) * vec_s[0:1, :] + vec_s[1:2, :]).astype(BF16)
    q0 = 2 * D_CONV
    f0 = q0 + D_REC
    i0 = f0 + D_REC
    g0 = i0 + D_REC

    def proj(c0, width):
        return jnp.dot(hb, win_ref[:, c0:c0 + width], preferred_element_type=F32)

    conv_in = proj(0, 2 * D_CONV)
    ubuf[CONV_HALO:CONV_HALO + t, :] = conv_in[:, 0:D_CONV] * _sigmoid(conv_in[:, D_CONV:2 * D_CONV])

    lead = CONV_HALO - (CONV_WIDTH - 1)
    win_rows = CONV_ROWS + CONV_HALO

    def conv(rows):
        for lt in range(D_CONV // LANES):
            lanes = slice(lt * LANES, (lt + 1) * LANES)
            win = ubuf[pl.ds(rows.start, win_rows), lanes]
            acc = jnp.broadcast_to(dwb_ref[:, lanes], (CONV_ROWS, LANES))
            for res in range(SUBLANES):
                shifted = win if res == 0 else pltpu.roll(win, win_rows - res, axis=0)
                for al in range(0, win_rows - CONV_ROWS + 1, SUBLANES):
                    kk = al + res - lead
                    if 0 <= kk < CONV_WIDTH and al + CONV_ROWS + res <= win_rows:
                        acc = acc + shifted[al:al + CONV_ROWS] * dwk_ref[kk:kk + 1, lanes]
            cv_s[rows, lanes] = acc

    row_loop(CONV_ROWS, conv)
    ubuf[0:CONV_HALO, :] = ubuf[t:t + CONV_HALO, :]

    conv_out = cv_s[...]
    centred = conv_out - jnp.mean(conv_out, axis=-1, keepdims=True)
    normed = centred * lax.rsqrt(jnp.mean(centred * centred, axis=-1, keepdims=True) + EPS)
    y_conv = jnp.dot(_silu(normed * lng_ref[...] + lnb_ref[...]).astype(BF16), wout_ref[0:D_CONV, :],
                     preferred_element_type=F32)

    l0 = lbl_ref[0:1, :]
    lmax = jnp.max(lbl_ref[...], axis=0, keepdims=True)
    lb = jnp.exp(l0 - lmax) / jnp.sum(jnp.exp(lbl_ref[...] - lmax), axis=0, keepdims=True)

    qf_s[...] = _silu(proj(q0, D_REC))
    forget = lb + (1.0 - lb) * _sigmoid(proj(f0, D_REC))
    k_s[...] = 1.0 - forget
    lf = jnp.log(forget)
    hi = lf.astype(BF16)
    rem = lf - hi.astype(F32)
    mid = rem.astype(BF16)
    lfp_s[0] = hi
    lfp_s[1] = mid
    lfp_s[2] = (rem - mid.astype(F32)).astype(BF16)
    v_s[...] = proj(i0, D_REC).astype(BF16)
    sg_s[...] = _silu(proj(g0, D_REC))

    row = lax.broadcasted_iota(I32, (CHUNK, CHUNK), 0)
    col = lax.broadcasted_iota(I32, (CHUNK, CHUNK), 1)
    causal = row >= col
    tri = jnp.where(causal, 1.0, 0.0).astype(BF16)
    for ci in range(n_chunks):
        rows = slice(ci * CHUNK, (ci + 1) * CHUNK)
        bcum = (jnp.dot(tri, lfp_s[0, rows, :], preferred_element_type=F32)
                + jnp.dot(tri, lfp_s[1, rows, :], preferred_element_type=F32)
                + jnp.dot(tri, lfp_s[2, rows, :], preferred_element_type=F32))
        a_last = jnp.exp(bcum[CHUNK - 1:CHUNK, :])
        k_inv = k_s[rows, :] * jnp.exp(-bcum)
        qd_s[rows, :] = (qf_s[rows, :] * jnp.exp(bcum)).astype(BF16)
        ki_s[rows, :] = k_inv.astype(BF16)
        ke_s[rows, :] = (k_inv * a_last).astype(BF16)
        a_s[ci] = a_last

    nt_dims = (((1,), (1,)), ((), ()))
    pairs = [(ci, hd) for ci in range(n_chunks) for hd in heads]

    def blk(ci, hd):
        return slice(ci * CHUNK, (ci + 1) * CHUNK), head_cols(0, hd)

    scores = {}
    for p in pairs:
        rows, cols = blk(*p)
        sc = lax.dot_general(qd_s[rows, cols], ki_s[rows, cols], nt_dims, preferred_element_type=F32)
        scores[p] = jnp.where(causal, sc, 0.0).astype(BF16)
    for p in pairs:
        rows, cols = blk(*p)
        o_s[rows, cols] = jnp.dot(scores[p], v_s[rows, cols], preferred_element_type=F32)
    upd = {}
    for p in pairs:
        rows, cols = blk(*p)
        v_t = v_s[rows, cols].astype(F32).T.astype(BF16)
        upd[p] = jnp.dot(v_t, ke_s[rows, cols], preferred_element_type=F32)
    prev = {}
    for hd in heads:
        st = state[hd]
        for ci in range(n_chunks):
            prev[(ci, hd)] = st.astype(BF16)
            st = st * a_s[ci, :, head_cols(0, hd)] + upd[(ci, hd)]
        state[hd] = st
    for p in pairs:
        rows, cols = blk(*p)
        o_s[rows, cols] += lax.dot_general(qd_s[rows, cols], prev[p], nt_dims, preferred_element_type=F32)

    for rows in stat_blocks:
        for hd in heads:
            o = _rms(o_s[rows, head_cols(0, hd)], rng_ref[...]) * sg_s[rows, head_cols(0, hd)]
            yb[rows, head_cols(0, hd)] = o.astype(BF16)

    y = y_conv + jnp.dot(yb[...], wout_ref[D_CONV:D_MODEL, :], preferred_element_type=F32)
    x1 = x_ref[0] + y * lax.rsqrt(jnp.mean(y * y, axis=-1, keepdims=True) + EPS) * vec_s[2:3, :]
    x1_ref[0] = x1
    h2v = x1 * lax.rsqrt(jnp.mean(x1 * x1, axis=-1, keepdims=True) + EPS) * vec_s[3:4, :] + vec_s[4:5, :]
    h2p_ref[0] = _pack_pair(h2v[:, 0:D_PACK], h2v[:, D_PACK:D_MODEL])

    h_hi = h2v.astype(BF16)
    h_lo = (h2v - h_hi.astype(F32)).astype(BF16)
    both = jnp.dot(h_hi, wr_ref[...], preferred_element_type=F32)
    logits = (both[:, 0:LANES] + both[:, LANES:2 * LANES]
              + jnp.dot(h_lo, wr_ref[:, 0:LANES], preferred_element_type=F32)) + br_ref[...]
    lt = logits.T
    neg = jnp.float32(-jnp.inf)
    r8 = lax.broadcasted_iota(I32, (SUBLANES, t), 0)
    gl = jnp.where(r8 < N_GROUPS, lt[GROUP_ROW0:GROUP_ROW0 + SUBLANES], neg)
    gmax = jnp.max(gl, axis=0, keepdims=True)
    gidx = jnp.min(jnp.where(gl == gmax, r8, SUBLANES), axis=0, keepdims=True)
    gprob = 1.0 / jnp.sum(jnp.exp(gl - gmax), axis=0, keepdims=True)
    re = lax.broadcasted_iota(I32, (N_EXPERTS, t), 0)
    el = jnp.where((re // EXPERTS_PER_GROUP) == gidx, lt[0:N_EXPERTS], neg)
    m1 = jnp.max(el, axis=0, keepdims=True)
    i1 = jnp.min(jnp.where(el == m1, re, N_EXPERTS), axis=0, keepdims=True)
    el2 = jnp.where(re == i1, neg, el)
    m2 = jnp.max(el2, axis=0, keepdims=True)
    i2 = jnp.min(jnp.where(el2 == m2, re, N_EXPERTS), axis=0, keepdims=True)
    r = jnp.exp(m2 - m1)
    w1 = gprob / (1.0 + r)
    w2 = gprob * r / (1.0 + r)
    hot1 = re == i1
    hot2 = re == i2
    hot = jnp.where(hot1 | hot2, 1.0, 0.0)
    prefix = jnp.dot(hot.astype(BF16), upper_ref[...], preferred_element_type=F32) + carry[...]
    rank1 = jnp.sum(jnp.where(hot1, prefix, 0.0), axis=0, keepdims=True)
    rank2 = jnp.sum(jnp.where(hot2, prefix, 0.0), axis=0, keepdims=True)
    carry[...] = carry[...] + jnp.sum(hot, axis=1, keepdims=True)
    zi = jnp.zeros((SUBLANES - 4, t), I32)
    ri_ref[0] = jnp.concatenate([i1, i2, rank1.astype(I32), rank2.astype(I32), zi], axis=0)
    rw_ref[0] = jnp.concatenate([w1, w2, jnp.zeros((SUBLANES - 2, t), F32)], axis=0)
    cnt_ref[...] = jnp.broadcast_to(carry[...], cnt_ref.shape)


def _mix(x, mod3, g_pre, g_post, g_ffn, w_in, dwk, dwb, lng, lnb, lbl, rng, w_out, w_r, b_r):
    bsz, s, d = x.shape
    t = SEQ_TILE
    nt = s // t
    tile = lambda b, j: (b, j, 0)
    rtile = lambda b, j: (b * nt + j, 0, 0)
    const2 = lambda b, j: (0, 0)

    def const_spec(shape):
        return pl.BlockSpec(shape, const2, pipeline_mode=pl.Buffered(1))

    upper = jnp.triu(jnp.ones((t, t), BF16), k=1)
    return pl.pallas_call(
        _mix_kernel,
        grid=(bsz, nt),
        in_specs=[
            pl.BlockSpec((1, t, d), tile),
            pl.BlockSpec((1, 6, d), lambda b, j: (b, 0, 0)),
            const_spec((1, d)),
            const_spec((1, d)),
            const_spec((1, d)),
            const_spec((d, D_IN)),
            const_spec((CONV_WIDTH, D_CONV)),
            const_spec((1, D_CONV)),
            const_spec((1, D_CONV)),
            const_spec((1, D_CONV)),
            const_spec((2, D_REC)),
            const_spec((1, REC_DV)),
            const_spec((d, d)),
            const_spec((d, 2 * LANES)),
            const_spec((1, LANES)),
            const_spec((t, t)),
        ],
        out_specs=[
            pl.BlockSpec((1, t, d), tile),
            pl.BlockSpec((1, t, D_PACK), tile),
            pl.BlockSpec((1, SUBLANES, t), rtile),
            pl.BlockSpec((1, SUBLANES, t), rtile),
            pl.BlockSpec((N_EXPERTS, LANES), const2),
        ],
        out_shape=[
            jax.ShapeDtypeStruct((bsz, s, d), F32),
            jax.ShapeDtypeStruct((bsz, s, D_PACK), U32),
            jax.ShapeDtypeStruct((bsz * nt, SUBLANES, t), I32),
            jax.ShapeDtypeStruct((bsz * nt, SUBLANES, t), F32),
            jax.ShapeDtypeStruct((N_EXPERTS, LANES), F32),
        ],
        scratch_shapes=[
            pltpu.VMEM((SUBLANES, d), F32),
            pltpu.VMEM((t, D_REC), F32),
            pltpu.VMEM((t, D_REC), F32),
            pltpu.VMEM((CONV_HALO + t, D_CONV), F32),
            pltpu.VMEM((t, D_CONV), F32),
            pltpu.VMEM((3, t, D_REC), BF16),
            pltpu.VMEM((t, D_REC), F32),
            pltpu.VMEM((t, D_REC), BF16),
            pltpu.VMEM((t, D_REC), BF16),
            pltpu.VMEM((t, D_REC), BF16),
            pltpu.VMEM((t, D_REC), BF16),
            pltpu.VMEM((t // CHUNK, 1, D_REC), F32),
            pltpu.VMEM((t, D_REC), F32),
            pltpu.VMEM((t, D_REC), BF16),
            pltpu.VMEM((REC_HEADS, REC_DV, REC_DK), F32),
            pltpu.VMEM((N_EXPERTS, 1), F32),
        ],
        compiler_params=pltpu.CompilerParams(
            dimension_semantics=("arbitrary", "arbitrary"),
            vmem_limit_bytes=VMEM_LIMIT),
        name="mixer",
    )(x, mod3, g_pre, g_post, g_ffn, w_in, dwk, dwb, lng, lnb, lbl, rng, w_out, w_r, b_r, upper)


def _meta_kernel(cnt_ref, start_ref, bexp_ref, bvalid_ref, nblk_ref):
    shift = EXPERT_ROWS.bit_length() - 1
    n_blocks = bexp_ref.shape[0]

    def fill(e, blk0, cnt):
        def body(jb, c_):
            bexp_ref[jb] = e
            bvalid_ref[jb] = jnp.clip(cnt - ((jb - blk0) << shift), 0, EXPERT_ROWS)
            return c_
        return body

    def per_expert(e, blk0):
        nb = (cnt_ref[e] + (EXPERT_ROWS - 1)) >> shift
        start_ref[e] = blk0 << shift
        lax.fori_loop(blk0, blk0 + nb, fill(e, blk0, cnt_ref[e]), 0)
        return blk0 + nb

    used = lax.fori_loop(0, N_EXPERTS, per_expert, jnp.int32(0))
    lax.fori_loop(used, n_blocks, fill(N_EXPERTS - 1, used, 0), 0)
    nblk_ref[0] = used


def _meta(counts, n_blocks):
    smem = pl.BlockSpec(memory_space=pltpu.SMEM)
    return pl.pallas_call(
        _meta_kernel,
        in_specs=[smem],
        out_specs=[smem, smem, smem, smem],
        out_shape=[
            jax.ShapeDtypeStruct((N_EXPERTS,), I32),
            jax.ShapeDtypeStruct((n_blocks,), I32),
            jax.ShapeDtypeStruct((n_blocks,), I32),
            jax.ShapeDtypeStruct((1,), I32),
        ],
        name="moe_layout",
    )(counts)


def _positions_kernel(start_ref, ri_ref, pos_ref):
    e = ri_ref[:, 0:TOP_K, :]
    seg = jnp.zeros(e.shape, I32)
    for k in range(N_EXPERTS):
        seg = jnp.where(e == k, start_ref[k], seg)
    pos = seg + ri_ref[:, TOP_K:2 * TOP_K, :]
    for k in range(TOP_K):
        pos_ref[k] = pos[:, k, :]


def _positions(starts, ri):
    n_tiles, _, t = ri.shape
    tb = SUBLANES
    return pl.pallas_call(
        _positions_kernel,
        grid_spec=pltpu.PrefetchScalarGridSpec(
            num_scalar_prefetch=1,
            grid=(n_tiles // tb,),
            in_specs=[pl.BlockSpec((tb, SUBLANES, t), lambda i, s_: (i, 0, 0))],
            out_specs=pl.BlockSpec((TOP_K, tb, t), lambda i, s_: (0, i, 0)),
        ),
        out_shape=jax.ShapeDtypeStruct((TOP_K, n_tiles, t), I32),
        name="moe_positions",
    )(starts, ri).reshape(TOP_K, n_tiles * t)


_SC_MESH = dict(core_axis_name="core", subcore_axis_name="subcore")


def _sc_worker_base(rows_per_worker):
    wid = lax.axis_index("core") * (SC_WORKERS // 2) + lax.axis_index("subcore")
    return wid * rows_per_worker


def _sc_dispatch(rows_in, pos, n_rows):
    n, d = rows_in.shape
    w = SC_WINDOW
    per_worker = n // SC_WORKERS
    chunks = per_worker // SC_INDEX_CHUNK
    windows = SC_INDEX_CHUNK // w
    dma = pltpu.SemaphoreType.DMA

    @pl.kernel(out_type=jax.ShapeDtypeStruct((n_rows, d), rows_in.dtype),
               mesh=plsc.VectorSubcoreMesh(**_SC_MESH),
               scratch_types=[pltpu.VMEM((SC_INDEX_CHUNK,), I32), pltpu.VMEM((SC_INDEX_CHUNK,), I32)]
               + [pltpu.VMEM((w, d), rows_in.dtype)] * windows + [dma] * (3 * windows),
               name="moe_dispatch_sc")
    def run(x_hbm, i_hbm, o_hbm, ia_v, ib_v, *rest):
        bufs, sems = rest[:windows], rest[windows:]
        base = _sc_worker_base(per_worker)

        @pl.loop(0, chunks)
        def _(c):
            row0 = base + c * SC_INDEX_CHUNK
            loads = [pltpu.make_async_copy(x_hbm.at[pl.ds(row0 + k * w, w)], bufs[k], sems[3 * k])
                     for k in range(windows)]
            for cp in loads:
                cp.start()
            pltpu.sync_copy(i_hbm.at[0, pl.ds(row0, SC_INDEX_CHUNK)], ia_v)
            pltpu.sync_copy(i_hbm.at[1, pl.ds(row0, SC_INDEX_CHUNK)], ib_v)
            stores = []
            for k in range(windows):
                loads[k].wait()
                for idx_v, sem in ((ia_v, sems[3 * k + 1]), (ib_v, sems[3 * k + 2])):
                    cp = pltpu.make_async_copy(bufs[k], o_hbm.at[idx_v.at[pl.ds(k * w, w)]], sem)
                    cp.start()
                    stores.append(cp)
            for cp in stores:
                cp.wait()

    return run(rows_in, pos)


def _sc_gather(table, pos, tok0, n_piece):
    m = TOP_K * n_piece
    d = table.shape[1]
    w = SC_WINDOW
    per_worker = m // SC_WORKERS
    chunks = per_worker // SC_INDEX_CHUNK
    windows = SC_INDEX_CHUNK // w
    workers_per_slot = SC_WORKERS // TOP_K
    dma = pltpu.SemaphoreType.DMA

    @pl.kernel(out_type=jax.ShapeDtypeStruct((m, d), table.dtype),
               mesh=plsc.VectorSubcoreMesh(**_SC_MESH),
               scratch_types=[pltpu.VMEM((SC_INDEX_CHUNK,), I32)]
               + [pltpu.VMEM((w, d), table.dtype)] * windows + [dma] * (2 * windows),
               name="moe_gather_sc")
    def run(x_hbm, i_hbm, o_hbm, i_v, *rest):
        bufs, sems = rest[:windows], rest[windows:]
        wid = lax.axis_index("core") * (SC_WORKERS // 2) + lax.axis_index("subcore")
        slot = wid // workers_per_slot
        src0 = tok0 + (wid % workers_per_slot) * per_worker

        @pl.loop(0, chunks)
        def _(c):
            pltpu.sync_copy(i_hbm.at[slot, pl.ds(src0 + c * SC_INDEX_CHUNK, SC_INDEX_CHUNK)], i_v)
            row0 = wid * per_worker + c * SC_INDEX_CHUNK
            gathers = [pltpu.make_async_copy(x_hbm.at[i_v.at[pl.ds(k * w, w)]], bufs[k], sems[2 * k])
                       for k in range(windows)]
            for cp in gathers:
                cp.start()
            stores = []
            for k in range(windows):
                gathers[k].wait()
                cp = pltpu.make_async_copy(bufs[k], o_hbm.at[pl.ds(row0 + k * w, w)], sems[2 * k + 1])
                cp.start()
                stores.append(cp)
            for cp in stores:
                cp.wait()

    return run(table, pos)


def _expert_kernel(bexp_ref, bvalid_ref, nblk_ref, x_ref, wg_ref, wu_ref, wd_ref, y_ref, wg_s, wu_s, wd_s):
    jb = pl.program_id(0)
    valid = bvalid_ref[jb]

    @pl.when((jb == 0) | (bexp_ref[jb] != bexp_ref[jnp.maximum(jb - 1, 0)]))
    def _():
        wg_s[...] = wg_ref[0].astype(BF16)
        wu_s[...] = wu_ref[0].astype(BF16)
        wd_s[...] = wd_ref[0].astype(BF16)

    subs = [slice(r0, r0 + EXPERT_SUB) for r0 in range(0, EXPERT_ROWS, EXPERT_SUB)]
    passes = (valid + (EXPERT_SUB - 1)) // EXPERT_SUB

    def run(n_live):
        halves = []
        for rows in subs[:n_live]:
            live = lax.broadcasted_iota(I32, (EXPERT_SUB, 1), 0) + rows.start < valid
            lo, hi = _unpack_pair(jnp.where(live, x_ref[rows, :], jnp.uint32(0)))
            halves.append((lo.astype(BF16), hi.astype(BF16)))

        def project(w_s):
            return [jnp.dot(lo, w_s[0:D_PACK, :], preferred_element_type=F32)
                    + jnp.dot(hi, w_s[D_PACK:D_MODEL, :], preferred_element_type=F32) for lo, hi in halves]

        hidden = [(_silu(g) * u).astype(BF16) for g, u in zip(project(wg_s), project(wu_s))]
        outs = [jnp.dot(hb, wd_s[...], preferred_element_type=F32) for hb in hidden]
        for rows, y in zip(subs, outs):
            y_ref[rows, :] = _pack_pair(y[:, 0:D_PACK], y[:, D_PACK:D_MODEL])
        for rows in subs[n_live:]:
            y_ref[rows, :] = jnp.zeros((EXPERT_SUB, D_PACK), U32)

    for n_live in range(len(subs) + 1):
        pl.when(passes == n_live)(lambda n_live=n_live: run(n_live))


def _experts(bexp, bvalid, nblk, xs, w_gate, w_up, w_down):
    n_rows, dp = xs.shape
    d = 2 * dp
    n_blocks = n_rows // EXPERT_ROWS
    return pl.pallas_call(
        _expert_kernel,
        grid_spec=pltpu.PrefetchScalarGridSpec(
            num_scalar_prefetch=3,
            grid=(n_blocks,),
            in_specs=[
                pl.BlockSpec((EXPERT_ROWS, dp), lambda jb, be, bv, nb: (jnp.minimum(jb, nb[0] - 1), 0)),
                pl.BlockSpec((1, d, D_EXPERT), lambda jb, be, bv, nb: (be[jb], 0, 0)),
                pl.BlockSpec((1, d, D_EXPERT), lambda jb, be, bv, nb: (be[jb], 0, 0)),
                pl.BlockSpec((1, D_EXPERT, d), lambda jb, be, bv, nb: (be[jb], 0, 0)),
            ],
            out_specs=pl.BlockSpec((EXPERT_ROWS, dp), lambda jb, be, bv, nb: (jb, 0)),
            scratch_shapes=[
                pltpu.VMEM((d, D_EXPERT), BF16),
                pltpu.VMEM((d, D_EXPERT), BF16),
                pltpu.VMEM((D_EXPERT, d), BF16),
            ],
        ),
        out_shape=jax.ShapeDtypeStruct((n_rows, dp), U32),
        compiler_params=pltpu.CompilerParams(dimension_semantics=("arbitrary",)),
        name="moe_experts",
    )(bexp, bvalid, nblk, xs, w_gate, w_up, w_down)


def _combine_kernel(rw_ref, x1_ref, gt2_ref, gpost_ref, ya_ref, yb_ref, o_ref):
    wt = rw_ref[0].T
    a_lo, a_hi = _unpack_pair(ya_ref[...])
    b_lo, b_hi = _unpack_pair(yb_ref[...])
    y_lo = a_lo * wt[:, 0:1] + b_lo * wt[:, 1:2]
    y_hi = a_hi * wt[:, 0:1] + b_hi * wt[:, 1:2]
    ssq = jnp.sum(y_lo * y_lo, axis=-1, keepdims=True) + jnp.sum(y_hi * y_hi, axis=-1, keepdims=True)
    inv = lax.rsqrt(ssq * (1.0 / D_MODEL) + EPS)
    gate = gt2_ref[0] * gpost_ref[...]
    o_ref[:, 0:D_PACK] = x1_ref[:, 0:D_PACK] + y_lo * inv * gate[:, 0:D_PACK]
    o_ref[:, D_PACK:D_MODEL] = x1_ref[:, D_PACK:D_MODEL] + y_hi * inv * gate[:, D_PACK:D_MODEL]


def _combine_kernel_into(prev_ref, *refs):
    del prev_ref
    _combine_kernel(*refs)


def _combine(rw, x1, gt2, g_post, yg, seq, piece, prev):
    n, d = x1.shape
    t = SEQ_TILE
    per_batch = seq // t
    tiles = yg.shape[0] // (TOP_K * t)
    tile0 = piece * tiles
    in_specs = [
        pl.BlockSpec((1, SUBLANES, t), lambda i: (i + tile0, 0, 0)),
        pl.BlockSpec((t, d), lambda i: (i + tile0, 0)),
        pl.BlockSpec((1, 1, d), lambda i: ((i + tile0) // per_batch, 0, 0)),
        pl.BlockSpec((1, d), lambda i: (0, 0)),
        pl.BlockSpec((t, D_PACK), lambda i: (i, 0)),
        pl.BlockSpec((t, D_PACK), lambda i: (i + tiles, 0)),
    ]
    args = (rw, x1, gt2, g_post, yg, yg)
    body, aliases = _combine_kernel, {}
    if prev is not None:
        in_specs = [pl.BlockSpec(memory_space=pl.ANY)] + in_specs
        args = (prev,) + args
        body, aliases = _combine_kernel_into, {0: 0}
    return pl.pallas_call(
        body,
        grid=(tiles,),
        in_specs=in_specs,
        out_specs=pl.BlockSpec((t, d), lambda i: (i + tile0, 0)),
        out_shape=jax.ShapeDtypeStruct((n, d), F32),
        input_output_aliases=aliases,
        compiler_params=pltpu.CompilerParams(dimension_semantics=("arbitrary",)),
        name="moe_combine",
    )(*args)


def kernel(x, c, w_ada, b_ada, g_pre_mix, g_post_mix, w_in, dw_kernel, dw_bias, conv_ln_gain, conv_ln_bias, lb_logits, rec_norm_gain, w_out, g_pre_ffn, g_post_ffn, w_router_group, b_router_group, w_router_expert, b_router_expert, w_gate, w_up, w_down):
    bsz, s, d = x.shape
    depth = w_ada.shape[0]
    assert depth == 1 and lb_logits.shape[0] == 2
    n_tok = bsz * s
    n_rows = n_tok * TOP_K + N_EXPERTS * EXPERT_ROWS
    for l in range(depth):
        mod = _ada(c, w_ada[l], b_ada[l])
        mod3 = mod.reshape(bsz, 6, d)
        pad = LANES - N_EXPERTS - N_GROUPS
        w_r = jnp.concatenate([w_router_expert[l], w_router_group[l], jnp.zeros((d, pad), F32)], axis=1)
        w_r_hi = w_r.astype(BF16)
        w_r = jnp.concatenate([w_r_hi, (w_r - w_r_hi.astype(F32)).astype(BF16)], axis=1)
        b_r = jnp.concatenate([b_router_expert[l], b_router_group[l], jnp.zeros((pad,), F32)]).reshape(1, LANES)
        x1, h2p, ri, rw, cnt = _mix(
            x, mod3, g_pre_mix[l].reshape(1, d), g_post_mix[l].reshape(1, d), g_pre_ffn[l].reshape(1, d),
            w_in[l].astype(BF16), dw_kernel[l], dw_bias[l].reshape(1, D_CONV),
            conv_ln_gain[l].reshape(1, D_CONV), conv_ln_bias[l].reshape(1, D_CONV),
            lb_logits, rec_norm_gain[l].reshape(1, REC_DV), w_out[l].astype(BF16), w_r, b_r)
        counts = cnt[:, 0].astype(I32)
        starts, bexp, bvalid, nblk = _meta(counts, n_rows // EXPERT_ROWS)
        pos = _positions(starts, ri)
        xs = _sc_dispatch(h2p.reshape(n_tok, D_PACK), pos, n_rows)
        ys = _experts(bexp, bvalid, nblk, xs, w_gate[l], w_up[l], w_down[l])
        piece_tok = n_tok // COMBINE_PIECES
        out = None
        for p in range(COMBINE_PIECES):
            yg = _sc_gather(ys, pos, p * piece_tok, piece_tok)
            out = _combine(rw, x1.reshape(n_tok, d), mod3[:, 5:6, :], g_post_ffn[l].reshape(1, d), yg, s, p, out)
        x = out.reshape(bsz, s, d)
    return x
```

```python
import jax
import jax.numpy as jnp
from jax import lax
from jax.experimental import pallas as pl
from jax.experimental.pallas import tpu as pltpu
from jax.experimental.pallas import tpu_sc as plsc

D_MODEL = 1024
D_CONV = 512
D_REC = 512
CONV_WIDTH = 31
REC_HEADS = 4
REC_DK = 128
REC_DV = 128
CHUNK = 64
D_IN = 2 * D_CONV + 4 * D_REC
N_GROUPS = 4
EXPERTS_PER_GROUP = 8
N_EXPERTS = 32
TOP_K = 2
D_EXPERT = 256
EPS = 1e-6

LANES = 128
SUBLANES = 8
SEQ_TILE = 1024
CONV_ROWS = 64
STAT_ROWS = 64
CONV_HALO = 32
GROUP_ROW0 = 32
EXPERT_ROWS = 1024
COMBINE_PIECES = 8
EXPERT_SUB = 256
SC_WINDOW = 64
SC_WORKERS = 32
SC_INDEX_CHUNK = 128
D_PACK = D_MODEL // 2
VMEM_LIMIT = 56 * 1024 * 1024

F32 = jnp.float32
BF16 = jnp.bfloat16
I32 = jnp.int32
U32 = jnp.uint32
HI = lax.Precision.HIGHEST


def _sigmoid(v):
    return 0.5 * jnp.tanh(0.5 * v) + 0.5


def _silu(v):
    return v * _sigmoid(v)


def _rms(v, gain):
    return v * lax.rsqrt(jnp.mean(v * v, axis=-1, keepdims=True) + EPS) * gain


def _pack_pair(lo, hi):
    lo_bits = lax.bitcast_convert_type(lo.astype(BF16).astype(F32), U32)
    hi_bits = lax.bitcast_convert_type(hi.astype(BF16).astype(F32), U32)
    return (lo_bits >> 16) | (hi_bits & jnp.uint32(0xFFFF0000))


def _unpack_pair(packed):
    lo = lax.bitcast_convert_type(packed << 16, F32)
    hi = lax.bitcast_convert_type(packed & jnp.uint32(0xFFFF0000), F32)
    return lo, hi


def _ada_kernel(c_ref, w_ref, b_ref, o_ref):
    cond = _silu(c_ref[...])
    o_ref[...] = jnp.dot(cond, w_ref[...], precision=HI, preferred_element_type=F32) + b_ref[...]


def _ada(c, w, b):
    bsz, d = c.shape
    n = w.shape[1]
    return pl.pallas_call(
        _ada_kernel,
        grid=(n // d,),
        in_specs=[
            pl.BlockSpec((bsz, d), lambda j: (0, 0)),
            pl.BlockSpec((d, d), lambda j: (0, j)),
            pl.BlockSpec((1, d), lambda j: (0, j)),
        ],
        out_specs=pl.BlockSpec((bsz, d), lambda j: (0, j)),
        out_shape=jax.ShapeDtypeStruct((bsz, n), F32),
        name="ada_mod",
    )(c, w, b.reshape(1, n))


def _mix_kernel(x_ref, mod_ref, gpre_ref, gpost_ref, gffn_ref, win_ref, dwk_ref, dwb_ref,
                lng_ref, lnb_ref, lbl_ref, rng_ref, wout_ref, wr_ref, br_ref, upper_ref,
                x1_ref, h2p_ref, ri_ref, rw_ref, cnt_ref,
                vec_s, qf_s, sg_s, ubuf, cv_s, lfp_s, k_s, qd_s, ki_s, ke_s, v_s, a_s, o_s, yb,
                state, carry):
    b = pl.program_id(0)
    j = pl.program_id(1)
    t = SEQ_TILE
    n_chunks = t // CHUNK
    heads = range(REC_HEADS)

    @pl.when(j == 0)
    def _():
        ubuf[0:CONV_HALO, :] = jnp.zeros((CONV_HALO, D_CONV), F32)
        state[...] = jnp.zeros(state.shape, F32)

    @pl.when((j == 0) & (b == 0))
    def _():
        carry[...] = jnp.zeros(carry.shape, F32)

    sh1 = mod_ref[0, 0:1, :]
    sc1 = mod_ref[0, 1:2, :]
    gt1 = mod_ref[0, 2:3, :]
    sh2 = mod_ref[0, 3:4, :]
    sc2 = mod_ref[0, 4:5, :]

    def row_loop(rows_per_step, body):
        def step(ci, c_):
            body(pl.ds(pl.multiple_of(ci * rows_per_step, rows_per_step), rows_per_step))
            return c_
        lax.fori_loop(0, t // rows_per_step, step, 0)

    stat_blocks = [slice(r0, r0 + STAT_ROWS) for r0 in range(0, t, STAT_ROWS)]

    def head_cols(base, hd):
        return slice(base + hd * REC_DK, base + (hd + 1) * REC_DK)

    vec_s[0:1, :] = gpre_ref[...] * (1.0 + sc1)
    vec_s[1:2, :] = sh1
    vec_s[2:3, :] = gt1 * gpost_ref[...]
    vec_s[3:4, :] = gffn_ref[...] * (1.0 + sc2)
    vec_s[4:5, :] = sh2

    xv = x_ref[0]
    hb = (xv * lax.rsqrt(jnp.mean(xv * xv, axis=-1, keepdims=True) + EPS) * vec_s[0:1, :] + vec_s[1:2, :]).astype(BF16)
    q0 = 2 * D_CONV
    f0 = q0 + D_REC
    i0 = f0 + D_REC
    g0 = i0 + D_REC

    def proj(c0, width):
        return jnp.dot(hb, win_ref[:, c0:c0 + width], preferred_element_type=F32)

    conv_in = proj(0, 2 * D_CONV)
    ubuf[CONV_HALO:CONV_HALO + t, :] = conv_in[:, 0:D_CONV] * _sigmoid(conv_in[:, D_CONV:2 * D_CONV])

    lead = CONV_HALO - (CONV_WIDTH - 1)
    win_rows = CONV_ROWS + CONV_HALO

    def conv(rows):
        for lt in range(D_CONV // LANES):
            lanes = slice(lt * LANES, (lt + 1) * LANES)
            win = ubuf[pl.ds(rows.start, win_rows), lanes]
            acc = jnp.broadcast_to(dwb_ref[:, lanes], (CONV_ROWS, LANES))
            for res in range(SUBLANES):
                shifted = win if res == 0 else pltpu.roll(win, win_rows - res, axis=0)
                for al in range(0, win_rows - CONV_ROWS + 1, SUBLANES):
                    kk = al + res - lead
                    if 0 <= kk < CONV_WIDTH and al + CONV_ROWS + res <= win_rows:
                        acc = acc + shifted[al:al + CONV_ROWS] * dwk_ref[kk:kk + 1, lanes]
            cv_s[rows, lanes] = acc

    row_loop(CONV_ROWS, conv)
    ubuf[0:CONV_HALO, :] = ubuf[t:t + CONV_HALO, :]

    conv_out = cv_s[...]
    centred = conv_out - jnp.mean(conv_out, axis=-1, keepdims=True)
    normed = centred * lax.rsqrt(jnp.mean(centred * centred, axis=-1, keepdims=True) + EPS)
    y_conv = jnp.dot(_silu(normed * lng_ref[...] + lnb_ref[...]).astype(BF16), wout_ref[0:D_CONV, :],
                     preferred_element_type=F32)

    l0 = lbl_ref[0:1, :]
    lmax = jnp.max(lbl_ref[...], axis=0, keepdims=True)
    lb = jnp.exp(l0 - lmax) / jnp.sum(jnp.exp(lbl_ref[...] - lmax), axis=0, keepdims=True)

    qf_s[...] = _silu(proj(q0, D_REC))
    forget = lb + (1.0 - lb) * _sigmoid(proj(f0, D_REC))
    k_s[...] = 1.0 - forget
    lf = jnp.log(forget)
    hi = lf.astype(BF16)
    rem = lf - hi.astype(F32)
    mid = rem.astype(BF16)
    lfp_s[0] = hi
    lfp_s[1] = mid
    lfp_s[2] = (rem - mid.astype(F32)).astype(BF16)
    v_s[...] = proj(i0, D_REC).astype(BF16)
    sg_s[...] = _silu(proj(g0, D_REC))

    row = lax.broadcasted_iota(I32, (CHUNK, CHUNK), 0)
    col = lax.broadcasted_iota(I32, (CHUNK, CHUNK), 1)
    causal = row >= col
    tri = jnp.where(causal, 1.0, 0.0).astype(BF16)
    for ci in range(n_chunks):
        rows = slice(ci * CHUNK, (ci + 1) * CHUNK)
        bcum = (jnp.dot(tri, lfp_s[0, rows, :], preferred_element_type=F32)
                + jnp.dot(tri, lfp_s[1, rows, :], preferred_element_type=F32)
                + jnp.dot(tri, lfp_s[2, rows, :], preferred_element_type=F32))
        a_last = jnp.exp(bcum[CHUNK - 1:CHUNK, :])
        k_inv = k_s[rows, :] * jnp.exp(-bcum)
        qd_s[rows, :] = (qf_s[rows, :] * jnp.exp(bcum)).astype(BF16)
        ki_s[rows, :] = k_inv.astype(BF16)
        ke_s[rows, :] = (k_inv * a_last).astype(BF16)
        a_s[ci] = a_last

    nt_dims = (((1,), (1,)), ((), ()))
    pairs = [(ci, hd) for ci in range(n_chunks) for hd in heads]

    def blk(ci, hd):
        return slice(ci * CHUNK, (ci + 1) * CHUNK), head_cols(0, hd)

    scores = {}
    for p in pairs:
        rows, cols = blk(*p)
        sc = lax.dot_general(qd_s[rows, cols], ki_s[rows, cols], nt_dims, preferred_element_type=F32)
        scores[p] = jnp.where(causal, sc, 0.0).astype(BF16)
    for p in pairs:
        rows, cols = blk(*p)
        o_s[rows, cols] = jnp.dot(scores[p], v_s[rows, cols], preferred_element_type=F32)
    upd = {}
    for p in pairs:
        rows, cols = blk(*p)
        v_t = v_s[rows, cols].astype(F32).T.astype(BF16)
        upd[p] = jnp.dot(v_t, ke_s[rows, cols], preferred_element_type=F32)
    prev = {}
    for hd in heads:
        st = state[hd]
        for ci in range(n_chunks):
            prev[(ci, hd)] = st.astype(BF16)
            st = st * a_s[ci, :, head_cols(0, hd)] + upd[(ci, hd)]
        state[hd] = st
    for p in pairs:
        rows, cols = blk(*p)
        o_s[rows, cols] += lax.dot_general(qd_s[rows, cols], prev[p], nt_dims, preferred_element_type=F32)

    for rows in stat_blocks:
        for hd in heads:
            o = _rms(o_s[rows, head_cols(0, hd)], rng_ref[...]) * sg_s[rows, head_cols(0, hd)]
            yb[rows, head_cols(0, hd)] = o.astype(BF16)

    y = y_conv + jnp.dot(yb[...], wout_ref[D_CONV:D_MODEL, :], preferred_element_type=F32)
    x1 = x_ref[0] + y * lax.rsqrt(jnp.mean(y * y, axis=-1, keepdims=True) + EPS) * vec_s[2:3, :]
    x1_ref[0] = x1
    h2v = x1 * lax.rsqrt(jnp.mean(x1 * x1, axis=-1, keepdims=True) + EPS) * vec_s[3:4, :] + vec_s[4:5, :]
    h2p_ref[0] = _pack_pair(h2v[:, 0:D_PACK], h2v[:, D_PACK:D_MODEL])

    h_hi = h2v.astype(BF16)
    h_lo = (h2v - h_hi.astype(F32)).astype(BF16)
    both = jnp.dot(h_hi, wr_ref[...], preferred_element_type=F32)
    logits = (both[:, 0:LANES] + both[:, LANES:2 * LANES]
              + jnp.dot(h_lo, wr_ref[:, 0:LANES], preferred_element_type=F32)) + br_ref[...]
    lt = logits.T
    neg = jnp.float32(-jnp.inf)
    r8 = lax.broadcasted_iota(I32, (SUBLANES, t), 0)
    gl = jnp.where(r8 < N_GROUPS, lt[GROUP_ROW0:GROUP_ROW0 + SUBLANES], neg)
    gmax = jnp.max(gl, axis=0, keepdims=True)
    gidx = jnp.min(jnp.where(gl == gmax, r8, SUBLANES), axis=0, keepdims=True)
    gprob = 1.0 / jnp.sum(jnp.exp(gl - gmax), axis=0, keepdims=True)
    re = lax.broadcasted_iota(I32, (N_EXPERTS, t), 0)
    el = jnp.where((re // EXPERTS_PER_GROUP) == gidx, lt[0:N_EXPERTS], neg)
    m1 = jnp.max(el, axis=0, keepdims=True)
    i1 = jnp.min(jnp.where(el == m1, re, N_EXPERTS), axis=0, keepdims=True)
    el2 = jnp.where(re == i1, neg, el)
    m2 = jnp.max(el2, axis=0, keepdims=True)
    i2 = jnp.min(jnp.where(el2 == m2, re, N_EXPERTS), axis=0, keepdims=True)
    r = jnp.exp(m2 - m1)
    w1 = gprob / (1.0 + r)
    w2 = gprob * r / (1.0 + r)
    hot1 = re == i1
    hot2 = re == i2
    hot = jnp.where(hot1 | hot2, 1.0, 0.0)
    prefix = jnp.dot(hot.astype(BF16), upper_ref[...], preferred_element_type=F32) + carry[...]
    rank1 = jnp.sum(jnp.where(hot1, prefix, 0.0), axis=0, keepdims=True)
    rank2 = jnp.sum(jnp.where(hot2, prefix, 0.0), axis=0, keepdims=True)
    carry[...] = carry[...] + jnp.sum(hot, axis=1, keepdims=True)
    zi = jnp.zeros((SUBLANES - 4, t), I32)
    ri_ref[0] = jnp.concatenate([i1, i2, rank1.astype(I32), rank2.astype(I32), zi], axis=0)
    rw_ref[0] = jnp.concatenate([w1, w2, jnp.zeros((SUBLANES - 2, t), F32)], axis=0)
    cnt_ref[...] = jnp.broadcast_to(carry[...], cnt_ref.shape)


def _mix(x, mod3, g_pre, g_post, g_ffn, w_in, dwk, dwb, lng, lnb, lbl, rng, w_out, w_r, b_r):
    bsz, s, d = x.shape
    t = SEQ_TILE
    nt = s // t
    tile = lambda b, j: (b, j, 0)
    rtile = lambda b, j: (b * nt + j, 0, 0)
    const2 = lambda b, j: (0, 0)

    def const_spec(shape):
        return pl.BlockSpec(shape, const2, pipeline_mode=pl.Buffered(1))

    upper = jnp.triu(jnp.ones((t, t), BF16), k=1)
    return pl.pallas_call(
        _mix_kernel,
        grid=(bsz, nt),
        in_specs=[
            pl.BlockSpec((1, t, d), tile),
            pl.BlockSpec((1, 6, d), lambda b, j: (b, 0, 0)),
            const_spec((1, d)),
            const_spec((1, d)),
            const_spec((1, d)),
            const_spec((d, D_IN)),
            const_spec((CONV_WIDTH, D_CONV)),
            const_spec((1, D_CONV)),
            const_spec((1, D_CONV)),
            const_spec((1, D_CONV)),
            const_spec((2, D_REC)),
            const_spec((1, REC_DV)),
            const_spec((d, d)),
            const_spec((d, 2 * LANES)),
            const_spec((1, LANES)),
            const_spec((t, t)),
        ],
        out_specs=[
            pl.BlockSpec((1, t, d), tile),
            pl.BlockSpec((1, t, D_PACK), tile),
            pl.BlockSpec((1, SUBLANES, t), rtile),
            pl.BlockSpec((1, SUBLANES, t), rtile),
            pl.BlockSpec((N_EXPERTS, LANES), const2),
        ],
        out_shape=[
            jax.ShapeDtypeStruct((bsz, s, d), F32),
            jax.ShapeDtypeStruct((bsz, s, D_PACK), U32),
            jax.ShapeDtypeStruct((bsz * nt, SUBLANES, t), I32),
            jax.ShapeDtypeStruct((bsz * nt, SUBLANES, t), F32),
            jax.ShapeDtypeStruct((N_EXPERTS, LANES), F32),
        ],
        scratch_shapes=[
            pltpu.VMEM((SUBLANES, d), F32),
            pltpu.VMEM((t, D_REC), F32),
            pltpu.VMEM((t, D_REC), F32),
            pltpu.VMEM((CONV_HALO + t, D_CONV), F32),
            pltpu.VMEM((t, D_CONV), F32),
            pltpu.VMEM((3, t, D_REC), BF16),
            pltpu.VMEM((t, D_REC), F32),
            pltpu.VMEM((t, D_REC), BF16),
            pltpu.VMEM((t, D_REC), BF16),
            pltpu.VMEM((t, D_REC), BF16),
            pltpu.VMEM((t, D_REC), BF16),
            pltpu.VMEM((t // CHUNK, 1, D_REC), F32),
            pltpu.VMEM((t, D_REC), F32),
            pltpu.VMEM((t, D_REC), BF16),
            pltpu.VMEM((REC_HEADS, REC_DV, REC_DK), F32),
            pltpu.VMEM((N_EXPERTS, 1), F32),
        ],
        compiler_params=pltpu.CompilerParams(
            dimension_semantics=("arbitrary", "arbitrary"),
            vmem_limit_bytes=VMEM_LIMIT),
        name="mixer",
    )(x, mod3, g_pre, g_post, g_ffn, w_in, dwk, dwb, lng, lnb, lbl, rng, w_out, w_r, b_r, upper)


def _meta_kernel(cnt_ref, start_ref, bexp_ref, bvalid_ref, nblk_ref):
    shift = EXPERT_ROWS.bit_length() - 1
    n_blocks = bexp_ref.shape[0]

    def fill(e, blk0, cnt):
        def body(jb, c_):
            bexp_ref[jb] = e
            bvalid_ref[jb] = jnp.clip(cnt - ((jb - blk0) << shift), 0, EXPERT_ROWS)
            return c_
        return body

    def per_expert(e, blk0):
        nb = (cnt_ref[e] + (EXPERT_ROWS - 1)) >> shift
        start_ref[e] = blk0 << shift
        lax.fori_loop(blk0, blk0 + nb, fill(e, blk0, cnt_ref[e]), 0)
        return blk0 + nb

    used = lax.fori_loop(0, N_EXPERTS, per_expert, jnp.int32(0))
    lax.fori_loop(used, n_blocks, fill(N_EXPERTS - 1, used, 0), 0)
    nblk_ref[0] = used


def _meta(counts, n_blocks):
    smem = pl.BlockSpec(memory_space=pltpu.SMEM)
    return pl.pallas_call(
        _meta_kernel,
        in_specs=[smem],
        out_specs=[smem, smem, smem, smem],
        out_shape=[
            jax.ShapeDtypeStruct((N_EXPERTS,), I32),
            jax.ShapeDtypeStruct((n_blocks,), I32),
            jax.ShapeDtypeStruct((n_blocks,), I32),
            jax.ShapeDtypeStruct((1,), I32),
        ],
        name="moe_layout",
    )(counts)


def _positions_kernel(start_ref, ri_ref, pos_ref):
    e = ri_ref[:, 0:TOP_K, :]
    seg = jnp.zeros(e.shape, I32)
    for k in range(N_EXPERTS):
        seg = jnp.where(e == k, start_ref[k], seg)
    pos = seg + ri_ref[:, TOP_K:2 * TOP_K, :]
    for k in range(TOP_K):
        pos_ref[k] = pos[:, k, :]


def _positions(starts, ri):
    n_tiles, _, t = ri.shape
    tb = SUBLANES
    return pl.pallas_call(
        _positions_kernel,
        grid_spec=pltpu.PrefetchScalarGridSpec(
            num_scalar_prefetch=1,
            grid=(n_tiles // tb,),
            in_specs=[pl.BlockSpec((tb, SUBLANES, t), lambda i, s_: (i, 0, 0))],
            out_specs=pl.BlockSpec((TOP_K, tb, t), lambda i, s_: (0, i, 0)),
        ),
        out_shape=jax.ShapeDtypeStruct((TOP_K, n_tiles, t), I32),
        name="moe_positions",
    )(starts, ri).reshape(TOP_K, n_tiles * t)


_SC_MESH = dict(core_axis_name="core", subcore_axis_name="subcore")


def _sc_worker_base(rows_per_worker):
    wid = lax.axis_index("core") * (SC_WORKERS // 2) + lax.axis_index("subcore")
    return wid * rows_per_worker


def _sc_dispatch(rows_in, pos, n_rows):
    n, d = rows_in.shape
    w = SC_WINDOW
    per_worker = n // SC_WORKERS
    chunks = per_worker // SC_INDEX_CHUNK
    windows = SC_INDEX_CHUNK // w
    dma = pltpu.SemaphoreType.DMA

    @pl.kernel(out_type=jax.ShapeDtypeStruct((n_rows, d), rows_in.dtype),
               mesh=plsc.VectorSubcoreMesh(**_SC_MESH),
               scratch_types=[pltpu.VMEM((SC_INDEX_CHUNK,), I32), pltpu.VMEM((SC_INDEX_CHUNK,), I32)]
               + [pltpu.VMEM((w, d), rows_in.dtype)] * windows + [dma] * (3 * windows),
               name="moe_dispatch_sc")
    def run(x_hbm, i_hbm, o_hbm, ia_v, ib_v, *rest):
        bufs, sems = rest[:windows], rest[windows:]
        base = _sc_worker_base(per_worker)

        @pl.loop(0, chunks)
        def _(c):
            row0 = base + c * SC_INDEX_CHUNK
            loads = [pltpu.make_async_copy(x_hbm.at[pl.ds(row0 + k * w, w)], bufs[k], sems[3 * k])
                     for k in range(windows)]
            for cp in loads:
                cp.start()
            pltpu.sync_copy(i_hbm.at[0, pl.ds(row0, SC_INDEX_CHUNK)], ia_v)
            pltpu.sync_copy(i_hbm.at[1, pl.ds(row0, SC_INDEX_CHUNK)], ib_v)
            stores = []
            for k in range(windows):
                loads[k].wait()
                for idx_v, sem in ((ia_v, sems[3 * k + 1]), (ib_v, sems[3 * k + 2])):
                    cp = pltpu.make_async_copy(bufs[k], o_hbm.at[idx_v.at[pl.ds(k * w, w)]], sem)
                    cp.start()
                    stores.append(cp)
            for cp in stores:
                cp.wait()

    return run(rows_in, pos)


def _sc_gather(table, pos, tok0, n_piece):
    m = TOP_K * n_piece
    d = table.shape[1]
    w = SC_WINDOW
    per_worker = m // SC_WORKERS
    chunks = per_worker // SC_INDEX_CHUNK
    windows = SC_INDEX_CHUNK // w
    workers_per_slot = SC_WORKERS // TOP_K
    dma = pltpu.SemaphoreType.DMA

    @pl.kernel(out_type=jax.ShapeDtypeStruct((m, d), table.dtype),
               mesh=plsc.VectorSubcoreMesh(**_SC_MESH),
               scratch_types=[pltpu.VMEM((SC_INDEX_CHUNK,), I32)]
               + [pltpu.VMEM((w, d), table.dtype)] * windows + [dma] * (2 * windows),
               name="moe_gather_sc")
    def run(x_hbm, i_hbm, o_hbm, i_v, *rest):
        bufs, sems = rest[:windows], rest[windows:]
        wid = lax.axis_index("core") * (SC_WORKERS // 2) + lax.axis_index("subcore")
        slot = wid // workers_per_slot
        src0 = tok0 + (wid % workers_per_slot) * per_worker

        @pl.loop(0, chunks)
        def _(c):
            pltpu.sync_copy(i_hbm.at[slot, pl.ds(src0 + c * SC_INDEX_CHUNK, SC_INDEX_CHUNK)], i_v)
            row0 = wid * per_worker + c * SC_INDEX_CHUNK
            gathers = [pltpu.make_async_copy(x_hbm.at[i_v.at[pl.ds(k * w, w)]], bufs[k], sems[2 * k])
                       for k in range(windows)]
            for cp in gathers:
                cp.start()
            stores = []
            for k in range(windows):
                gathers[k].wait()
                cp = pltpu.make_async_copy(bufs[k], o_hbm.at[pl.ds(row0 + k * w, w)], sems[2 * k + 1])
                cp.start()
                stores.append(cp)
            for cp in stores:
                cp.wait()

    return run(table, pos)


def _expert_kernel(bexp_ref, bvalid_ref, nblk_ref, x_ref, wg_ref, wu_ref, wd_ref, y_ref):
    jb = pl.program_id(0)
    valid = bvalid_ref[jb]
    subs =[slice(r0, r0 + EXPERT_SUB) for r0 in range(0, EXPERT_ROWS, EXPERT_SUB)]
    passes = (valid + (EXPERT_SUB - 1)) // EXPERT_SUB

    def run(n_live):
        halves = []
        for rows in subs[:n_live]:
            live = lax.broadcasted_iota(I32, (EXPERT_SUB, 1), 0) + rows.start < valid
            lo, hi = _unpack_pair(jnp.where(live, x_ref[rows, :], jnp.uint32(0)))
            halves.append((lo.astype(BF16), hi.astype(BF16)))

        def project(w_ref):
            return [jnp.dot(lo, w_ref[0, 0:D_PACK, :], preferred_element_type=F32)
                    + jnp.dot(hi, w_ref[0, D_PACK:D_MODEL, :], preferred_element_type=F32) for lo, hi in halves]

        hidden = [(_silu(g) * u).astype(BF16) for g, u in zip(project(wg_ref), project(wu_ref))]
        outs = [jnp.dot(hb, wd_ref[0], preferred_element_type=F32) for hb in hidden]
        for rows, y in zip(subs, outs):
            y_ref[rows, :] = _pack_pair(y[:, 0:D_PACK], y[:, D_PACK:D_MODEL])
        for rows in subs[n_live:]:
            y_ref[rows, :] = jnp.zeros((EXPERT_SUB, D_PACK), U32)

    for n_live in range(len(subs) + 1):
        pl.when(passes == n_live)(lambda n_live=n_live: run(n_live))


def _experts(bexp, bvalid, nblk, xs, w_gate, w_up, w_down):
    n_rows, dp = xs.shape
    d = 2 * dp
    n_blocks = n_rows // EXPERT_ROWS
    return pl.pallas_call(
        _expert_kernel,
        grid_spec=pltpu.PrefetchScalarGridSpec(
            num_scalar_prefetch=3,
            grid=(n_blocks,),
            in_specs=[
                pl.BlockSpec((EXPERT_ROWS, dp), lambda jb, be, bv, nb: (jnp.minimum(jb, nb[0] - 1), 0)),
                pl.BlockSpec((1, d, D_EXPERT), lambda jb, be, bv, nb: (be[jb], 0, 0)),
                pl.BlockSpec((1, d, D_EXPERT), lambda jb, be, bv, nb: (be[jb], 0, 0)),
                pl.BlockSpec((1, D_EXPERT, d), lambda jb, be, bv, nb: (be[jb], 0, 0)),
            ],
            out_specs=pl.BlockSpec((EXPERT_ROWS, dp), lambda jb, be, bv, nb: (jb, 0)),
        ),
        out_shape=jax.ShapeDtypeStruct((n_rows, dp), U32),
        compiler_params=pltpu.CompilerParams(dimension_semantics=("arbitrary",)),
        name="moe_experts",
    )(bexp, bvalid, nblk, xs, w_gate, w_up, w_down)


def _combine_kernel(rw_ref, x1_ref, gt2_ref, gpost_ref, ya_ref, yb_ref, o_ref):
    wt = rw_ref[0].T
    a_lo, a_hi = _unpack_pair(ya_ref[...])
    b_lo, b_hi = _unpack_pair(yb_ref[...])
    y_lo = a_lo * wt[:, 0:1] + b_lo * wt[:, 1:2]
    y_hi = a_hi * wt[:, 0:1] + b_hi * wt[:, 1:2]
    ssq = jnp.sum(y_lo * y_lo, axis=-1, keepdims=True) + jnp.sum(y_hi * y_hi, axis=-1, keepdims=True)
    inv = lax.rsqrt(ssq * (1.0 / D_MODEL) + EPS)
    gate = gt2_ref[0] * gpost_ref[...]
    o_ref[:, 0:D_PACK] = x1_ref[:, 0:D_PACK] + y_lo * inv * gate[:, 0:D_PACK]
    o_ref[:, D_PACK:D_MODEL] = x1_ref[:, D_PACK:D_MODEL] + y_hi * inv * gate[:, D_PACK:D_MODEL]


def _combine_kernel_into(prev_ref, *refs):
    del prev_ref
    _combine_kernel(*refs)


def _combine(rw, x1, gt2, g_post, yg, seq, piece, prev):
    n, d = x1.shape
    t = SEQ_TILE
    per_batch = seq // t
    tiles = yg.shape[0] // (TOP_K * t)
    tile0 = piece * tiles
    in_specs = [
        pl.BlockSpec((1, SUBLANES, t), lambda i: (i + tile0, 0, 0)),
        pl.BlockSpec((t, d), lambda i: (i + tile0, 0)),
        pl.BlockSpec((1, 1, d), lambda i: ((i + tile0) // per_batch, 0, 0)),
        pl.BlockSpec((1, d), lambda i: (0, 0)),
        pl.BlockSpec((t, D_PACK), lambda i: (i, 0)),
        pl.BlockSpec((t, D_PACK), lambda i: (i + tiles, 0)),
    ]
    args = (rw, x1, gt2, g_post, yg, yg)
    body, aliases = _combine_kernel, {}
    if prev is not None:
        in_specs = [pl.BlockSpec(memory_space=pl.ANY)] + in_specs
        args = (prev,) + args
        body, aliases = _combine_kernel_into, {0: 0}
    return pl.pallas_call(
        body,
        grid=(tiles,),
        in_specs=in_specs,
        out_specs=pl.BlockSpec((t, d), lambda i: (i + tile0, 0)),
        out_shape=jax.ShapeDtypeStruct((n, d), F32),
        input_output_aliases=aliases,
        compiler_params=pltpu.CompilerParams(dimension_semantics=("arbitrary",)),
        name="moe_combine",
    )(*args)


def kernel(x, c, w_ada, b_ada, g_pre_mix, g_post_mix, w_in, dw_kernel, dw_bias, conv_ln_gain, conv_ln_bias, lb_logits, rec_norm_gain, w_out, g_pre_ffn, g_post_ffn, w_router_group, b_router_group, w_router_expert, b_router_expert, w_gate, w_up, w_down):
    bsz, s, d = x.shape
    depth = w_ada.shape[0]
    assert depth == 1 and lb_logits.shape[0] == 2
    n_tok = bsz * s
    n_rows = n_tok * TOP_K + N_EXPERTS * EXPERT_ROWS
    for l in range(depth):
        mod = _ada(c, w_ada[l], b_ada[l])
        mod3 = mod.reshape(bsz, 6, d)
        pad = LANES - N_EXPERTS - N_GROUPS
        w_r = jnp.concatenate([w_router_expert[l], w_router_group[l], jnp.zeros((d, pad), F32)], axis=1)
        w_r_hi = w_r.astype(BF16)
        w_r = jnp.concatenate([w_r_hi, (w_r - w_r_hi.astype(F32)).astype(BF16)], axis=1)
        b_r = jnp.concatenate([b_router_expert[l], b_router_group[l], jnp.zeros((pad,), F32)]).reshape(1, LANES)
        x1, h2p, ri, rw, cnt = _mix(
            x, mod3, g_pre_mix[l].reshape(1, d), g_post_mix[l].reshape(1, d), g_pre_ffn[l].reshape(1, d),
            w_in[l].astype(BF16), dw_kernel[l], dw_bias[l].reshape(1, D_CONV),
            conv_ln_gain[l].reshape(1, D_CONV), conv_ln_bias[l].reshape(1, D_CONV),
            lb_logits, rec_norm_gain[l].reshape(1, REC_DV), w_out[l].astype(BF16), w_r, b_r)
        counts = cnt[:, 0].astype(I32)
        starts, bexp, bvalid, nblk = _meta(counts, n_rows // EXPERT_ROWS)
        pos = _positions(starts, ri)
        xs = _sc_dispatch(h2p.reshape(n_tok, D_PACK), pos, n_rows)
        ys = _experts(bexp, bvalid, nblk, xs, w_gate[l].astype(BF16), w_up[l].astype(BF16), w_down[l].astype(BF16))
        piece_tok = n_tok // COMBINE_PIECES
        out = None
        for p in range(COMBINE_PIECES):
            yg = _sc_gather(ys, pos, p * piece_tok, piece_tok)
            out = _combine(rw, x1.reshape(n_tok, d), mod3[:, 5:6, :], g_post_ffn[l].reshape(1, d), yg, s, p, out)
        x = out.reshape(bsz, s, d)
    return x
```

```python
import jax
import jax.numpy as jnp
from jax import lax
from jax.experimental import pallas as pl
from jax.experimental.pallas import tpu as pltpu
from jax.experimental.pallas import tpu_sc as plsc

D_MODEL = 1024
D_CONV = 512
D_REC = 512
CONV_WIDTH = 31
REC_HEADS = 4
REC_DK = 128
REC_DV = 128
CHUNK = 64
D_IN = 2 * D_CONV + 4 * D_REC
N_GROUPS = 4
EXPERTS_PER_GROUP = 8
N_EXPERTS = 32
TOP_K = 2
D_EXPERT = 256
EPS = 1e-6

LANES = 128
SUBLANES = 8
SEQ_TILE = 1024
CONV_ROWS = 64
STAT_ROWS = 64
CONV_HALO = 32
GROUP_ROW0 = 32
EXPERT_ROWS = 1024
COMBINE_PIECES = 8
EXPERT_SUB = 256
SC_WINDOW = 64
SC_WORKERS = 32
SC_INDEX_CHUNK = 128
D_PACK = D_MODEL // 2
VMEM_LIMIT = 56 * 1024 * 1024

F32 = jnp.float32
BF16 = jnp.bfloat16
I32 = jnp.int32
U32 = jnp.uint32
HI = lax.Precision.HIGHEST


def _sigmoid(v):
    return 0.5 * jnp.tanh(0.5 * v) + 0.5


def _silu(v):
    return v * _sigmoid(v)


def _rms(v, gain):
    return v * lax.rsqrt(jnp.mean(v * v, axis=-1, keepdims=True) + EPS) * gain


def _pack_pair(lo, hi):
    lo_bits = lax.bitcast_convert_type(lo.astype(BF16).astype(F32), U32)
    hi_bits = lax.bitcast_convert_type(hi.astype(BF16).astype(F32), U32)
    return (lo_bits >> 16) | (hi_bits & jnp.uint32(0xFFFF0000))


def _unpack_pair(packed):
    lo = lax.bitcast_convert_type(packed << 16, F32)
    hi = lax.bitcast_convert_type(packed & jnp.uint32(0xFFFF0000), F32)
    return lo, hi


def _ada_kernel(c_ref, w_ref, b_ref, o_ref):
    cond = _silu(c_ref[...])
    o_ref[...] = jnp.dot(cond, w_ref[...], precision=HI, preferred_element_type=F32) + b_ref[...]


def _ada(c, w, b):
    bsz, d = c.shape
    n = w.shape[1]
    return pl.pallas_call(
        _ada_kernel,
        grid=(n // d,),
        in_specs=[
            pl.BlockSpec((bsz, d), lambda j: (0, 0)),
            pl.BlockSpec((d, d), lambda j: (0, j)),
            pl.BlockSpec((1, d), lambda j: (0, j)),
        ],
        out_specs=pl.BlockSpec((bsz, d), lambda j: (0, j)),
        out_shape=jax.ShapeDtypeStruct((bsz, n), F32),
        name="ada_mod",
    )(c, w, b.reshape(1, n))


def _mix_kernel(x_ref, mod_ref, gpre_ref, gpost_ref, gffn_ref, win_ref, dwk_ref, dwb_ref,
                lng_ref, lnb_ref, lbl_ref, rng_ref, wout_ref, wr_ref, br_ref, upper_ref,
                x1_ref, h2p_ref, ri_ref, rw_ref, cnt_ref,
                vec_s, qf_s, sg_s, ubuf, cv_s, lfp_s, k_s, qd_s, ki_s, ke_s, v_s, a_s, o_s, yb,
                state, carry):
    b = pl.program_id(0)
    j = pl.program_id(1)
    t = SEQ_TILE
    n_chunks = t // CHUNK
    heads = range(REC_HEADS)

    @pl.when(j == 0)
    def _():
        ubuf[0:CONV_HALO, :] = jnp.zeros((CONV_HALO, D_CONV), F32)
        state[...] = jnp.zeros(state.shape, F32)

    @pl.when((j == 0) & (b == 0))
    def _():
        carry[...] = jnp.zeros(carry.shape, F32)

    sh1 = mod_ref[0, 0:1, :]
    sc1 = mod_ref[0, 1:2, :]
    gt1 = mod_ref[0, 2:3, :]
    sh2 = mod_ref[0, 3:4, :]
    sc2 = mod_ref[0, 4:5, :]

    def row_loop(rows_per_step, body):
        def step(ci, c_):
            body(pl.ds(pl.multiple_of(ci * rows_per_step, rows_per_step), rows_per_step))
            return c_
        lax.fori_loop(0, t // rows_per_step, step, 0)

    stat_blocks = [slice(r0, r0 + STAT_ROWS) for r0 in range(0, t, STAT_ROWS)]

    def head_cols(base, hd):
        return slice(base + hd * REC_DK, base + (hd + 1) * REC_DK)

    vec_s[0:1, :] = gpre_ref[...] * (1.0 + sc1)
    vec_s[1:2, :] = sh1
    vec_s[2:3, :] = gt1 * gpost_ref[...]
    vec_s[3:4, :] = gffn_ref[...] * (1.0 + sc2)
    vec_s[4:5, :] = sh2

    xv = x_ref[0]
    hb = (xv * lax.rsqrt(jnp.mean(xv * xv, axis=-1, keepdims=True) + EPS) * vec_s[0:1, :] + vec_s[1:2, :]).astype(BF16)
    q0 = 2 * D_CONV
    f0 = q0 + D_REC
    i0 = f0 + D_REC
    g0 = i0 + D_REC

    def proj(c0, width):
        return jnp.dot(hb, win_ref[:, c0:c0 + width], preferred_element_type=F32)

    conv_in = proj(0, 2 * D_CONV)
    ubuf[CONV_HALO:CONV_HALO + t, :] = conv_in[:, 0:D_CONV] * _sigmoid(conv_in[:, D_CONV:2 * D_CONV])

    lead = CONV_HALO - (CONV_WIDTH - 1)
    win_rows = CONV_ROWS + CONV_HALO

    def conv(rows):
        for lt in range(D_CONV // LANES):
            lanes = slice(lt * LANES, (lt + 1) * LANES)
            win = ubuf[pl.ds(rows.start, win_rows), lanes]
            acc = jnp.broadcast_to(dwb_ref[:, lanes], (CONV_ROWS, LANES))
            for res in range(SUBLANES):
                shifted = win if res == 0 else pltpu.roll(win, win_rows - res, axis=0)
                for al in range(0, win_rows - CONV_ROWS + 1, SUBLANES):
                    kk = al + res - lead
                    if 0 <= kk < CONV_WIDTH and al + CONV_ROWS + res <= win_rows:
                        acc = acc + shifted[al:al + CONV_ROWS] * dwk_ref[kk:kk + 1, lanes]
            cv_s[rows, lanes] = acc

    row_loop(CONV_ROWS, conv)
    ubuf[0:CONV_HALO, :] = ubuf[t:t + CONV_HALO, :]

    conv_out = cv_s[...]
    centred = conv_out - jnp.mean(conv_out, axis=-1, keepdims=True)
    normed = centred * lax.rsqrt(jnp.mean(centred * centred, axis=-1, keepdims=True) + EPS)
    y_conv = jnp.dot(_silu(normed * lng_ref[...] + lnb_ref[...]).astype(BF16), wout_ref[0:D_CONV, :],
                     preferred_element_type=F32)

    l0 = lbl_ref[0:1, :]
    lmax = jnp.max(lbl_ref[...], axis=0, keepdims=True)
    lb = jnp.exp(l0 - lmax) / jnp.sum(jnp.exp(lbl_ref[...] - lmax), axis=0, keepdims=True)

    qf_s[...] = _silu(proj(q0, D_REC))
    forget = lb + (1.0 - lb) * _sigmoid(proj(f0, D_REC))
    k_s[...] = 1.0 - forget
    lf = jnp.log(forget)
    hi = lf.astype(BF16)
    rem = lf - hi.astype(F32)
    mid = rem.astype(BF16)
    lfp_s[0] = hi
    lfp_s[1] = mid
    lfp_s[2] = (rem - mid.astype(F32)).astype(BF16)
    v_s[...] = proj(i0, D_REC).astype(BF16)
    sg_s[...] = _silu(proj(g0, D_REC))

    row = lax.broadcasted_iota(I32, (CHUNK, CHUNK), 0)
    col = lax.broadcasted_iota(I32, (CHUNK, CHUNK), 1)
    causal = row >= col
    tri = jnp.where(causal, 1.0, 0.0).astype(BF16)
    for ci in range(n_chunks):
        rows = slice(ci * CHUNK, (ci + 1) * CHUNK)
        bcum = (jnp.dot(tri, lfp_s[0, rows, :], preferred_element_type=F32)
                + jnp.dot(tri, lfp_s[1, rows, :], preferred_element_type=F32)
                + jnp.dot(tri, lfp_s[2, rows, :], preferred_element_type=F32))
        a_last = jnp.exp(bcum[CHUNK - 1:CHUNK, :])
        k_inv = k_s[rows, :] * jnp.exp(-bcum)
        qd_s[rows, :] = (qf_s[rows, :] * jnp.exp(bcum)).astype(BF16)
        ki_s[rows, :] = k_inv.astype(BF16)
        ke_s[rows, :] = (k_inv * a_last).astype(BF16)
        a_s[ci] = a_last

    nt_dims = (((1,), (1,)), ((), ()))
    pairs = [(ci, hd) for ci in range(n_chunks) for hd in heads]

    def blk(ci, hd):
        return slice(ci * CHUNK, (ci + 1) * CHUNK), head_cols(0, hd)

    scores = {}
    for p in pairs:
        rows, cols = blk(*p)
        sc = lax.dot_general(qd_s[rows, cols], ki_s[rows, cols], nt_dims, preferred_element_type=F32)
        scores[p] = jnp.where(causal, sc, 0.0).astype(BF16)
    for p in pairs:
        rows, cols = blk(*p)
        o_s[rows, cols] = jnp.dot(scores[p], v_s[rows, cols], preferred_element_type=F32)
    upd = {}
    for p in pairs:
        rows, cols = blk(*p)
        v_t = v_s[rows, cols].astype(F32).T.astype(BF16)
        upd[p] = jnp.dot(v_t, ke_s[rows, cols], preferred_element_type=F32)
    prev = {}
    for hd in heads:
        st = state[hd]
        for ci in range(n_chunks):
            prev[(ci, hd)] = st.astype(BF16)
            st = st * a_s[ci, :, head_cols(0, hd)] + upd[(ci, hd)]
        state[hd] = st
    for p in pairs:
        rows, cols = blk(*p)
        o_s[rows, cols] += lax.dot_general(qd_s[rows, cols], prev[p], nt_dims, preferred_element_type=F32)

    for rows in stat_blocks:
        for hd in heads:
            o = _rms(o_s[rows, head_cols(0, hd)], rng_ref[...]) * sg_s[rows, head_cols(0, hd)]
            yb[rows, head_cols(0, hd)] = o.astype(BF16)

    y = y_conv + jnp.dot(yb[...], wout_ref[D_CONV:D_MODEL, :], preferred_element_type=F32)
    x1 = x_ref[0] + y * lax.rsqrt(jnp.mean(y * y, axis=-1, keepdims=True) + EPS) * vec_s[2:3, :]
    x1_ref[0] = x1
    h2v = x1 * lax.rsqrt(jnp.mean(x1 * x1, axis=-1, keepdims=True) + EPS) * vec_s[3:4, :] + vec_s[4:5, :]
    h2p_ref[0] = _pack_pair(h2v[:, 0:D_PACK], h2v[:, D_PACK:D_MODEL])

    h_hi = h2v.astype(BF16)
    h_lo = (h2v - h_hi.astype(F32)).astype(BF16)
    both = jnp.dot(h_hi, wr_ref[...], preferred_element_type=F32)
    logits = (both[:, 0:LANES] + both[:, LANES:2 * LANES]
              + jnp.dot(h_lo, wr_ref[:, 0:LANES], preferred_element_type=F32)) + br_ref[...]
    lt = logits.T
    neg = jnp.float32(-jnp.inf)
    r8 = lax.broadcasted_iota(I32, (SUBLANES, t), 0)
    gl = jnp.where(r8 < N_GROUPS, lt[GROUP_ROW0:GROUP_ROW0 + SUBLANES], neg)
    gmax = jnp.max(gl, axis=0, keepdims=True)
    gidx = jnp.min(jnp.where(gl == gmax, r8, SUBLANES), axis=0, keepdims=True)
    gprob = 1.0 / jnp.sum(jnp.exp(gl - gmax), axis=0, keepdims=True)
    re = lax.broadcasted_iota(I32, (N_EXPERTS, t), 0)
    el = jnp.where((re // EXPERTS_PER_GROUP) == gidx, lt[0:N_EXPERTS], neg)
    m1 = jnp.max(el, axis=0, keepdims=True)
    i1 = jnp.min(jnp.where(el == m1, re, N_EXPERTS), axis=0, keepdims=True)
    el2 = jnp.where(re == i1, neg, el)
    m2 = jnp.max(el2, axis=0, keepdims=True)
    i2 = jnp.min(jnp.where(el2 == m2, re, N_EXPERTS), axis=0, keepdims=True)
    r = jnp.exp(m2 - m1)
    w1 = gprob / (1.0 + r)
    w2 = gprob * r / (1.0 + r)
    hot1 = re == i1
    hot2 = re == i2
    hot = jnp.where(hot1 | hot2, 1.0, 0.0)
    prefix = jnp.dot(hot.astype(BF16), upper_ref[...], preferred_element_type=F32) + carry[...]
    rank1 = jnp.sum(jnp.where(hot1, prefix, 0.0), axis=0, keepdims=True)
    rank2 = jnp.sum(jnp.where(hot2, prefix, 0.0), axis=0, keepdims=True)
    carry[...] = carry[...] + jnp.sum(hot, axis=1, keepdims=True)
    zi = jnp.zeros((SUBLANES - 4, t), I32)
    ri_ref[0] = jnp.concatenate([i1, i2, rank1.astype(I32), rank2.astype(I32), zi], axis=0)
    rw_ref[0] = jnp.concatenate([w1, w2, jnp.zeros((SUBLANES - 2, t), F32)], axis=0)
    cnt_ref[...] = jnp.broadcast_to(carry[...], cnt_ref.shape)


def _mix(x, mod3, g_pre, g_post, g_ffn, w_in, dwk, dwb, lng, lnb, lbl, rng, w_out, w_r, b_r):
    bsz, s, d = x.shape
    t = SEQ_TILE
    nt = s // t
    tile = lambda b, j: (b, j, 0)
    rtile = lambda b, j: (b * nt + j, 0, 0)
    const2 = lambda b, j: (0, 0)

    def const_spec(shape):
        return pl.BlockSpec(shape, const2, pipeline_mode=pl.Buffered(1))

    upper = jnp.triu(jnp.ones((t, t), BF16), k=1)
    return pl.pallas_call(
        _mix_kernel,
        grid=(bsz, nt),
        in_specs=[
            pl.BlockSpec((1, t, d), tile),
            pl.BlockSpec((1, 6, d), lambda b, j: (b, 0, 0)),
            const_spec((1, d)),
            const_spec((1, d)),
            const_spec((1, d)),
            const_spec((d, D_IN)),
            const_spec((CONV_WIDTH, D_CONV)),
            const_spec((1, D_CONV)),
            const_spec((1, D_CONV)),
            const_spec((1, D_CONV)),
            const_spec((2, D_REC)),
            const_spec((1, REC_DV)),
            const_spec((d, d)),
            const_spec((d, 2 * LANES)),
            const_spec((1, LANES)),
            const_spec((t, t)),
        ],
        out_specs=[
            pl.BlockSpec((1, t, d), tile),
            pl.BlockSpec((1, t, D_PACK), tile),
            pl.BlockSpec((1, SUBLANES, t), rtile),
            pl.BlockSpec((1, SUBLANES, t), rtile),
            pl.BlockSpec((N_EXPERTS, LANES), const2),
        ],
        out_shape=[
            jax.ShapeDtypeStruct((bsz, s, d), F32),
            jax.ShapeDtypeStruct((bsz, s, D_PACK), U32),
            jax.ShapeDtypeStruct((bsz * nt, SUBLANES, t), I32),
            jax.ShapeDtypeStruct((bsz * nt, SUBLANES, t), F32),
            jax.ShapeDtypeStruct((N_EXPERTS, LANES), F32),
        ],
        scratch_shapes=[
            pltpu.VMEM((SUBLANES, d), F32),
            pltpu.VMEM((t, D_REC), F32),
            pltpu.VMEM((t, D_REC), F32),
            pltpu.VMEM((CONV_HALO + t, D_CONV), F32),
            pltpu.VMEM((t, D_CONV), F32),
            pltpu.VMEM((3, t, D_REC), BF16),
            pltpu.VMEM((t, D_REC), F32),
            pltpu.VMEM((t, D_REC), BF16),
            pltpu.VMEM((t, D_REC), BF16),
            pltpu.VMEM((t, D_REC), BF16),
            pltpu.VMEM((t, D_REC), BF16),
            pltpu.VMEM((t // CHUNK, 1, D_REC), F32),
            pltpu.VMEM((t, D_REC), F32),
            pltpu.VMEM((t, D_REC), BF16),
            pltpu.VMEM((REC_HEADS, REC_DV, REC_DK), F32),
            pltpu.VMEM((N_EXPERTS, 1), F32),
        ],
        compiler_params=pltpu.CompilerParams(
            dimension_semantics=("arbitrary", "arbitrary"),
            vmem_limit_bytes=VMEM_LIMIT),
        name="mixer",
    )(x, mod3, g_pre, g_post, g_ffn, w_in, dwk, dwb, lng, lnb, lbl, rng, w_out, w_r, b_r, upper)


def _meta_kernel(cnt_ref, start_ref, bexp_ref, bvalid_ref, nblk_ref):
    shift = EXPERT_ROWS.bit_length() - 1
    n_blocks = bexp_ref.shape[0]

    def fill(e, blk0, cnt):
        def body(jb, c_):
            bexp_ref[jb] = e
            bvalid_ref[jb] = jnp.clip(cnt - ((jb - blk0) << shift), 0, EXPERT_ROWS)
            return c_
        return body

    def per_expert(e, blk0):
        nb = (cnt_ref[e] + (EXPERT_ROWS - 1)) >> shift
        start_ref[e] = blk0 << shift
        lax.fori_loop(blk0, blk0 + nb, fill(e, blk0, cnt_ref[e]), 0)
        return blk0 + nb

    used = lax.fori_loop(0, N_EXPERTS, per_expert, jnp.int32(0))
    lax.fori_loop(used, n_blocks, fill(N_EXPERTS - 1, used, 0), 0)
    nblk_ref[0] = used


def _meta(counts, n_blocks):
    smem = pl.BlockSpec(memory_space=pltpu.SMEM)
    return pl.pallas_call(
        _meta_kernel,
        in_specs=[smem],
        out_specs=[smem, smem, smem, smem],
        out_shape=[
            jax.ShapeDtypeStruct((N_EXPERTS,), I32),
            jax.ShapeDtypeStruct((n_blocks,), I32),
            jax.ShapeDtypeStruct((n_blocks,), I32),
            jax.ShapeDtypeStruct((1,), I32),
        ],
        name="moe_layout",
    )(counts)


def _positions_kernel(start_ref, ri_ref, pos_ref):
    e = ri_ref[:, 0:TOP_K, :]
    seg = jnp.zeros(e.shape, I32)
    for k in range(N_EXPERTS):
        seg = jnp.where(e == k, start_ref[k], seg)
    pos = seg + ri_ref[:, TOP_K:2 * TOP_K, :]
    for k in range(TOP_K):
        pos_ref[k] = pos[:, k, :]


def _positions(starts, ri):
    n_tiles, _, t = ri.shape
    tb = SUBLANES
    return pl.pallas_call(
        _positions_kernel,
        grid_spec=pltpu.PrefetchScalarGridSpec(
            num_scalar_prefetch=1,
            grid=(n_tiles // tb,),
            in_specs=[pl.BlockSpec((tb, SUBLANES, t), lambda i, s_: (i, 0, 0))],
            out_specs=pl.BlockSpec((TOP_K, tb, t), lambda i, s_: (0, i, 0)),
        ),
        out_shape=jax.ShapeDtypeStruct((TOP_K, n_tiles, t), I32),
        name="moe_positions",
    )(starts, ri).reshape(TOP_K, n_tiles * t)


_SC_MESH = dict(core_axis_name="core", subcore_axis_name="subcore")


def _sc_worker_base(rows_per_worker):
    wid = lax.axis_index("core") * (SC_WORKERS // 2) + lax.axis_index("subcore")
    return wid * rows_per_worker


def _sc_dispatch(rows_in, pos, n_rows):
    n, d = rows_in.shape
    w = SC_WINDOW
    per_worker = n // SC_WORKERS
    chunks = per_worker // SC_INDEX_CHUNK
    windows = SC_INDEX_CHUNK // w
    dma = pltpu.SemaphoreType.DMA

    @pl.kernel(out_type=jax.ShapeDtypeStruct((n_rows, d), rows_in.dtype),
               mesh=plsc.VectorSubcoreMesh(**_SC_MESH),
               scratch_types=[pltpu.VMEM((SC_INDEX_CHUNK,), I32), pltpu.VMEM((SC_INDEX_CHUNK,), I32)]
               + [pltpu.VMEM((w, d), rows_in.dtype)] * windows + [dma] * (3 * windows),
               name="moe_dispatch_sc")
    def run(x_hbm, i_hbm, o_hbm, ia_v, ib_v, *rest):
        bufs, sems = rest[:windows], rest[windows:]
        base = _sc_worker_base(per_worker)

        @pl.loop(0, chunks)
        def _(c):
            row0 = base + c * SC_INDEX_CHUNK
            loads = [pltpu.make_async_copy(x_hbm.at[pl.ds(row0 + k * w, w)], bufs[k], sems[3 * k])
                     for k in range(windows)]
            for cp in loads:
                cp.start()
            pltpu.sync_copy(i_hbm.at[0, pl.ds(row0, SC_INDEX_CHUNK)], ia_v)
            pltpu.sync_copy(i_hbm.at[1, pl.ds(row0, SC_INDEX_CHUNK)], ib_v)
            stores = []
            for k in range(windows):
                loads[k].wait()
                for idx_v, sem in ((ia_v, sems[3 * k + 1]), (ib_v, sems[3 * k + 2])):
                    cp = pltpu.make_async_copy(bufs[k], o_hbm.at[idx_v.at[pl.ds(k * w, w)]], sem)
                    cp.start()
                    stores.append(cp)
            for cp in stores:
                cp.wait()

    return run(rows_in, pos)


def _sc_gather(table, pos, tok0, n_piece):
    m = TOP_K * n_piece
    d = table.shape[1]
    w = SC_WINDOW
    per_worker = m // SC_WORKERS
    chunks = per_worker // SC_INDEX_CHUNK
    windows = SC_INDEX_CHUNK // w
    workers_per_slot = SC_WORKERS // TOP_K
    dma = pltpu.SemaphoreType.DMA

    @pl.kernel(out_type=jax.ShapeDtypeStruct((m, d), table.dtype),
               mesh=plsc.VectorSubcoreMesh(**_SC_MESH),
               scratch_types=[pltpu.VMEM((SC_INDEX_CHUNK,), I32)]
               + [pltpu.VMEM((w, d), table.dtype)] * windows + [dma] * (2 * windows),
               name="moe_gather_sc")
    def run(x_hbm, i_hbm, o_hbm, i_v, *rest):
        bufs, sems = rest[:windows], rest[windows:]
        wid = lax.axis_index("core") * (SC_WORKERS // 2) + lax.axis_index("subcore")
        slot = wid // workers_per_slot
        src0 = tok0 + (wid % workers_per_slot) * per_worker

        @pl.loop(0, chunks)
        def _(c):
            pltpu.sync_copy(i_hbm.at[slot, pl.ds(src0 + c * SC_INDEX_CHUNK, SC_INDEX_CHUNK)], i_v)
            row0 = wid * per_worker + c * SC_INDEX_CHUNK
            gathers = [pltpu.make_async_copy(x_hbm.at[i_v.at[pl.ds(k * w, w)]], bufs[k], sems[2 * k])
                       for k in range(windows)]
            for cp in gathers:
                cp.start()
            stores = []
            for k in range(windows):
                gathers[k].wait()
                cp = pltpu.make_async_copy(bufs[k], o_hbm.at[pl.ds(row0 + k * w, w)], sems[2 * k + 1])
                cp.start()
                stores.append(cp)
            for cp in stores:
                cp.wait()

    return run(table, pos)


def _expert_kernel(bexp_ref, bvalid_ref, nblk_ref, x_ref, wg_ref, wu_ref, wd_ref, y_ref, wd_s):
    jb = pl.program_id(0)
    valid = bvalid_ref[jb]

    @pl.when((jb == 0) | (bexp_ref[jb] != bexp_ref[jnp.maximum(jb - 1, 0)]))
    def _():
        wd_s[...] = wd_ref[0].astype(BF16)

    subs =[slice(r0, r0 + EXPERT_SUB) for r0 in range(0, EXPERT_ROWS, EXPERT_SUB)]
    passes = (valid + (EXPERT_SUB - 1)) // EXPERT_SUB

    def run(n_live):
        halves = []
        for rows in subs[:n_live]:
            live = lax.broadcasted_iota(I32, (EXPERT_SUB, 1), 0) + rows.start < valid
            lo, hi = _unpack_pair(jnp.where(live, x_ref[rows, :], jnp.uint32(0)))
            halves.append((lo.astype(BF16), hi.astype(BF16)))

        def project(w_ref):
            return [jnp.dot(lo, w_ref[0, 0:D_PACK, :], preferred_element_type=F32)
                    + jnp.dot(hi, w_ref[0, D_PACK:D_MODEL, :], preferred_element_type=F32) for lo, hi in halves]

        hidden = [(_silu(g) * u).astype(BF16) for g, u in zip(project(wg_ref), project(wu_ref))]
        outs = [jnp.dot(hb, wd_s[...], preferred_element_type=F32) for hb in hidden]
        for rows, y in zip(subs, outs):
            y_ref[rows, :] = _pack_pair(y[:, 0:D_PACK], y[:, D_PACK:D_MODEL])
        for rows in subs[n_live:]:
            y_ref[rows, :] = jnp.zeros((EXPERT_SUB, D_PACK), U32)

    for n_live in range(len(subs) + 1):
        pl.when(passes == n_live)(lambda n_live=n_live: run(n_live))


def _experts(bexp, bvalid, nblk, xs, w_gate, w_up, w_down):
    n_rows, dp = xs.shape
    d = 2 * dp
    n_blocks = n_rows // EXPERT_ROWS
    return pl.pallas_call(
        _expert_kernel,
        grid_spec=pltpu.PrefetchScalarGridSpec(
            num_scalar_prefetch=3,
            grid=(n_blocks,),
            in_specs=[
                pl.BlockSpec((EXPERT_ROWS, dp), lambda jb, be, bv, nb: (jnp.minimum(jb, nb[0] - 1), 0)),
                pl.BlockSpec((1, d, D_EXPERT), lambda jb, be, bv, nb: (be[jb], 0, 0)),
                pl.BlockSpec((1, d, D_EXPERT), lambda jb, be, bv, nb: (be[jb], 0, 0)),
                pl.BlockSpec((1, D_EXPERT, d), lambda jb, be, bv, nb: (be[jb], 0, 0)),
            ],
            out_specs=pl.BlockSpec((EXPERT_ROWS, dp), lambda jb, be, bv, nb: (jb, 0)),
            scratch_shapes=[pltpu.VMEM((D_EXPERT, d), BF16)],
        ),
        out_shape=jax.ShapeDtypeStruct((n_rows, dp), U32),
        compiler_params=pltpu.CompilerParams(dimension_semantics=("arbitrary",)),
        name="moe_experts",
    )(bexp, bvalid, nblk, xs, w_gate, w_up, w_down)


def _combine_kernel(rw_ref, x1_ref, gt2_ref, gpost_ref, ya_ref, yb_ref, o_ref):
    wt = rw_ref[0].T
    a_lo, a_hi = _unpack_pair(ya_ref[...])
    b_lo, b_hi = _unpack_pair(yb_ref[...])
    y_lo = a_lo * wt[:, 0:1] + b_lo * wt[:, 1:2]
    y_hi = a_hi * wt[:, 0:1] + b_hi * wt[:, 1:2]
    ssq = jnp.sum(y_lo * y_lo, axis=-1, keepdims=True) + jnp.sum(y_hi * y_hi, axis=-1, keepdims=True)
    inv = lax.rsqrt(ssq * (1.0 / D_MODEL) + EPS)
    gate = gt2_ref[0] * gpost_ref[...]
    o_ref[:, 0:D_PACK] = x1_ref[:, 0:D_PACK] + y_lo * inv * gate[:, 0:D_PACK]
    o_ref[:, D_PACK:D_MODEL] = x1_ref[:, D_PACK:D_MODEL] + y_hi * inv * gate[:, D_PACK:D_MODEL]


def _combine_kernel_into(prev_ref, *refs):
    del prev_ref
    _combine_kernel(*refs)


def _combine(rw, x1, gt2, g_post, yg, seq, piece, prev):
    n, d = x1.shape
    t = SEQ_TILE
    per_batch = seq // t
    tiles = yg.shape[0] // (TOP_K * t)
    tile0 = piece * tiles
    in_specs = [
        pl.BlockSpec((1, SUBLANES, t), lambda i: (i + tile0, 0, 0)),
        pl.BlockSpec((t, d), lambda i: (i + tile0, 0)),
        pl.BlockSpec((1, 1, d), lambda i: ((i + tile0) // per_batch, 0, 0)),
        pl.BlockSpec((1, d), lambda i: (0, 0)),
        pl.BlockSpec((t, D_PACK), lambda i: (i, 0)),
        pl.BlockSpec((t, D_PACK), lambda i: (i + tiles, 0)),
    ]
    args = (rw, x1, gt2, g_post, yg, yg)
    body, aliases = _combine_kernel, {}
    if prev is not None:
        in_specs = [pl.BlockSpec(memory_space=pl.ANY)] + in_specs
        args = (prev,) + args
        body, aliases = _combine_kernel_into, {0: 0}
    return pl.pallas_call(
        body,
        grid=(tiles,),
        in_specs=in_specs,
        out_specs=pl.BlockSpec((t, d), lambda i: (i + tile0, 0)),
        out_shape=jax.ShapeDtypeStruct((n, d), F32),
        input_output_aliases=aliases,
        compiler_params=pltpu.CompilerParams(dimension_semantics=("arbitrary",)),
        name="moe_combine",
    )(*args)


def kernel(x, c, w_ada, b_ada, g_pre_mix, g_post_mix, w_in, dw_kernel, dw_bias, conv_ln_gain, conv_ln_bias, lb_logits, rec_norm_gain, w_out, g_pre_ffn, g_post_ffn, w_router_group, b_router_group, w_router_expert, b_router_expert, w_gate, w_up, w_down):
    bsz, s, d = x.shape
    depth = w_ada.shape[0]
    assert depth == 1 and lb_logits.shape[0] == 2
    n_tok = bsz * s
    n_rows = n_tok * TOP_K + N_EXPERTS * EXPERT_ROWS
    for l in range(depth):
        mod = _ada(c, w_ada[l], b_ada[l])
        mod3 = mod.reshape(bsz, 6, d)
        pad = LANES - N_EXPERTS - N_GROUPS
        w_r = jnp.concatenate([w_router_expert[l], w_router_group[l], jnp.zeros((d, pad), F32)], axis=1)
        w_r_hi = w_r.astype(BF16)
        w_r = jnp.concatenate([w_r_hi, (w_r - w_r_hi.astype(F32)).astype(BF16)], axis=1)
        b_r = jnp.concatenate([b_router_expert[l], b_router_group[l], jnp.zeros((pad,), F32)]).reshape(1, LANES)
        x1, h2p, ri, rw, cnt = _mix(
            x, mod3, g_pre_mix[l].reshape(1, d), g_post_mix[l].reshape(1, d), g_pre_ffn[l].reshape(1, d),
            w_in[l].astype(BF16), dw_kernel[l], dw_bias[l].reshape(1, D_CONV),
            conv_ln_gain[l].reshape(1, D_CONV), conv_ln_bias[l].reshape(1, D_CONV),
            lb_logits, rec_norm_gain[l].reshape(1, REC_DV), w_out[l].astype(BF16), w_r, b_r)
        counts = cnt[:, 0].astype(I32)
        starts, bexp, bvalid, nblk = _meta(counts, n_rows // EXPERT_ROWS)
        pos = _positions(starts, ri)
        xs = _sc_dispatch(h2p.reshape(n_tok, D_PACK), pos, n_rows)
        ys = _experts(bexp, bvalid, nblk, xs, w_gate[l].astype(BF16), w_up[l].astype(BF16), w_down[l])
        piece_tok = n_tok // COMBINE_PIECES
        out = None
        for p in range(COMBINE_PIECES):
            yg = _sc_gather(ys, pos, p * piece_tok, piece_tok)
            out = _combine(rw, x1.reshape(n_tok, d), mod3[:, 5:6, :], g_post_ffn[l].reshape(1, d), yg, s, p, out)
        x = out.reshape(bsz, s, d)
    return x
```

```python
import jax
import jax.numpy as jnp
from jax import lax
from jax.experimental import pallas as pl
from jax.experimental.pallas import tpu as pltpu
from jax.experimental.pallas import tpu_sc as plsc

D_MODEL = 1024
D_CONV = 512
D_REC = 512
CONV_WIDTH = 31
REC_HEADS = 4
REC_DK = 128
REC_DV = 128
CHUNK = 64
D_IN = 2 * D_CONV + 4 * D_REC
N_GROUPS = 4
EXPERTS_PER_GROUP = 8
N_EXPERTS = 32
TOP_K = 2
D_EXPERT = 256
EPS = 1e-6

LANES = 128
SUBLANES = 8
SEQ_TILE = 1024
CONV_ROWS = 64
STAT_ROWS = 64
CONV_HALO = 32
GROUP_ROW0 = 32
EXPERT_ROWS = 1024
COMBINE_PIECES = 8
EXPERT_SUB = 256
SC_WINDOW = 64
SC_WORKERS = 32
SC_INDEX_CHUNK = 128
D_PACK = D_MODEL // 2
VMEM_LIMIT = 56 * 1024 * 1024

F32 = jnp.float32
BF16 = jnp.bfloat16
I32 = jnp.int32
U32 = jnp.uint32
HI = lax.Precision.HIGHEST


def _sigmoid(v):
    return 0.5 * jnp.tanh(0.5 * v) + 0.5


def _silu(v):
    return v * _sigmoid(v)


def _rms(v, gain):
    return v * lax.rsqrt(jnp.mean(v * v, axis=-1, keepdims=True) + EPS) * gain


def _pack_pair(lo, hi):
    lo_bits = lax.bitcast_convert_type(lo.astype(BF16).astype(F32), U32)
    hi_bits = lax.bitcast_convert_type(hi.astype(BF16).astype(F32), U32)
    return (lo_bits >> 16) | (hi_bits & jnp.uint32(0xFFFF0000))


def _unpack_pair(packed):
    lo = lax.bitcast_convert_type(packed << 16, F32)
    hi = lax.bitcast_convert_type(packed & jnp.uint32(0xFFFF0000), F32)
    return lo, hi


def _ada_kernel(c_ref, w_ref, b_ref, o_ref):
    cond = _silu(c_ref[...])
    o_ref[...] = jnp.dot(cond, w_ref[...], precision=HI, preferred_element_type=F32) + b_ref[...]


def _ada(c, w, b):
    bsz, d = c.shape
    n = w.shape[1]
    return pl.pallas_call(
        _ada_kernel,
        grid=(n // d,),
        in_specs=[
            pl.BlockSpec((bsz, d), lambda j: (0, 0)),
            pl.BlockSpec((d, d), lambda j: (0, j)),
            pl.BlockSpec((1, d), lambda j: (0, j)),
        ],
        out_specs=pl.BlockSpec((bsz, d), lambda j: (0, j)),
        out_shape=jax.ShapeDtypeStruct((bsz, n), F32),
        name="ada_mod",
    )(c, w, b.reshape(1, n))


def _mix_kernel(x_ref, mod_ref, gpre_ref, gpost_ref, gffn_ref, win_ref, dwk_ref, dwb_ref,
                lng_ref, lnb_ref, lbl_ref, rng_ref, wout_ref, wr_ref, br_ref, upper_ref,
                x1_ref, h2p_ref, ri_ref, rw_ref, cnt_ref,
                vec_s, qf_s, sg_s, ubuf, cv_s, lfp_s, k_s, qd_s, ki_s, ke_s, v_s, a_s, o_s, yb,
                state, carry):
    b = pl.program_id(0)
    j = pl.program_id(1)
    t = SEQ_TILE
    n_chunks = t // CHUNK
    heads = range(REC_HEADS)

    @pl.when(j == 0)
    def _():
        ubuf[0:CONV_HALO, :] = jnp.zeros((CONV_HALO, D_CONV), F32)
        state[...] = jnp.zeros(state.shape, F32)

    @pl.when((j == 0) & (b == 0))
    def _():
        carry[...] = jnp.zeros(carry.shape, F32)

    sh1 = mod_ref[0, 0:1, :]
    sc1 = mod_ref[0, 1:2, :]
    gt1 = mod_ref[0, 2:3, :]
    sh2 = mod_ref[0, 3:4, :]
    sc2 = mod_ref[0, 4:5, :]

    def row_loop(rows_per_step, body):
        def step(ci, c_):
            body(pl.ds(pl.multiple_of(ci * rows_per_step, rows_per_step), rows_per_step))
            return c_
        lax.fori_loop(0, t // rows_per_step, step, 0)

    stat_blocks = [slice(r0, r0 + STAT_ROWS) for r0 in range(0, t, STAT_ROWS)]

    def head_cols(base, hd):
        return slice(base + hd * REC_DK, base + (hd + 1) * REC_DK)

    vec_s[0:1, :] = gpre_ref[...] * (1.0 + sc1)
    vec_s[1:2, :] = sh1
    vec_s[2:3, :] = gt1 * gpost_ref[...]
    vec_s[3:4, :] = gffn_ref[...] * (1.0 + sc2)
    vec_s[4:5, :] = sh2

    xv = x_ref[0]
    hb = (xv * lax.rsqrt(jnp.mean(xv * xv, axis=-1, keepdims=True) + EPS) * vec_s[0:1, :] + vec_s[1:2, :]).astype(BF16)
    q0 = 2 * D_CONV
    f0 = q0 + D_REC
    i0 = f0 + D_REC
    g0 = i0 + D_REC

    def proj(c0, width):
        return jnp.dot(hb, win_ref[:, c0:c0 + width], preferred_element_type=F32)

    conv_in = proj(0, 2 * D_CONV)
    ubuf[CONV_HALO:CONV_HALO + t, :] = conv_in[:, 0:D_CONV] * _sigmoid(conv_in[:, D_CONV:2 * D_CONV])

    lead = CONV_HALO - (CONV_WIDTH - 1)
    win_rows = CONV_ROWS + CONV_HALO

    def conv(rows):
        for lt in range(D_CONV // LANES):
            lanes = slice(lt * LANES, (lt + 1) * LANES)
            win = ubuf[pl.ds(rows.start, win_rows), lanes]
            acc = jnp.broadcast_to(dwb_ref[:, lanes], (CONV_ROWS, LANES))
            for res in range(SUBLANES):
                shifted = win if res == 0 else pltpu.roll(win, win_rows - res, axis=0)
                for al in range(0, win_rows - CONV_ROWS + 1, SUBLANES):
                    kk = al + res - lead
                    if 0 <= kk < CONV_WIDTH and al + CONV_ROWS + res <= win_rows:
                        acc = acc + shifted[al:al + CONV_ROWS] * dwk_ref[kk:kk + 1, lanes]
            cv_s[rows, lanes] = acc

    row_loop(CONV_ROWS, conv)
    ubuf[0:CONV_HALO, :] = ubuf[t:t + CONV_HALO, :]

    conv_out = cv_s[...]
    centred = conv_out - jnp.mean(conv_out, axis=-1, keepdims=True)
    normed = centred * lax.rsqrt(jnp.mean(centred * centred, axis=-1, keepdims=True) + EPS)
    y_conv = jnp.dot(_silu(normed * lng_ref[...] + lnb_ref[...]).astype(BF16), wout_ref[0:D_CONV, :],
                     preferred_element_type=F32)

    l0 = lbl_ref[0:1, :]
    lmax = jnp.max(lbl_ref[...], axis=0, keepdims=True)
    lb = jnp.exp(l0 - lmax) / jnp.sum(jnp.exp(lbl_ref[...] - lmax), axis=0, keepdims=True)

    qf_s[...] = _silu(proj(q0, D_REC))
    forget = lb + (1.0 - lb) * _sigmoid(proj(f0, D_REC))
    k_s[...] = 1.0 - forget
    lf = jnp.log(forget)
    hi = lf.astype(BF16)
    rem = lf - hi.astype(F32)
    mid = rem.astype(BF16)
    lfp_s[0] = hi
    lfp_s[1] = mid
    lfp_s[2] = (rem - mid.astype(F32)).astype(BF16)
    v_s[...] = proj(i0, D_REC).astype(BF16)
    sg_s[...] = _silu(proj(g0, D_REC))

    row = lax.broadcasted_iota(I32, (CHUNK, CHUNK), 0)
    col = lax.broadcasted_iota(I32, (CHUNK, CHUNK), 1)
    causal = row >= col
    tri = jnp.where(causal, 1.0, 0.0).astype(BF16)
    for ci in range(n_chunks):
        rows = slice(ci * CHUNK, (ci + 1) * CHUNK)
        bcum = (jnp.dot(tri, lfp_s[0, rows, :], preferred_element_type=F32)
                + jnp.dot(tri, lfp_s[1, rows, :], preferred_element_type=F32)
                + jnp.dot(tri, lfp_s[2, rows, :], preferred_element_type=F32))
        a_last = jnp.exp(bcum[CHUNK - 1:CHUNK, :])
        k_inv = k_s[rows, :] * jnp.exp(-bcum)
        qd_s[rows, :] = (qf_s[rows, :] * jnp.exp(bcum)).astype(BF16)
        ki_s[rows, :] = k_inv.astype(BF16)
        ke_s[rows, :] = (k_inv * a_last).astype(BF16)
        a_s[ci] = a_last

    nt_dims = (((1,), (1,)), ((), ()))
    pairs = [(ci, hd) for ci in range(n_chunks) for hd in heads]

    def blk(ci, hd):
        return slice(ci * CHUNK, (ci + 1) * CHUNK), head_cols(0, hd)

    scores = {}
    for p in pairs:
        rows, cols = blk(*p)
        sc = lax.dot_general(qd_s[rows, cols], ki_s[rows, cols], nt_dims, preferred_element_type=F32)
        scores[p] = jnp.where(causal, sc, 0.0).astype(BF16)
    for p in pairs:
        rows, cols = blk(*p)
        o_s[rows, cols] = jnp.dot(scores[p], v_s[rows, cols], preferred_element_type=F32)
    upd = {}
    for p in pairs:
        rows, cols = blk(*p)
        v_t = v_s[rows, cols].astype(F32).T.astype(BF16)
        upd[p] = jnp.dot(v_t, ke_s[rows, cols], preferred_element_type=F32)
    prev = {}
    for hd in heads:
        st = state[hd]
        for ci in range(n_chunks):
            prev[(ci, hd)] = st.astype(BF16)
            st = st * a_s[ci, :, head_cols(0, hd)] + upd[(ci, hd)]
        state[hd] = st
    for p in pairs:
        rows, cols = blk(*p)
        o_s[rows, cols] += lax.dot_general(qd_s[rows, cols], prev[p], nt_dims, preferred_element_type=F32)

    for rows in stat_blocks:
        for hd in heads:
            o = _rms(o_s[rows, head_cols(0, hd)], rng_ref[...]) * sg_s[rows, head_cols(0, hd)]
            yb[rows, head_cols(0, hd)] = o.astype(BF16)

    y = y_conv + jnp.dot(yb[...], wout_ref[D_CONV:D_MODEL, :], preferred_element_type=F32)
    x1 = x_ref[0] + y * lax.rsqrt(jnp.mean(y * y, axis=-1, keepdims=True) + EPS) * vec_s[2:3, :]
    x1_ref[0] = x1
    h2v = x1 * lax.rsqrt(jnp.mean(x1 * x1, axis=-1, keepdims=True) + EPS) * vec_s[3:4, :] + vec_s[4:5, :]
    h2p_ref[0] = _pack_pair(h2v[:, 0:D_PACK], h2v[:, D_PACK:D_MODEL])

    h_hi = h2v.astype(BF16)
    h_lo = (h2v - h_hi.astype(F32)).astype(BF16)
    both = jnp.dot(h_hi, wr_ref[...], preferred_element_type=F32)
    logits = (both[:, 0:LANES] + both[:, LANES:2 * LANES]
              + jnp.dot(h_lo, wr_ref[:, 0:LANES], preferred_element_type=F32)) + br_ref[...]
    lt = logits.T
    neg = jnp.float32(-jnp.inf)
    r8 = lax.broadcasted_iota(I32, (SUBLANES, t), 0)
    gl = jnp.where(r8 < N_GROUPS, lt[GROUP_ROW0:GROUP_ROW0 + SUBLANES], neg)
    gmax = jnp.max(gl, axis=0, keepdims=True)
    gidx = jnp.min(jnp.where(gl == gmax, r8, SUBLANES), axis=0, keepdims=True)
    gprob = 1.0 / jnp.sum(jnp.exp(gl - gmax), axis=0, keepdims=True)
    re = lax.broadcasted_iota(I32, (N_EXPERTS, t), 0)
    el = jnp.where((re // EXPERTS_PER_GROUP) == gidx, lt[0:N_EXPERTS], neg)
    m1 = jnp.max(el, axis=0, keepdims=True)
    i1 = jnp.min(jnp.where(el == m1, re, N_EXPERTS), axis=0, keepdims=True)
    el2 = jnp.where(re == i1, neg, el)
    m2 = jnp.max(el2, axis=0, keepdims=True)
    i2 = jnp.min(jnp.where(el2 == m2, re, N_EXPERTS), axis=0, keepdims=True)
    r = jnp.exp(m2 - m1)
    w1 = gprob / (1.0 + r)
    w2 = gprob * r / (1.0 + r)
    hot1 = re == i1
    hot2 = re == i2
    hot = jnp.where(hot1 | hot2, 1.0, 0.0)
    prefix = jnp.dot(hot.astype(BF16), upper_ref[...], preferred_element_type=F32) + carry[...]
    rank1 = jnp.sum(jnp.where(hot1, prefix, 0.0), axis=0, keepdims=True)
    rank2 = jnp.sum(jnp.where(hot2, prefix, 0.0), axis=0, keepdims=True)
    carry[...] = carry[...] + jnp.sum(hot, axis=1, keepdims=True)
    zi = jnp.zeros((SUBLANES - 4, t), I32)
    ri_ref[0] = jnp.concatenate([i1, i2, rank1.astype(I32), rank2.astype(I32), zi], axis=0)
    rw_ref[0] = jnp.concatenate([w1, w2, jnp.zeros((SUBLANES - 2, t), F32)], axis=0)
    cnt_ref[...] = jnp.broadcast_to(carry[...], cnt_ref.shape)


def _mix(x, mod3, g_pre, g_post, g_ffn, w_in, dwk, dwb, lng, lnb, lbl, rng, w_out, w_r, b_r):
    bsz, s, d = x.shape
    t = SEQ_TILE
    nt = s // t
    tile = lambda b, j: (b, j, 0)
    rtile = lambda b, j: (b * nt + j, 0, 0)
    const2 = lambda b, j: (0, 0)

    def const_spec(shape):
        return pl.BlockSpec(shape, const2, pipeline_mode=pl.Buffered(1))

    upper = jnp.triu(jnp.ones((t, t), BF16), k=1)
    return pl.pallas_call(
        _mix_kernel,
        grid=(bsz, nt),
        in_specs=[
            pl.BlockSpec((1, t, d), tile),
            pl.BlockSpec((1, 6, d), lambda b, j: (b, 0, 0)),
            const_spec((1, d)),
            const_spec((1, d)),
            const_spec((1, d)),
            const_spec((d, D_IN)),
            const_spec((CONV_WIDTH, D_CONV)),
            const_spec((1, D_CONV)),
            const_spec((1, D_CONV)),
            const_spec((1, D_CONV)),
            const_spec((2, D_REC)),
            const_spec((1, REC_DV)),
            const_spec((d, d)),
            const_spec((d, 2 * LANES)),
            const_spec((1, LANES)),
            const_spec((t, t)),
        ],
        out_specs=[
            pl.BlockSpec((1, t, d), tile),
            pl.BlockSpec((1, t, D_PACK), tile),
            pl.BlockSpec((1, SUBLANES, t), rtile),
            pl.BlockSpec((1, SUBLANES, t), rtile),
            pl.BlockSpec((N_EXPERTS, LANES), const2),
        ],
        out_shape=[
            jax.ShapeDtypeStruct((bsz, s, d), F32),
            jax.ShapeDtypeStruct((bsz, s, D_PACK), U32),
            jax.ShapeDtypeStruct((bsz * nt, SUBLANES, t), I32),
            jax.ShapeDtypeStruct((bsz * nt, SUBLANES, t), F32),
            jax.ShapeDtypeStruct((N_EXPERTS, LANES), F32),
        ],
        scratch_shapes=[
            pltpu.VMEM((SUBLANES, d), F32),
            pltpu.VMEM((t, D_REC), F32),
            pltpu.VMEM((t, D_REC), F32),
            pltpu.VMEM((CONV_HALO + t, D_CONV), F32),
            pltpu.VMEM((t, D_CONV), F32),
            pltpu.VMEM((3, t, D_REC), BF16),
            pltpu.VMEM((t, D_REC), F32),
            pltpu.VMEM((t, D_REC), BF16),
            pltpu.VMEM((t, D_REC), BF16),
            pltpu.VMEM((t, D_REC), BF16),
            pltpu.VMEM((t, D_REC), BF16),
            pltpu.VMEM((t // CHUNK, 1, D_REC), F32),
            pltpu.VMEM((t, D_REC), F32),
            pltpu.VMEM((t, D_REC), BF16),
            pltpu.VMEM((REC_HEADS, REC_DV, REC_DK), F32),
            pltpu.VMEM((N_EXPERTS, 1), F32),
        ],
        compiler_params=pltpu.CompilerParams(
            dimension_semantics=("arbitrary", "arbitrary"),
            vmem_limit_bytes=VMEM_LIMIT),
        name="mixer",
    )(x, mod3, g_pre, g_post, g_ffn, w_in, dwk, dwb, lng, lnb, lbl, rng, w_out, w_r, b_r, upper)


def _meta_kernel(cnt_ref, start_ref, bexp_ref, bvalid_ref, nblk_ref):
    shift = EXPERT_ROWS.bit_length() - 1
    n_blocks = bexp_ref.shape[0]

    def fill(e, blk0, cnt):
        def body(jb, c_):
            bexp_ref[jb] = e
            bvalid_ref[jb] = jnp.clip(cnt - ((jb - blk0) << shift), 0, EXPERT_ROWS)
            return c_
        return body

    def per_expert(e, blk0):
        nb = (cnt_ref[e] + (EXPERT_ROWS - 1)) >> shift
        start_ref[e] = blk0 << shift
        lax.fori_loop(blk0, blk0 + nb, fill(e, blk0, cnt_ref[e]), 0)
        return blk0 + nb

    used = lax.fori_loop(0, N_EXPERTS, per_expert, jnp.int32(0))
    lax.fori_loop(used, n_blocks, fill(N_EXPERTS - 1, used, 0), 0)
    nblk_ref[0] = used


def _meta(counts, n_blocks):
    smem = pl.BlockSpec(memory_space=pltpu.SMEM)
    return pl.pallas_call(
        _meta_kernel,
        in_specs=[smem],
        out_specs=[smem, smem, smem, smem],
        out_shape=[
            jax.ShapeDtypeStruct((N_EXPERTS,), I32),
            jax.ShapeDtypeStruct((n_blocks,), I32),
            jax.ShapeDtypeStruct((n_blocks,), I32),
            jax.ShapeDtypeStruct((1,), I32),
        ],
        name="moe_layout",
    )(counts)


def _positions_kernel(start_ref, ri_ref, pos_ref):
    e = ri_ref[:, 0:TOP_K, :]
    seg = jnp.zeros(e.shape, I32)
    for k in range(N_EXPERTS):
        seg = jnp.where(e == k, start_ref[k], seg)
    pos = seg + ri_ref[:, TOP_K:2 * TOP_K, :]
    for k in range(TOP_K):
        pos_ref[k] = pos[:, k, :]


def _positions(starts, ri):
    n_tiles, _, t = ri.shape
    tb = SUBLANES
    return pl.pallas_call(
        _positions_kernel,
        grid_spec=pltpu.PrefetchScalarGridSpec(
            num_scalar_prefetch=1,
            grid=(n_tiles // tb,),
            in_specs=[pl.BlockSpec((tb, SUBLANES, t), lambda i, s_: (i, 0, 0))],
            out_specs=pl.BlockSpec((TOP_K, tb, t), lambda i, s_: (0, i, 0)),
        ),
        out_shape=jax.ShapeDtypeStruct((TOP_K, n_tiles, t), I32),
        name="moe_positions",
    )(starts, ri).reshape(TOP_K, n_tiles * t)


_SC_MESH = dict(core_axis_name="core", subcore_axis_name="subcore")


def _sc_worker_base(rows_per_worker):
    wid = lax.axis_index("core") * (SC_WORKERS // 2) + lax.axis_index("subcore")
    return wid * rows_per_worker


def _sc_dispatch(rows_in, pos, n_rows):
    n, d = rows_in.shape
    w = SC_WINDOW
    per_worker = n // SC_WORKERS
    chunks = per_worker // SC_INDEX_CHUNK
    windows = SC_INDEX_CHUNK // w
    dma = pltpu.SemaphoreType.DMA

    @pl.kernel(out_type=jax.ShapeDtypeStruct((n_rows, d), rows_in.dtype),
               mesh=plsc.VectorSubcoreMesh(**_SC_MESH),
               scratch_types=[pltpu.VMEM((per_worker,), I32), pltpu.VMEM((per_worker,), I32)]
               + [pltpu.VMEM((w, d), rows_in.dtype)] * windows + [dma] * (3 * windows),
               name="moe_dispatch_sc")
    def run(x_hbm, i_hbm, o_hbm, ia_v, ib_v, *rest):
        bufs, sems = rest[:windows], rest[windows:]
        base = _sc_worker_base(per_worker)
        pltpu.sync_copy(i_hbm.at[0, pl.ds(base, per_worker)], ia_v)
        pltpu.sync_copy(i_hbm.at[1, pl.ds(base, per_worker)], ib_v)

        @pl.loop(0, chunks)
        def _(c):
            off = c * SC_INDEX_CHUNK
            loads = [pltpu.make_async_copy(x_hbm.at[pl.ds(base + off + k * w, w)], bufs[k], sems[3 * k])
                     for k in range(windows)]
            for cp in loads:
                cp.start()
            stores = []
            for k in range(windows):
                loads[k].wait()
                for idx_v, sem in ((ia_v, sems[3 * k + 1]), (ib_v, sems[3 * k + 2])):
                    cp = pltpu.make_async_copy(bufs[k], o_hbm.at[idx_v.at[pl.ds(off + k * w, w)]], sem)
                    cp.start()
                    stores.append(cp)
            for cp in stores:
                cp.wait()

    return run(rows_in, pos)


def _sc_gather(table, pos, tok0, n_piece):
    m = TOP_K * n_piece
    d = table.shape[1]
    w = SC_WINDOW
    per_worker = m // SC_WORKERS
    chunks = per_worker // SC_INDEX_CHUNK
    windows = SC_INDEX_CHUNK // w
    workers_per_slot = SC_WORKERS // TOP_K
    dma = pltpu.SemaphoreType.DMA

    @pl.kernel(out_type=jax.ShapeDtypeStruct((m, d), table.dtype),
               mesh=plsc.VectorSubcoreMesh(**_SC_MESH),
               scratch_types=[pltpu.VMEM((per_worker,), I32)]
               + [pltpu.VMEM((w, d), table.dtype)] * windows + [dma] * (2 * windows),
               name="moe_gather_sc")
    def run(x_hbm, i_hbm, o_hbm, i_v, *rest):
        bufs, sems = rest[:windows], rest[windows:]
        wid = lax.axis_index("core") * (SC_WORKERS // 2) + lax.axis_index("subcore")
        slot = wid // workers_per_slot
        src0 = tok0 + (wid % workers_per_slot) * per_worker
        pltpu.sync_copy(i_hbm.at[slot, pl.ds(src0, per_worker)], i_v)

        @pl.loop(0, chunks)
        def _(c):
            off = c * SC_INDEX_CHUNK
            row0 = wid * per_worker + off
            gathers = [pltpu.make_async_copy(x_hbm.at[i_v.at[pl.ds(off + k * w, w)]], bufs[k], sems[2 * k])
                       for k in range(windows)]
            for cp in gathers:
                cp.start()
            stores = []
            for k in range(windows):
                gathers[k].wait()
                cp = pltpu.make_async_copy(bufs[k], o_hbm.at[pl.ds(row0 + k * w, w)], sems[2 * k + 1])
                cp.start()
                stores.append(cp)
            for cp in stores:
                cp.wait()

    return run(table, pos)


def _expert_kernel(bexp_ref, bvalid_ref, nblk_ref, x_ref, wg_ref, wu_ref, wd_ref, y_ref, wg_s, wu_s, wd_s):
    jb = pl.program_id(0)
    valid = bvalid_ref[jb]

    @pl.when((jb == 0) | (bexp_ref[jb] != bexp_ref[jnp.maximum(jb - 1, 0)]))
    def _():
        wg_s[...] = wg_ref[0].astype(BF16)
        wu_s[...] = wu_ref[0].astype(BF16)
        wd_s[...] = wd_ref[0].astype(BF16)

    subs = [slice(r0, r0 + EXPERT_SUB) for r0 in range(0, EXPERT_ROWS, EXPERT_SUB)]
    passes = (valid + (EXPERT_SUB - 1)) // EXPERT_SUB

    def run(n_live):
        halves = []
        for rows in subs[:n_live]:
            live = lax.broadcasted_iota(I32, (EXPERT_SUB, 1), 0) + rows.start < valid
            lo, hi = _unpack_pair(jnp.where(live, x_ref[rows, :], jnp.uint32(0)))
            halves.append((lo.astype(BF16), hi.astype(BF16)))

        def project(w_s):
            return [jnp.dot(lo, w_s[0:D_PACK, :], preferred_element_type=F32)
                    + jnp.dot(hi, w_s[D_PACK:D_MODEL, :], preferred_element_type=F32) for lo, hi in halves]

        hidden = [(_silu(g) * u).astype(BF16) for g, u in zip(project(wg_s), project(wu_s))]
        outs = [jnp.dot(hb, wd_s[...], preferred_element_type=F32) for hb in hidden]
        for rows, y in zip(subs, outs):
            y_ref[rows, :] = _pack_pair(y[:, 0:D_PACK], y[:, D_PACK:D_MODEL])
        for rows in subs[n_live:]:
            y_ref[rows, :] = jnp.zeros((EXPERT_SUB, D_PACK), U32)

    for n_live in range(len(subs) + 1):
        pl.when(passes == n_live)(lambda n_live=n_live: run(n_live))


def _experts(bexp, bvalid, nblk, xs, w_gate, w_up, w_down):
    n_rows, dp = xs.shape
    d = 2 * dp
    n_blocks = n_rows // EXPERT_ROWS
    return pl.pallas_call(
        _expert_kernel,
        grid_spec=pltpu.PrefetchScalarGridSpec(
            num_scalar_prefetch=3,
            grid=(n_blocks,),
            in_specs=[
                pl.BlockSpec((EXPERT_ROWS, dp), lambda jb, be, bv, nb: (jnp.minimum(jb, nb[0] - 1), 0)),
                pl.BlockSpec((1, d, D_EXPERT), lambda jb, be, bv, nb: (be[jb], 0, 0)),
                pl.BlockSpec((1, d, D_EXPERT), lambda jb, be, bv, nb: (be[jb], 0, 0)),
                pl.BlockSpec((1, D_EXPERT, d), lambda jb, be, bv, nb: (be[jb], 0, 0)),
            ],
            out_specs=pl.BlockSpec((EXPERT_ROWS, dp), lambda jb, be, bv, nb: (jb, 0)),
            scratch_shapes=[
                pltpu.VMEM((d, D_EXPERT), BF16),
                pltpu.VMEM((d, D_EXPERT), BF16),
                pltpu.VMEM((D_EXPERT, d), BF16),
            ],
        ),
        out_shape=jax.ShapeDtypeStruct((n_rows, dp), U32),
        compiler_params=pltpu.CompilerParams(dimension_semantics=("arbitrary",)),
        name="moe_experts",
    )(bexp, bvalid, nblk, xs, w_gate, w_up, w_down)


def _combine_kernel(rw_ref, x1_ref, gt2_ref, gpost_ref, ya_ref, yb_ref, o_ref):
    wt = rw_ref[0].T
    a_lo, a_hi = _unpack_pair(ya_ref[...])
    b_lo, b_hi = _unpack_pair(yb_ref[...])
    y_lo = a_lo * wt[:, 0:1] + b_lo * wt[:, 1:2]
    y_hi = a_hi * wt[:, 0:1] + b_hi * wt[:, 1:2]
    ssq = jnp.sum(y_lo * y_lo, axis=-1, keepdims=True) + jnp.sum(y_hi * y_hi, axis=-1, keepdims=True)
    inv = lax.rsqrt(ssq * (1.0 / D_MODEL) + EPS)
    gate = gt2_ref[0] * gpost_ref[...]
    o_ref[:, 0:D_PACK] = x1_ref[:, 0:D_PACK] + y_lo * inv * gate[:, 0:D_PACK]
    o_ref[:, D_PACK:D_MODEL] = x1_ref[:, D_PACK:D_MODEL] + y_hi * inv * gate[:, D_PACK:D_MODEL]


def _combine_kernel_into(prev_ref, *refs):
    del prev_ref
    _combine_kernel(*refs)


def _combine(rw, x1, gt2, g_post, yg, seq, piece, prev):
    n, d = x1.shape
    t = SEQ_TILE
    per_batch = seq // t
    tiles = yg.shape[0] // (TOP_K * t)
    tile0 = piece * tiles
    in_specs = [
        pl.BlockSpec((1, SUBLANES, t), lambda i: (i + tile0, 0, 0)),
        pl.BlockSpec((t, d), lambda i: (i + tile0, 0)),
        pl.BlockSpec((1, 1, d), lambda i: ((i + tile0) // per_batch, 0, 0)),
        pl.BlockSpec((1, d), lambda i: (0, 0)),
        pl.BlockSpec((t, D_PACK), lambda i: (i, 0)),
        pl.BlockSpec((t, D_PACK), lambda i: (i + tiles, 0)),
    ]
    args = (rw, x1, gt2, g_post, yg, yg)
    body, aliases = _combine_kernel, {}
    if prev is not None:
        in_specs = [pl.BlockSpec(memory_space=pl.ANY)] + in_specs
        args = (prev,) + args
        body, aliases = _combine_kernel_into, {0: 0}
    return pl.pallas_call(
        body,
        grid=(tiles,),
        in_specs=in_specs,
        out_specs=pl.BlockSpec((t, d), lambda i: (i + tile0, 0)),
        out_shape=jax.ShapeDtypeStruct((n, d), F32),
        input_output_aliases=aliases,
        compiler_params=pltpu.CompilerParams(dimension_semantics=("arbitrary",)),
        name="moe_combine",
    )(*args)


def kernel(x, c, w_ada, b_ada, g_pre_mix, g_post_mix, w_in, dw_kernel, dw_bias, conv_ln_gain, conv_ln_bias, lb_logits, rec_norm_gain, w_out, g_pre_ffn, g_post_ffn, w_router_group, b_router_group, w_router_expert, b_router_expert, w_gate, w_up, w_down):
    bsz, s, d = x.shape
    depth = w_ada.shape[0]
    assert depth == 1 and lb_logits.shape[0] == 2
    n_tok = bsz * s
    n_rows = n_tok * TOP_K + N_EXPERTS * EXPERT_ROWS
    for l in range(depth):
        mod = _ada(c, w_ada[l], b_ada[l])
        mod3 = mod.reshape(bsz, 6, d)
        pad = LANES - N_EXPERTS - N_GROUPS
        w_r = jnp.concatenate([w_router_expert[l], w_router_group[l], jnp.zeros((d, pad), F32)], axis=1)
        w_r_hi = w_r.astype(BF16)
        w_r = jnp.concatenate([w_r_hi, (w_r - w_r_hi.astype(F32)).astype(BF16)], axis=1)
        b_r = jnp.concatenate([b_router_expert[l], b_router_group[l], jnp.zeros((pad,), F32)]).reshape(1, LANES)
        x1, h2p, ri, rw, cnt = _mix(
            x, mod3, g_pre_mix[l].reshape(1, d), g_post_mix[l].reshape(1, d), g_pre_ffn[l].reshape(1, d),
            w_in[l].astype(BF16), dw_kernel[l], dw_bias[l].reshape(1, D_CONV),
            conv_ln_gain[l].reshape(1, D_CONV), conv_ln_bias[l].reshape(1, D_CONV),
            lb_logits, rec_norm_gain[l].reshape(1, REC_DV), w_out[l].astype(BF16), w_r, b_r)
        counts = cnt[:, 0].astype(I32)
        starts, bexp, bvalid, nblk = _meta(counts, n_rows // EXPERT_ROWS)
        pos = _positions(starts, ri)
        xs = _sc_dispatch(h2p.reshape(n_tok, D_PACK), pos, n_rows)
        ys = _experts(bexp, bvalid, nblk, xs, w_gate[l], w_up[l], w_down[l])
        piece_tok = n_tok // COMBINE_PIECES
        out = None
        for p in range(COMBINE_PIECES):
            yg = _sc_gather(ys, pos, p * piece_tok, piece_tok)
            out = _combine(rw, x1.reshape(n_tok, d), mod3[:, 5:6, :], g_post_ffn[l].reshape(1, d), yg, s, p, out)
        x = out.reshape(bsz, s, d)
    return x
```
